```python
import math
import jax, jax.numpy as jnp
from jax import lax
import numpy as np

D_MODEL = 1024
BATCH = 4
SEQ = 8192
DEPTH = 4

N_MIXERS = 3
N_HEADS = 16
HEAD_DIM = D_MODEL // N_HEADS
D_FF = 4 * D_MODEL
NORM_EPS = 1e-6
REL_BUCKETS = 32
REL_MAX_DIST = 128
MOBA_BLOCK = 256
MOBA_TOPK = 3
MOBA_Q_CHUNK = 16
A_IN_COLS = 3 * N_HEADS * HEAD_DIM
SWA_WINDOW = 128
SWA_BLOCK = 128
SWA_KV_HEADS = 2
SWA_GROUP = N_HEADS // SWA_KV_HEADS
B_IN_COLS = (N_HEADS + 2 * SWA_KV_HEADS) * HEAD_DIM
DSA_KV_RANK = D_MODEL // 4
DSA_IDX_HEADS = 8
DSA_IDX_DIM = 64
DSA_TOPK_MAX = 256
DSA_Q_CHUNK = 128
C_SPLITS = [N_HEADS * HEAD_DIM, DSA_KV_RANK, DSA_IDX_HEADS * DSA_IDX_DIM, DSA_IDX_DIM, DSA_IDX_HEADS]
C_IN_COLS = sum(C_SPLITS)
N_A = len(range(0, DEPTH, N_MIXERS))
N_B = len(range(1, DEPTH, N_MIXERS))
N_C = len(range(2, DEPTH, N_MIXERS))

kernel_name = "hybrid_moba_swa_dsa_trunk"


def rmsnorm(x, g):
    xf = x.astype(jnp.float32)
    y = xf * lax.rsqrt(jnp.mean(xf * xf, axis=-1, keepdims=True) + NORM_EPS)
    return (y * g.astype(jnp.float32)).astype(x.dtype)


def rel_bucket(dist):
    n = jnp.maximum(dist, 0)
    max_exact = REL_BUCKETS // 2
    nf = jnp.maximum(n, 1).astype(jnp.float32)
    large = max_exact + (jnp.log(nf / max_exact) / math.log(REL_MAX_DIST / max_exact)
                         * (REL_BUCKETS - max_exact)).astype(jnp.int32)
    large = jnp.minimum(large, REL_BUCKETS - 1)
    return jnp.where(n < max_exact, n, large)


def squared_relu_mlp(h, w_up, w_down):
    return jnp.square(jax.nn.relu(h @ w_up)) @ w_down


def moba_attention(h, w_in, w_out, rel_bias):
    B, S, _ = h.shape
    H, dh, BLK, QC = N_HEADS, HEAD_DIM, MOBA_BLOCK, MOBA_Q_CHUNK
    q, k, v = jnp.split(h @ w_in, 3, axis=-1)
    to_heads = lambda t: t.reshape(B, S, H, dh).transpose(0, 2, 1, 3)
    q, k, v = to_heads(q), to_heads(k), to_heads(v)
    nb = -(-S // BLK)
    pad = ((0, 0), (0, 0), (0, nb * BLK - S), (0, 0))
    k, v = jnp.pad(k, pad), jnp.pad(v, pad)
    kb = k.reshape(B, H, nb, BLK, dh)
    vb = v.reshape(B, H, nb, BLK, dh)
    k_mean = jnp.mean(kb, axis=3)
    n_sel = min(MOBA_TOPK, nb)
    scale = dh ** -0.5
    bias_hb = rel_bias.T
    b_idx = jnp.arange(B)[:, None, None, None]
    h_idx = jnp.arange(H)[None, :, None, None]
    blk_ids = jnp.arange(nb)
    j_in = jnp.arange(BLK)

    def chunk(start):
        qc = lax.dynamic_slice_in_dim(q, start, QC, axis=2)
        t = start + jnp.arange(QC)
        qb = start // BLK
        gate = jnp.einsum('bhqd,bhnd->bhqn', qc, k_mean).astype(jnp.float32)
        gate = jnp.where(blk_ids < qb, gate, -jnp.inf)
        _, sel = lax.top_k(gate, n_sel)
        valid = sel < qb
        k_sel = kb[b_idx, h_idx, sel]
        v_sel = vb[b_idx, h_idx, sel]
        s_past = jnp.einsum('bhqd,bhqnkd->bhqnk', qc, k_sel).astype(jnp.float32) * scale
        dist_past = t[:, None, None] - (sel[..., None] * BLK + j_in)
        s_past = s_past + bias_hb[h_idx[..., None], rel_bucket(dist_past)]
        s_past = jnp.where(valid[..., None], s_past, -jnp.inf).reshape(B, H, QC, n_sel * BLK)
        k_own = lax.dynamic_slice_in_dim(k, qb * BLK, BLK, axis=2)
        v_own = lax.dynamic_slice_in_dim(v, qb * BLK, BLK, axis=2)
        s_own = jnp.einsum('bhqd,bhkd->bhqk', qc, k_own).astype(jnp.float32) * scale
        dist_own = t[:, None] - (qb * BLK + j_in)[None, :]
        s_own = s_own + rel_bias[rel_bucket(dist_own)].transpose(2, 0, 1)
        s_own = jnp.where(dist_own >= 0, s_own, -jnp.inf)
        p = jax.nn.softmax(jnp.concatenate([s_past, s_own], axis=-1), axis=-1)
        p_past = p[..., :n_sel * BLK].reshape(B, H, QC, n_sel, BLK).astype(v.dtype)
        p_own = p[..., n_sel * BLK:].astype(v.dtype)
        return (jnp.einsum('bhqnk,bhqnkd->bhqd', p_past, v_sel)
                + jnp.einsum('bhqk,bhkd->bhqd', p_own, v_own))

    o = lax.map(chunk, jnp.arange(S // QC) * QC)
    o = o.transpose(1, 0, 3, 2, 4).reshape(B, S, H * dh)
    return o @ w_out


def swa_sink_attention(h, w_in, sinks, w_out, rel_bias):
    B, S, _ = h.shape
    HKV, G, dh, BLK = SWA_KV_HEADS, SWA_GROUP, HEAD_DIM, SWA_BLOCK
    proj = h @ w_in
    nq = N_HEADS * dh
    q = proj[..., :nq].reshape(B, S, HKV, G, dh)
    k = proj[..., nq:nq + HKV * dh].reshape(B, S, HKV, dh)
    v = proj[..., nq + HKV * dh:].reshape(B, S, HKV, dh)
    nb = S // BLK
    qb = q.reshape(B, nb, BLK, HKV, G, dh)
    pad = ((0, 0), (BLK, 0), (0, 0), (0, 0))
    kp = jnp.pad(k, pad).reshape(B, nb + 1, BLK, HKV, dh)
    vp = jnp.pad(v, pad).reshape(B, nb + 1, BLK, HKV, dh)
    k_band = jnp.concatenate([kp[:, :-1], kp[:, 1:]], axis=2)
    v_band = jnp.concatenate([vp[:, :-1], vp[:, 1:]], axis=2)
    s = jnp.einsum('bnqkgd,bnskd->bnkgqs', qb, k_band).astype(jnp.float32) * dh ** -0.5
    dist = BLK + jnp.arange(BLK)[:, None] - jnp.arange(2 * BLK)[None, :]
    key_pos = jnp.arange(nb)[:, None] * BLK - BLK + jnp.arange(2 * BLK)[None, :]
    mask = ((dist >= 0) & (dist < SWA_WINDOW))[None] & (key_pos >= 0)[:, None, :]
    bias = rel_bias[rel_bucket(dist)].transpose(2, 0, 1).reshape(HKV, G, BLK, 2 * BLK)
    s = jnp.where(mask[None, :, None, None], s + bias, -jnp.inf)
    sink = jnp.broadcast_to(sinks.astype(jnp.float32).reshape(HKV, G)[None, None, :, :, None, None],
                            s.shape[:-1] + (1,))
    p = jax.nn.softmax(jnp.concatenate([s, sink], axis=-1), axis=-1)[..., :-1]
    o = jnp.einsum('bnkgqs,bnskd->bnqkgd', p.astype(v.dtype), v_band)
    return o.reshape(B, S, N_HEADS * dh) @ w_out


def dsa_attention(h, w_in, kv_norm, w_uk, w_uv, w_out, rel_bias):
    B, S, _ = h.shape
    H, dh, HI, dI, QC = N_HEADS, HEAD_DIM, DSA_IDX_HEADS, DSA_IDX_DIM, DSA_Q_CHUNK
    q, c_kv, q_idx, k_idx, w_idx = jnp.split(h @ w_in, list(np.cumsum(C_SPLITS)[:-1]), axis=-1)
    q = q.reshape(B, S, H, dh)
    c = rmsnorm(c_kv, kv_norm)
    q_lat = jnp.einsum('bthd,rhd->bthr', q, w_uk) * dh ** -0.5
    q_idx = q_idx.reshape(B, S, HI, dI)
    w_idx = w_idx * HI ** -0.5
    top_k = min(DSA_TOPK_MAX, S // 4)
    key_pos = jnp.arange(S)
    b_idx = jnp.arange(B)[:, None, None]

    def chunk(start):
        t = start + jnp.arange(QC)
        qi = lax.dynamic_slice_in_dim(q_idx, start, QC, axis=1)
        wi = lax.dynamic_slice_in_dim(w_idx, start, QC, axis=1)
        rel = jax.nn.relu(jnp.einsum('bqhd,bsd->bqhs', qi, k_idx).astype(jnp.float32) * dI ** -0.5)
        score = jnp.einsum('bqh,bqhs->bqs', wi.astype(jnp.float32), rel)
        score = jnp.where(key_pos[None, None, :] <= t[None, :, None], score, -jnp.inf)
        _, sel = lax.top_k(score, top_k)
        valid = sel <= t[None, :, None]
        c_sel = c[b_idx, sel]
        ql = lax.dynamic_slice_in_dim(q_lat, start, QC, axis=1)
        s = jnp.einsum('bqhr,bqkr->bhqk', ql, c_sel).astype(jnp.float32)
        s = s + rel_bias[rel_bucket(t[None, :, None] - sel)].transpose(0, 3, 1, 2)
        s = jnp.where(valid[:, None], s, -jnp.inf)
        p = jax.nn.softmax(s, axis=-1).astype(c.dtype)
        o_lat = jnp.einsum('bhqk,bqkr->bqhr', p, c_sel)
        return jnp.einsum('bqhr,rhd->bqhd', o_lat, w_uv)

    o = lax.map(chunk, jnp.arange(S // QC) * QC)
    o = o.transpose(1, 0, 2, 3, 4).reshape(B, S, H * dh)
    return o @ w_out


def setup_inputs(seed: int = 0) -> dict:
    key = jax.random.key(seed)
    ks = jax.random.split(key, 20)
    nrm = lambda k, shape, fan_in: jax.random.normal(k, shape, jnp.float32) * fan_in ** -0.5
    gain = lambda k, shape: 1.0 + 0.02 * jax.random.normal(k, shape, jnp.float32)
    D, HD = D_MODEL, N_HEADS * HEAD_DIM
    return {
        "x": jax.random.normal(ks[0], (BATCH, SEQ, D), jnp.float32),
        "rel_bias": 0.5 * jax.random.normal(ks[1], (REL_BUCKETS, N_HEADS), jnp.float32),
        "norm_mix": gain(ks[2], (DEPTH, D)),
        "norm_mlp": gain(ks[3], (DEPTH, D)),
        "mlp_up": nrm(ks[4], (DEPTH, D, D_FF), D),
        "mlp_down": nrm(ks[5], (DEPTH, D_FF, D), D_FF),
        "a_w_in": nrm(ks[6], (N_A, D, A_IN_COLS), D),
        "a_w_out": nrm(ks[7], (N_A, HD, D), HD),
        "b_w_in": nrm(ks[8], (N_B, D, B_IN_COLS), D),
        "b_sinks": 0.5 * jax.random.normal(ks[9], (N_B, N_HEADS), jnp.float32),
        "b_w_out": nrm(ks[10], (N_B, HD, D), HD),
        "c_w_in": nrm(ks[11], (N_C, D, C_IN_COLS), D),
        "c_kv_norm": gain(ks[12], (N_C, DSA_KV_RANK)),
        "c_w_uk": nrm(ks[13], (N_C, DSA_KV_RANK, N_HEADS, HEAD_DIM), DSA_KV_RANK),
        "c_w_uv": nrm(ks[14], (N_C, DSA_KV_RANK, N_HEADS, HEAD_DIM), DSA_KV_RANK),
        "c_w_out": nrm(ks[15], (N_C, HD, D), HD),
        "final_norm": gain(ks[16], (D,)),
    }


def reference(x, rel_bias, norm_mix, norm_mlp, mlp_up, mlp_down, a_w_in, a_w_out,
              b_w_in, b_sinks, b_w_out, c_w_in, c_kv_norm, c_w_uk, c_w_uv, c_w_out, final_norm):
    h = x
    for i in range(DEPTH):
        kind, j = i % N_MIXERS, i // N_MIXERS
        y = rmsnorm(h, norm_mix[i])
        if kind == 0:
            y = moba_attention(y, a_w_in[j], a_w_out[j], rel_bias)
        elif kind == 1:
            y = swa_sink_attention(y, b_w_in[j], b_sinks[j], b_w_out[j], rel_bias)
        else:
            y = dsa_attention(y, c_w_in[j], c_kv_norm[j], c_w_uk[j], c_w_uv[j], c_w_out[j], rel_bias)
        h = h + y
        h = h + squared_relu_mlp(rmsnorm(h, norm_mlp[i]), mlp_up[i], mlp_down[i])
    return rmsnorm(h, final_norm)
```

```python
import functools
import math

import numpy as np
import jax
import jax.numpy as jnp
from jax import lax
from jax.experimental import pallas as pl
from jax.experimental.pallas import tpu as pltpu

N_HEADS = 16
HEAD_DIM = 64
NORM_EPS = 1e-6
REL_BUCKETS = 32
REL_MAX_DIST = 128
MOBA_BLOCK = 256
MOBA_TOPK = 3
SWA_WINDOW = 128
SWA_BLOCK = 128
SWA_KV_HEADS = 2
DSA_KV_RANK = 256
DSA_IDX_HEADS = 8
DSA_IDX_DIM = 64
DSA_TOPK_MAX = 256
DSA_TILE = 256

LANES = 128
MXU_DTYPE = jnp.bfloat16
NEG = -1e30
INT_MIN = -2 ** 31
VMEM_LIMIT_BYTES = 56 * 1024 * 1024

_NT = (((1,), (1,)), ((), ()))


def _params(semantics):
    return pltpu.CompilerParams(dimension_semantics=semantics, vmem_limit_bytes=VMEM_LIMIT_BYTES)


def _rmsnorm(x, g):
    var = jnp.mean(x * x, axis=-1, keepdims=True)
    return x * lax.rsqrt(var + NORM_EPS) * g


def _norm_proj_kernel(x_ref, g_ref, w_ref, o_ref):
    xn = _rmsnorm(x_ref[...], g_ref[...]).astype(w_ref.dtype)
    o_ref[...] = jnp.dot(xn, w_ref[...], preferred_element_type=jnp.float32).astype(o_ref.dtype)


def _norm_proj(x, g, w, *, tm=512):
    t, d = x.shape
    n = w.shape[1]
    assert t % tm == 0
    return pl.pallas_call(
        _norm_proj_kernel,
        out_shape=jax.ShapeDtypeStruct((t, n), w.dtype),
        grid=(t // tm,),
        in_specs=[pl.BlockSpec((tm, d), lambda i: (i, 0)),
                  pl.BlockSpec((1, d), lambda i: (0, 0)),
                  pl.BlockSpec((d, n), lambda i: (0, 0))],
        out_specs=pl.BlockSpec((tm, n), lambda i: (i, 0)),
        compiler_params=_params(("parallel",)),
        name="norm_proj",
    )(x, g.reshape(1, d), w)


def _proj_residual_kernel(a_ref, w_ref, r_ref, o_ref):
    o_ref[...] = r_ref[...] + jnp.dot(a_ref[...], w_ref[...], preferred_element_type=jnp.float32)


def _proj_residual(a, w, res, *, tm=512):
    t, k = a.shape
    n = w.shape[1]
    assert t % tm == 0
    return pl.pallas_call(
        _proj_residual_kernel,
        out_shape=jax.ShapeDtypeStruct((t, n), jnp.float32),
        grid=(t // tm,),
        in_specs=[pl.BlockSpec((tm, k), lambda i: (i, 0)),
                  pl.BlockSpec((k, n), lambda i: (0, 0)),
                  pl.BlockSpec((tm, n), lambda i: (i, 0))],
        out_specs=pl.BlockSpec((tm, n), lambda i: (i, 0)),
        compiler_params=_params(("parallel",)),
        name="proj_residual",
    )(a, w, res)


def _mlp_kernel(x_ref, g_ref, wu_ref, wd_ref, gf_ref, o_ref, xn_ref, acc_ref, *, final_norm):
    f = pl.program_id(1)

    @pl.when(f == 0)
    def _():
        x = x_ref[...]
        xn_ref[...] = _rmsnorm(x, g_ref[...]).astype(xn_ref.dtype)
        acc_ref[...] = x

    u = jnp.dot(xn_ref[...], wu_ref[...], preferred_element_type=jnp.float32)
    a = jnp.square(jnp.maximum(u, 0.0)).astype(wd_ref.dtype)
    acc_ref[...] += jnp.dot(a, wd_ref[...], preferred_element_type=jnp.float32)

    @pl.when(f == pl.num_programs(1) - 1)
    def _():
        y = acc_ref[...]
        if final_norm:
            y = _rmsnorm(y, gf_ref[...])
        o_ref[...] = y


def _mlp(x, g, w_up, w_down, g_final, *, final_norm, tm=1024, tf=1024):
    t, d = x.shape
    ff = w_up.shape[1]
    assert t % tm == 0 and ff % tf == 0
    return pl.pallas_call(
        functools.partial(_mlp_kernel, final_norm=final_norm),
        out_shape=jax.ShapeDtypeStruct((t, d), jnp.float32),
        grid=(t // tm, ff // tf),
        in_specs=[pl.BlockSpec((tm, d), lambda i, f: (i, 0)),
                  pl.BlockSpec((1, d), lambda i, f: (0, 0)),
                  pl.BlockSpec((d, tf), lambda i, f: (0, f)),
                  pl.BlockSpec((tf, d), lambda i, f: (f, 0)),
                  pl.BlockSpec((1, d), lambda i, f: (0, 0))],
        out_specs=pl.BlockSpec((tm, d), lambda i, f: (i, 0)),
        scratch_shapes=[pltpu.VMEM((tm, d), w_up.dtype), pltpu.VMEM((tm, d), jnp.float32)],
        compiler_params=_params(("parallel", "arbitrary")),
        name="mlp",
    )(x, g.reshape(1, d), w_up, w_down, g_final.reshape(1, d))


def _rel_bucket_np(dist):
    n = np.maximum(dist, 0)
    max_exact = REL_BUCKETS // 2
    nf = np.maximum(n, 1).astype(np.float64)
    large = max_exact + (np.log(nf / max_exact) / math.log(REL_MAX_DIST / max_exact)
                         * (REL_BUCKETS - max_exact)).astype(np.int64)
    large = np.minimum(large, REL_BUCKETS - 1)
    return np.where(n < max_exact, n, large).astype(np.int32)


def _bias_table(rel_bias, dist, valid, shift):
    tab = jnp.moveaxis(rel_bias[_rel_bucket_np(dist)], -1, 0)
    tab = tab - shift.reshape((-1,) + (1,) * dist.ndim)
    return jnp.where(jnp.asarray(valid)[None], tab, NEG).astype(jnp.float32)


def _moba_route_mask(gate, blkid, lane_f, i, nblk):
    valid = (blkid >= 0) & (blkid < i)
    g = jnp.where(valid, gate, -jnp.inf)
    sel = blkid == i
    for _ in range(MOBA_TOPK):
        mx = jnp.max(g, axis=1, keepdims=True)
        first = jnp.min(jnp.where(g == mx, lane_f, float(LANES)), axis=1, keepdims=True)
        pick = (lane_f == first) & valid
        sel = sel | pick
        g = jnp.where(pick, -jnp.inf, g)
    inrange = (blkid >= 0) & (blkid < nblk)
    return jnp.where(inrange & jnp.logical_not(sel), NEG, 0.0)


def _softmax_start(s, v):
    m = jnp.max(s, axis=1, keepdims=True)
    p = jnp.exp(s - m)
    l = jnp.sum(p, axis=1, keepdims=True)
    acc = jnp.dot(p.astype(v.dtype), v, preferred_element_type=jnp.float32)
    return m, l, acc


def _softmax_step(s, v, m, l, acc):
    m_new = jnp.maximum(m, jnp.max(s, axis=1, keepdims=True))
    alpha = jnp.exp(m - m_new)
    p = jnp.exp(s - m_new)
    l = alpha * l + jnp.sum(p, axis=1, keepdims=True)
    acc = alpha * acc + jnp.dot(p.astype(v.dtype), v, preferred_element_type=jnp.float32)
    return m_new, l, acc


def _moba_kernel(q_ref, k_ref, v_ref, town_ref, tadj_ref, o_ref, kaug_ref, kmrows_ref, *, blk, nblk):
    i = pl.program_id(2)
    half = LANES // 2
    lane = lax.broadcasted_iota(jnp.int32, (blk, LANES), 1)
    lo = lane < half

    @pl.when(i == 0)
    def _():
        kmrows_ref[...] = jnp.zeros(kmrows_ref.shape, kmrows_ref.dtype)
        lane1 = lax.broadcasted_iota(jnp.int32, (1, LANES), 1)

        def build(n, carry):
            rows = pl.ds(pl.multiple_of(n * blk, blk), blk)
            kn = k_ref[0, rows, :].astype(jnp.float32)
            kaug_ref[0, rows, :] = jnp.where(
                lo, kn, jnp.where(lane == half + n, 1.0, 0.0)).astype(kaug_ref.dtype)
            kaug_ref[1, rows, :] = jnp.where(
                lo, jnp.where(lane == n, 1.0, 0.0), kn).astype(kaug_ref.dtype)
            mean = jnp.sum(kn, axis=0, keepdims=True) * (1.0 / blk)
            kmrows_ref[0, pl.ds(half + n, 1), :] = jnp.where(lane1 < half, mean, 0.0)
            kmrows_ref[1, pl.ds(n, 1), :] = jnp.where(lane1 < half, 0.0, mean)
            return carry

        lax.fori_loop(0, nblk, build, 0)

    q = q_ref[0]
    lane1 = lax.broadcasted_iota(jnp.int32, (1, LANES), 1)
    scale = HEAD_DIM ** -0.5
    keep_lo = jnp.where(lane1 < half, scale, 0.0).astype(q.dtype)
    keep_hi = jnp.where(lane1 < half, 0.0, scale).astype(q.dtype)
    lane_f = lane.astype(jnp.float32)
    j_adj = jnp.maximum(i - 1, 0)
    pen_adj = jnp.where(i >= 1, 0.0, NEG)
    rows_own = pl.ds(pl.multiple_of(i * blk, blk), blk)
    rows_adj = pl.ds(pl.multiple_of(j_adj * blk, blk), blk)

    outs = []
    for hh in range(2):
        blkid = lane - half if hh == 0 else lane
        q_h = q * (keep_lo if hh == 0 else keep_hi)
        gate = lax.dot_general(q_h, kmrows_ref[hh].astype(q.dtype), _NT,
                               preferred_element_type=jnp.float32)
        route = _moba_route_mask(gate, blkid, lane_f, i, nblk)
        q_aug = q_h + route.astype(q.dtype)

        def scores(rows, hh=hh, q_aug=q_aug):
            return lax.dot_general(q_aug, kaug_ref[hh, rows, :], _NT,
                                   preferred_element_type=jnp.float32)

        m, l, acc = _softmax_start(scores(rows_own) + town_ref[hh], v_ref[0, rows_own, :])
        m, l, acc = _softmax_step(scores(rows_adj) + tadj_ref[hh] + pen_adj,
                                  v_ref[0, rows_adj, :], m, l, acc)

        def far(j, carry, scores=scores):
            rows = pl.ds(pl.multiple_of(j * blk, blk), blk)
            return _softmax_step(scores(rows), v_ref[0, rows, :], *carry)

        m, l, acc = lax.fori_loop(0, i - 1, far, (m, l, acc))
        outs.append(acc / l)

    o_ref[0] = jnp.where(lo, outs[0], outs[1]).astype(o_ref.dtype)


def _moba_attention(qkv, rel_bias):
    b, s, _ = qkv.shape
    blk = MOBA_BLOCK
    nblk = s // blk
    hp = N_HEADS // 2
    assert s % blk == 0 and nblk <= LANES // 2 and 2 * HEAD_DIM == LANES
    r = np.arange(blk)[:, None]
    c = np.arange(blk)[None, :]
    shift = rel_bias[REL_BUCKETS - 1]
    t_own = _bias_table(rel_bias, r - c, r >= c, shift)
    t_adj = _bias_table(rel_bias, blk + r - c, np.ones((blk, blk), bool), shift)
    return pl.pallas_call(
        functools.partial(_moba_kernel, blk=blk, nblk=nblk),
        out_shape=jax.ShapeDtypeStruct((b, s, N_HEADS * HEAD_DIM), qkv.dtype),
        grid=(b, hp, nblk),
        in_specs=[pl.BlockSpec((1, blk, LANES), lambda bb, p, i: (bb, i, p)),
                  pl.BlockSpec((1, s, LANES), lambda bb, p, i: (bb, 0, hp + p)),
                  pl.BlockSpec((1, s, LANES), lambda bb, p, i: (bb, 0, 2 * hp + p)),
                  pl.BlockSpec((2, blk, blk), lambda bb, p, i: (p, 0, 0)),
                  pl.BlockSpec((2, blk, blk), lambda bb, p, i: (p, 0, 0))],
        out_specs=pl.BlockSpec((1, blk, LANES), lambda bb, p, i: (bb, i, p)),
        scratch_shapes=[pltpu.VMEM((2, s, LANES), qkv.dtype),
                        pltpu.VMEM((2, LANES, LANES), jnp.float32)],
        compiler_params=_params(("parallel", "parallel", "arbitrary")),
        name="moba_attention",
    )(qkv, qkv, qkv, t_own, t_adj)


def _swa_kernel(q_ref, kp_ref, kc_ref, vp_ref, vc_ref, tab_ref, sink_ref, o_ref, *, blk, group):
    n = pl.program_id(1)
    half = LANES // 2
    lane = lax.broadcasted_iota(jnp.int32, (blk, LANES), 1)
    lo = lane < half
    lane_k = lax.broadcasted_iota(jnp.int32, (2 * blk, LANES), 1)
    lo_k = lane_k < half
    col = lax.broadcasted_iota(jnp.int32, (1, 2 * blk), 1)
    colpen = jnp.where((n == 0) & (col < blk), NEG, 0.0)

    dt = q_ref.dtype
    kband = jnp.concatenate([kp_ref[0], kc_ref[0]], axis=0).astype(jnp.float32)
    vband = jnp.concatenate([vp_ref[0], vc_ref[0]], axis=0).astype(jnp.float32)
    kswap = pltpu.roll(kband, half, 1)
    vswap = pltpu.roll(vband, half, 1)
    lane1 = lax.broadcasted_iota(jnp.int32, (1, LANES), 1)
    scale = HEAD_DIM ** -0.5
    keep = (jnp.where(lane1 < half, scale, 0.0).astype(dt), jnp.where(lane1 < half, 0.0, scale).astype(dt))
    rows_g = group * blk

    for kv in range(SWA_KV_HEADS):
        k2 = (jnp.where(lo_k, kband, kswap) if kv == 0 else jnp.where(lo_k, kswap, kband)).astype(dt)
        v2 = (jnp.where(lo_k, vband, vswap) if kv == 0 else jnp.where(lo_k, vswap, vband)).astype(dt)
        parts = []
        for g in range(group):
            h = kv * group + g
            parts.append(q_ref[0, :, (h // 2) * LANES:(h // 2 + 1) * LANES] * keep[h % 2])
        qs = jnp.concatenate(parts, axis=0)
        s = lax.dot_general(qs, k2, _NT, preferred_element_type=jnp.float32)
        s = s + tab_ref[kv * rows_g:(kv + 1) * rows_g, :] + colpen
        sink = sink_ref[kv * rows_g:(kv + 1) * rows_g, 0:1]
        m = jnp.maximum(jnp.max(s, axis=1, keepdims=True), sink)
        p = jnp.exp(s - m)
        l = jnp.sum(p, axis=1, keepdims=True) + jnp.exp(sink - m)
        o = jnp.dot(p.astype(v2.dtype), v2, preferred_element_type=jnp.float32) / l
        for g in range(0, group, 2):
            h = kv * group + g
            pair = jnp.where(lo, o[g * blk:(g + 1) * blk], o[(g + 1) * blk:(g + 2) * blk])
            o_ref[0, :, (h // 2) * LANES:(h // 2 + 1) * LANES] = pair.astype(o_ref.dtype)


def _swa_attention(proj, sinks, rel_bias):
    b, s, _ = proj.shape
    blk = SWA_BLOCK
    nb = s // blk
    group = N_HEADS // SWA_KV_HEADS
    nq = N_HEADS * HEAD_DIM
    assert s % blk == 0 and SWA_KV_HEADS * HEAD_DIM == LANES and SWA_WINDOW <= blk
    kcol = nq // LANES
    dist = blk + np.arange(blk)[:, None] - np.arange(2 * blk)[None, :]
    tab = _bias_table(rel_bias, dist, (dist >= 0) & (dist < SWA_WINDOW), jnp.zeros_like(sinks))
    tab = tab.reshape(N_HEADS * blk, 2 * blk)
    sink_rows = jnp.broadcast_to(sinks.astype(jnp.float32)[:, None, None],
                                 (N_HEADS, blk, LANES)).reshape(N_HEADS * blk, LANES)
    prev = lambda bb, n: (bb, jnp.maximum(n - 1, 0), kcol)
    cur = lambda bb, n: (bb, n, kcol)
    prev_v = lambda bb, n: (bb, jnp.maximum(n - 1, 0), kcol + 1)
    cur_v = lambda bb, n: (bb, n, kcol + 1)
    return pl.pallas_call(
        functools.partial(_swa_kernel, blk=blk, group=group),
        out_shape=jax.ShapeDtypeStruct((b, s, nq), proj.dtype),
        grid=(b, nb),
        in_specs=[pl.BlockSpec((1, blk, nq), lambda bb, n: (bb, n, 0)),
                  pl.BlockSpec((1, blk, LANES), prev),
                  pl.BlockSpec((1, blk, LANES), cur),
                  pl.BlockSpec((1, blk, LANES), prev_v),
                  pl.BlockSpec((1, blk, LANES), cur_v),
                  pl.BlockSpec((N_HEADS * blk, 2 * blk), lambda bb, n: (0, 0)),
                  pl.BlockSpec((N_HEADS * blk, LANES), lambda bb, n: (0, 0))],
        out_specs=pl.BlockSpec((1, blk, nq), lambda bb, n: (bb, n, 0)),
        compiler_params=_params(("parallel", "arbitrary")),
        name="swa_attention",
    )(proj, proj, proj, proj, proj, tab, sink_rows)


def _dsa_proj_kernel(x_ref, g_ref, w_ref, wwt_ref, kvn_ref, wuk_ref,
                     ql_ref, c_ref, qi_ref, ki_ref, wt_ref, *, idx_scale):
    dt = w_ref.dtype
    nq = N_HEADS * HEAD_DIM
    r = DSA_KV_RANK
    ni = DSA_IDX_HEADS * DSA_IDX_DIM
    xn = _rmsnorm(x_ref[0], g_ref[...]).astype(dt)
    y = jnp.dot(xn, w_ref[...], preferred_element_type=jnp.float32)
    scale = HEAD_DIM ** -0.5
    for p in range(N_HEADS // 2):
        qp = y[:, p * LANES:(p + 1) * LANES].astype(dt)
        ql = jnp.dot(qp, wuk_ref[p], preferred_element_type=jnp.float32) * scale
        ql_ref[0, 2 * p] = ql[:, :r].astype(ql_ref.dtype)
        ql_ref[0, 2 * p + 1] = ql[:, r:].astype(ql_ref.dtype)
    c_ref[0] = _rmsnorm(y[:, nq:nq + r], kvn_ref[...]).astype(c_ref.dtype)
    qi_ref[0] = y[:, nq + r:nq + r + ni].astype(qi_ref.dtype)
    ki_ref[0] = y[:, nq + r + ni:nq + r + ni + LANES].astype(ki_ref.dtype)
    wt = lax.dot_general(wwt_ref[...], xn, _NT, preferred_element_type=jnp.float32)
    wt_ref[0] = wt * idx_scale


def _dsa_proj(x, g, w_in, kv_norm, w_uk, dt, *, tm=512):
    b, s, d = x.shape
    nq = N_HEADS * HEAD_DIM
    r = DSA_KV_RANK
    ni = DSA_IDX_HEADS * DSA_IDX_DIM
    di = DSA_IDX_DIM
    assert 2 * di == LANES and 2 * HEAD_DIM == LANES and s % tm == 0
    k_idx = w_in[:, nq + r + ni:nq + r + ni + di]
    w_main = jnp.concatenate([w_in[:, :nq + r + ni], k_idx, k_idx], axis=1).astype(dt)
    wwt = w_in[:, nq + r + ni + di:].T.astype(dt)
    uk = jnp.transpose(w_uk, (1, 2, 0)).reshape(N_HEADS // 2, 2, HEAD_DIM, r)
    z = jnp.zeros_like(uk[:, 0])
    wuk_bd = jnp.concatenate([jnp.concatenate([uk[:, 0], z], axis=2),
                              jnp.concatenate([z, uk[:, 1]], axis=2)], axis=1).astype(dt)
    nw = w_main.shape[1]
    idx_scale = DSA_IDX_HEADS ** -0.5 * DSA_IDX_DIM ** -0.5
    return pl.pallas_call(
        functools.partial(_dsa_proj_kernel, idx_scale=idx_scale),
        out_shape=(jax.ShapeDtypeStruct((b, N_HEADS, s, r), dt),
                   jax.ShapeDtypeStruct((b, s, r), dt),
                   jax.ShapeDtypeStruct((b, s, ni), dt),
                   jax.ShapeDtypeStruct((b, s, LANES), dt),
                   jax.ShapeDtypeStruct((b, DSA_IDX_HEADS, s), jnp.float32)),
        grid=(b, s // tm),
        in_specs=[pl.BlockSpec((1, tm, d), lambda bb, i: (bb, i, 0)),
                  pl.BlockSpec((1, d), lambda bb, i: (0, 0)),
                  pl.BlockSpec((d, nw), lambda bb, i: (0, 0)),
                  pl.BlockSpec((DSA_IDX_HEADS, d), lambda bb, i: (0, 0)),
                  pl.BlockSpec((1, r), lambda bb, i: (0, 0)),
                  pl.BlockSpec((N_HEADS // 2, LANES, 2 * r), lambda bb, i: (0, 0, 0))],
        out_specs=(pl.BlockSpec((1, N_HEADS, tm, r), lambda bb, i: (bb, 0, i, 0)),
                   pl.BlockSpec((1, tm, r), lambda bb, i: (bb, i, 0)),
                   pl.BlockSpec((1, tm, ni), lambda bb, i: (bb, i, 0)),
                   pl.BlockSpec((1, tm, LANES), lambda bb, i: (bb, i, 0)),
                   pl.BlockSpec((1, DSA_IDX_HEADS, tm), lambda bb, i: (bb, 0, i))),
        compiler_params=_params(("parallel", "parallel")),
        name="dsa_proj",
    )(x, g.reshape(1, d), w_main, wwt, kv_norm.reshape(1, r), wuk_bd)


def _dsa_kernel(qi_ref, wt_ref, ki_ref, c_ref, ct_ref, ql_ref, tdiag_ref, tadj_ref, o_ref,
                u_ref, acc_ref, m_ref, l_ref, *, tile, topk, s_len):
    i = pl.program_id(1)
    half = LANES // 2
    nih = DSA_IDX_HEADS

    def tile_rows(j):
        return pl.ds(pl.multiple_of(j * tile, tile), tile)

    lane1 = lax.broadcasted_iota(jnp.int32, (1, LANES), 1)
    keep = (jnp.where(lane1 < half, 1.0, 0.0).astype(qi_ref.dtype),
            jnp.where(lane1 < half, 0.0, 1.0).astype(qi_ref.dtype))
    parts = []
    for h in range(nih):
        parts.append(qi_ref[0, :, (h // 2) * LANES:(h // 2 + 1) * LANES] * keep[h % 2])
    qs = jnp.concatenate(parts, axis=0)
    w_t = wt_ref[0]
    krow = lax.broadcasted_iota(jnp.int32, (tile, tile), 0)
    qcol = lax.broadcasted_iota(jnp.int32, (tile, tile), 1)

    def index_tile(j, carry):
        rows = tile_rows(j)
        rel = lax.dot_general(ki_ref[0, rows, :], qs, _NT, preferred_element_type=jnp.float32)
        sc = jnp.zeros((tile, tile), jnp.float32)
        for h in range(nih):
            sc = sc + jnp.maximum(rel[:, h * tile:(h + 1) * tile], 0.0) * w_t[h:h + 1, :]
        sc = jnp.where(sc == 0.0, 0.0, sc)
        bits = lax.bitcast_convert_type(sc, jnp.int32)
        u = jnp.where(bits < 0, bits ^ jnp.int32(0x7FFFFFFF), bits)
        causal = (j * tile + krow) <= (i * tile + qcol)
        u_ref[rows, :] = jnp.where(causal, u, jnp.int32(INT_MIN))
        return carry

    lax.fori_loop(0, i + 1, index_tile, 0)

    def count(pred):
        def body(j, acc8):
            ones = jnp.where(pred(j, u_ref[tile_rows(j), :]), 1.0, 0.0)
            return acc8 + jnp.sum(ones.reshape(tile // 8, 8, tile), axis=0)
        acc8 = lax.fori_loop(0, i + 1, body, jnp.zeros((8, tile), jnp.float32))
        return jnp.sum(acc8, axis=0, keepdims=True)

    def bit_step(bi, carry):
        ans, cnt_ans = carry
        cand = ans | lax.shift_left(jnp.int32(1), 31 - bi)
        thr_c = cand ^ jnp.int32(INT_MIN)
        cnt = count(lambda j, u: u >= thr_c)
        ok = cnt >= float(topk)
        return jnp.where(ok, cand, ans), jnp.where(ok, cnt, cnt_ans)

    ans, cnt_thr = lax.fori_loop(0, 32, bit_step, (jnp.zeros((1, tile), jnp.int32),
                                                   jnp.zeros((1, tile), jnp.float32)))
    thr = ans ^ jnp.int32(INT_MIN)

    nbits = int(s_len).bit_length()
    cut_all = jnp.full((1, tile), 2 ** nbits - 1, jnp.int32)

    def tie_search():
        need = float(topk) - count(lambda j, u: u > thr)

        def cut_step(bi, cut):
            cand = cut | lax.shift_left(jnp.int32(1), nbits - 1 - bi)
            cnt = count(lambda j, u: (u == thr) & ((j * tile + krow) < cand))
            return jnp.where(cnt <= need, cand, cut)

        return lax.fori_loop(0, nbits, cut_step, jnp.zeros((1, tile), jnp.int32))

    cut = lax.cond(jnp.max(cnt_thr) > float(topk), tie_search, lambda: cut_all)

    m_ref[...] = jnp.full(m_ref.shape, NEG, jnp.float32)
    l_ref[...] = jnp.zeros(l_ref.shape, jnp.float32)
    acc_ref[...] = jnp.zeros(acc_ref.shape, jnp.float32)

    def attend_tile(j, table_ref, pen):
        rows = tile_rows(j)
        u = u_ref[rows, :]
        tied = (u == thr) & ((j * tile + krow) < cut) & (u != jnp.int32(INT_MIN))
        maskadd = jnp.where((u > thr) | tied, 0.0, NEG) + pen
        c_t = c_ref[0, rows, :]
        ct_t = ct_ref[0, j]

        def head(h, carry):
            st = lax.dot_general(c_t, ql_ref[0, h], _NT, preferred_element_type=jnp.float32)
            logit = st + maskadd
            if table_ref is not None:
                logit = logit + table_ref[h]
            hrow = pl.ds(h, 1)
            m_old = m_ref[hrow, :]
            m_new = jnp.maximum(m_old, jnp.max(logit, axis=0, keepdims=True))
            alpha = jnp.exp(m_old - m_new)
            p = jnp.exp(logit - m_new)
            l_ref[hrow, :] = alpha * l_ref[hrow, :] + jnp.sum(p, axis=0, keepdims=True)
            m_ref[hrow, :] = m_new
            pv = jnp.dot(ct_t, p.astype(ct_t.dtype), preferred_element_type=jnp.float32)
            acc_ref[h] = acc_ref[h] * alpha + pv
            return carry

        lax.fori_loop(0, N_HEADS, head, 0)

    def far(j, carry):
        attend_tile(j, None, 0.0)
        return carry

    lax.fori_loop(0, i - 1, far, 0)
    attend_tile(jnp.maximum(i - 1, 0), tadj_ref, jnp.where(i >= 1, 0.0, NEG))
    attend_tile(i, tdiag_ref, 0.0)

    def finish(h, carry):
        o_ref[0, h] = (acc_ref[h] / l_ref[pl.ds(h, 1), :]).astype(o_ref.dtype)
        return carry

    lax.fori_loop(0, N_HEADS, finish, 0)


def _dsa_attention(ql, c, qi, ki2, wt, rel_bias):
    b, s, r = c.shape
    tile = DSA_TILE
    nt = s // tile
    topk = min(DSA_TOPK_MAX, s // 4)
    assert s % tile == 0
    ct = jnp.transpose(c.reshape(b, nt, tile, r), (0, 1, 3, 2))
    kk = np.arange(tile)[:, None]
    qq = np.arange(tile)[None, :]
    shift = rel_bias[REL_BUCKETS - 1]
    t_diag = _bias_table(rel_bias, qq - kk, qq >= kk, shift)
    t_adj = _bias_table(rel_bias, tile + qq - kk, np.ones((tile, tile), bool), shift)
    ni = qi.shape[2]
    return pl.pallas_call(
        functools.partial(_dsa_kernel, tile=tile, topk=topk, s_len=s),
        out_shape=jax.ShapeDtypeStruct((b, N_HEADS, r, s), c.dtype),
        grid=(b, nt),
        in_specs=[pl.BlockSpec((1, tile, ni), lambda bb, i: (bb, i, 0)),
                  pl.BlockSpec((1, DSA_IDX_HEADS, tile), lambda bb, i: (bb, 0, i)),
                  pl.BlockSpec((1, s, LANES), lambda bb, i: (bb, 0, 0)),
                  pl.BlockSpec((1, s, r), lambda bb, i: (bb, 0, 0)),
                  pl.BlockSpec((1, nt, r, tile), lambda bb, i: (bb, 0, 0, 0)),
                  pl.BlockSpec((1, N_HEADS, tile, r), lambda bb, i: (bb, 0, i, 0)),
                  pl.BlockSpec((N_HEADS, tile, tile), lambda bb, i: (0, 0, 0)),
                  pl.BlockSpec((N_HEADS, tile, tile), lambda bb, i: (0, 0, 0))],
        out_specs=pl.BlockSpec((1, N_HEADS, r, tile), lambda bb, i: (bb, 0, 0, i)),
        scratch_shapes=[pltpu.VMEM((s, tile), jnp.int32),
                        pltpu.VMEM((N_HEADS, r, tile), jnp.float32),
                        pltpu.VMEM((N_HEADS, tile), jnp.float32),
                        pltpu.VMEM((N_HEADS, tile), jnp.float32)],
        compiler_params=_params(("parallel", "arbitrary")),
        name="dsa_attention",
    )(qi, wt, ki2, c, ct, ql, t_diag, t_adj)


def _dsa_out_kernel(ol_ref, wuv_ref, wo_ref, r_ref, o_ref):
    parts = []
    for p in range(N_HEADS // 2):
        olp = ol_ref[:, p * 2 * DSA_KV_RANK:(p + 1) * 2 * DSA_KV_RANK]
        parts.append(jnp.dot(olp, wuv_ref[p], preferred_element_type=jnp.float32))
    o = jnp.concatenate(parts, axis=1).astype(wo_ref.dtype)
    o_ref[...] = r_ref[...] + jnp.dot(o, wo_ref[...], preferred_element_type=jnp.float32)


def _dsa_out(ol, w_uv, w_out, res, dt, *, tm=512):
    t, n = ol.shape
    d = w_out.shape[1]
    r = DSA_KV_RANK
    uv = jnp.transpose(w_uv, (1, 0, 2)).reshape(N_HEADS // 2, 2, r, HEAD_DIM)
    z = jnp.zeros_like(uv[:, 0])
    wuv_bd = jnp.concatenate([jnp.concatenate([uv[:, 0], z], axis=2),
                              jnp.concatenate([z, uv[:, 1]], axis=2)], axis=1).astype(dt)
    return pl.pallas_call(
        _dsa_out_kernel,
        out_shape=jax.ShapeDtypeStruct((t, d), jnp.float32),
        grid=(t // tm,),
        in_specs=[pl.BlockSpec((tm, n), lambda i: (i, 0)),
                  pl.BlockSpec((N_HEADS // 2, 2 * r, LANES), lambda i: (0, 0, 0)),
                  pl.BlockSpec((N_HEADS * HEAD_DIM, d), lambda i: (0, 0)),
                  pl.BlockSpec((tm, d), lambda i: (i, 0))],
        out_specs=pl.BlockSpec((tm, d), lambda i: (i, 0)),
        compiler_params=_params(("parallel",)),
        name="dsa_out",
    )(ol, wuv_bd, w_out.astype(dt), res)


def _dsa_layer(h, g, w_in, kv_norm, w_uk, w_uv, w_out, rel_bias, mxu_dtype):
    b, s, d = h.shape
    ql, c, qi, ki2, wt = _dsa_proj(h, g, w_in, kv_norm, w_uk, mxu_dtype)
    olt = _dsa_attention(ql, c, qi, ki2, wt, rel_bias)
    ol = jnp.transpose(olt, (0, 3, 1, 2)).reshape(b * s, N_HEADS * DSA_KV_RANK)
    return _dsa_out(ol, w_uv, w_out, h.reshape(b * s, d), mxu_dtype).reshape(b, s, d)


def kernel(x, rel_bias, norm_mix, norm_mlp, mlp_up, mlp_down, a_w_in, a_w_out, b_w_in, b_sinks, b_w_out,
           c_w_in, c_kv_norm, c_w_uk, c_w_uv, c_w_out, final_norm):
    b, s, d = x.shape
    t = b * s
    dt = MXU_DTYPE
    depth = norm_mix.shape[0]
    h = x.reshape(t, d)
    for i in range(depth):
        kind, j = i % 3, i // 3
        if kind == 0:
            qkv = _norm_proj(h, norm_mix[i], a_w_in[j].astype(dt)).reshape(b, s, -1)
            o = _moba_attention(qkv, rel_bias)
            h = _proj_residual(o.reshape(t, -1), a_w_out[j].astype(dt), h)
        elif kind == 1:
            proj = _norm_proj(h, norm_mix[i], b_w_in[j].astype(dt)).reshape(b, s, -1)
            o = _swa_attention(proj, b_sinks[j], rel_bias)
            h = _proj_residual(o.reshape(t, -1), b_w_out[j].astype(dt), h)
        else:
            h = _dsa_layer(h.reshape(b, s, d), norm_mix[i], c_w_in[j], c_kv_norm[j], c_w_uk[j],
                           c_w_uv[j], c_w_out[j], rel_bias, dt).reshape(t, d)
        h = _mlp(h, norm_mlp[i], mlp_up[i].astype(dt), mlp_down[i].astype(dt), final_norm,
                 final_norm=(i == depth - 1))
    return h.reshape(b, s, d)
```

```python
import functools
import math

import numpy as np
import jax
import jax.numpy as jnp
from jax import lax
from jax.experimental import pallas as pl
from jax.experimental.pallas import tpu as pltpu

N_HEADS = 16
HEAD_DIM = 64
NORM_EPS = 1e-6
REL_BUCKETS = 32
REL_MAX_DIST = 128
MOBA_BLOCK = 256
MOBA_TOPK = 3
MOBA_GROUP = 4
SWA_WINDOW = 128
SWA_BLOCK = 128
SWA_KV_HEADS = 2
DSA_KV_RANK = 256
DSA_IDX_HEADS = 8
DSA_IDX_DIM = 64
DSA_TOPK_MAX = 256
DSA_TILE = 256
DSA_HEAD_GROUP = 8

LANES = 128
MXU_DTYPE = jnp.bfloat16
NEG = -1e30
INT_MIN = -2 ** 31
VMEM_LIMIT_BYTES = 56 * 1024 * 1024

_NT = (((1,), (1,)), ((), ()))


def _params(semantics):
    return pltpu.CompilerParams(dimension_semantics=semantics, vmem_limit_bytes=VMEM_LIMIT_BYTES)


def _rmsnorm(x, g):
    var = jnp.mean(x * x, axis=-1, keepdims=True)
    return x * lax.rsqrt(var + NORM_EPS) * g


def _norm_proj_kernel(x_ref, g_ref, w_ref, o_ref):
    xn = _rmsnorm(x_ref[...], g_ref[...]).astype(w_ref.dtype)
    o_ref[...] = jnp.dot(xn, w_ref[...], preferred_element_type=jnp.float32).astype(o_ref.dtype)


def _norm_proj(x, g, w, *, tm=512):
    t, d = x.shape
    n = w.shape[1]
    assert t % tm == 0
    return pl.pallas_call(
        _norm_proj_kernel,
        out_shape=jax.ShapeDtypeStruct((t, n), w.dtype),
        grid=(t // tm,),
        in_specs=[pl.BlockSpec((tm, d), lambda i: (i, 0)),
                  pl.BlockSpec((1, d), lambda i: (0, 0)),
                  pl.BlockSpec((d, n), lambda i: (0, 0))],
        out_specs=pl.BlockSpec((tm, n), lambda i: (i, 0)),
        compiler_params=_params(("parallel",)),
        name="norm_proj",
    )(x, g.reshape(1, d), w)


def _proj_residual_kernel(a_ref, w_ref, r_ref, o_ref):
    o_ref[...] = r_ref[...] + jnp.dot(a_ref[...], w_ref[...], preferred_element_type=jnp.float32)


def _proj_residual(a, w, res, *, tm=512):
    t, k = a.shape
    n = w.shape[1]
    assert t % tm == 0
    return pl.pallas_call(
        _proj_residual_kernel,
        out_shape=jax.ShapeDtypeStruct((t, n), jnp.float32),
        grid=(t // tm,),
        in_specs=[pl.BlockSpec((tm, k), lambda i: (i, 0)),
                  pl.BlockSpec((k, n), lambda i: (0, 0)),
                  pl.BlockSpec((tm, n), lambda i: (i, 0))],
        out_specs=pl.BlockSpec((tm, n), lambda i: (i, 0)),
        compiler_params=_params(("parallel",)),
        name="proj_residual",
    )(a, w, res)


def _mlp_kernel(x_ref, g_ref, wu_ref, wd_ref, gf_ref, o_ref, xn_ref, acc_ref, *, final_norm):
    f = pl.program_id(1)

    @pl.when(f == 0)
    def _():
        x = x_ref[...]
        xn_ref[...] = _rmsnorm(x, g_ref[...]).astype(xn_ref.dtype)
        acc_ref[...] = x

    u = jnp.dot(xn_ref[...], wu_ref[...], preferred_element_type=jnp.float32)
    a = jnp.square(jnp.maximum(u, 0.0)).astype(wd_ref.dtype)
    acc_ref[...] += jnp.dot(a, wd_ref[...], preferred_element_type=jnp.float32)

    @pl.when(f == pl.num_programs(1) - 1)
    def _():
        y = acc_ref[...]
        if final_norm:
            y = _rmsnorm(y, gf_ref[...])
        o_ref[...] = y


def _mlp(x, g, w_up, w_down, g_final, *, final_norm, tm=1024, tf=1024):
    t, d = x.shape
    ff = w_up.shape[1]
    assert t % tm == 0 and ff % tf == 0
    return pl.pallas_call(
        functools.partial(_mlp_kernel, final_norm=final_norm),
        out_shape=jax.ShapeDtypeStruct((t, d), jnp.float32),
        grid=(t // tm, ff // tf),
        in_specs=[pl.BlockSpec((tm, d), lambda i, f: (i, 0)),
                  pl.BlockSpec((1, d), lambda i, f: (0, 0)),
                  pl.BlockSpec((d, tf), lambda i, f: (0, f)),
                  pl.BlockSpec((tf, d), lambda i, f: (f, 0)),
                  pl.BlockSpec((1, d), lambda i, f: (0, 0))],
        out_specs=pl.BlockSpec((tm, d), lambda i, f: (i, 0)),
        scratch_shapes=[pltpu.VMEM((tm, d), w_up.dtype), pltpu.VMEM((tm, d), jnp.float32)],
        compiler_params=_params(("parallel", "arbitrary")),
        name="mlp",
    )(x, g.reshape(1, d), w_up, w_down, g_final.reshape(1, d))


def _rel_bucket_np(dist):
    n = np.maximum(dist, 0)
    max_exact = REL_BUCKETS // 2
    nf = np.maximum(n, 1).astype(np.float64)
    large = max_exact + (np.log(nf / max_exact) / math.log(REL_MAX_DIST / max_exact)
                         * (REL_BUCKETS - max_exact)).astype(np.int64)
    large = np.minimum(large, REL_BUCKETS - 1)
    return np.where(n < max_exact, n, large).astype(np.int32)


def _bias_table(rel_bias, dist, valid, shift):
    tab = jnp.moveaxis(rel_bias[_rel_bucket_np(dist)], -1, 0)
    tab = tab - shift.reshape((-1,) + (1,) * dist.ndim)
    return jnp.where(jnp.asarray(valid)[None], tab, NEG).astype(jnp.float32)


def _moba_route_mask(gate, blkid, lane_f, i):
    valid = (blkid >= 0) & (blkid < i)
    g = jnp.where(valid, gate, -jnp.inf)
    sel = blkid == i
    for _ in range(MOBA_TOPK):
        mx = jnp.max(g, axis=1, keepdims=True)
        first = jnp.min(jnp.where(g == mx, lane_f, float(LANES)), axis=1, keepdims=True)
        pick = (lane_f == first) & valid
        sel = sel | pick
        g = jnp.where(pick, -jnp.inf, g)
    inrange = (blkid >= 0) & (blkid < LANES // 2)
    return jnp.where(inrange & jnp.logical_not(sel), NEG, 0.0)


def _softmax_start(s, v):
    m = jnp.max(s, axis=1, keepdims=True)
    p = jnp.exp(s - m)
    l = jnp.sum(p, axis=1, keepdims=True)
    acc = jnp.dot(p.astype(v.dtype), v, preferred_element_type=jnp.float32)
    return m, l, acc


def _softmax_step(s, v, m, l, acc):
    m_new = jnp.maximum(m, jnp.max(s, axis=1, keepdims=True))
    alpha = jnp.exp(m - m_new)
    p = jnp.exp(s - m_new)
    l = alpha * l + jnp.sum(p, axis=1, keepdims=True)
    acc = alpha * acc + jnp.dot(p.astype(v.dtype), v, preferred_element_type=jnp.float32)
    return m_new, l, acc


def _moba_kernel(q_ref, k_ref, v_ref, tnear_ref, o_ref, kaug_ref, vpad_ref, kmrows_ref,
                 *, blk, nblk, group):
    i = pl.program_id(2)
    half = LANES // 2
    pad = group * blk
    lane = lax.broadcasted_iota(jnp.int32, (blk, LANES), 1)
    lo = lane < half

    @pl.when(i == 0)
    def _():
        kmrows_ref[...] = jnp.zeros(kmrows_ref.shape, kmrows_ref.dtype)
        lane1 = lax.broadcasted_iota(jnp.int32, (1, LANES), 1)
        lane_p = lax.broadcasted_iota(jnp.int32, (pad, LANES), 1)
        kaug_ref[0, 0:pad, :] = jnp.where(lane_p == LANES - 1, 1.0, 0.0).astype(kaug_ref.dtype)
        kaug_ref[1, 0:pad, :] = jnp.where(lane_p == half - 1, 1.0, 0.0).astype(kaug_ref.dtype)
        vpad_ref[0:pad, :] = jnp.zeros((pad, LANES), vpad_ref.dtype)

        def build(n, carry):
            rows = pl.ds(pl.multiple_of(n * blk, blk), blk)
            prows = pl.ds(pl.multiple_of(n * blk + pad, blk), blk)
            kn = k_ref[0, rows, :].astype(jnp.float32)
            kaug_ref[0, prows, :] = jnp.where(
                lo, kn, jnp.where(lane == half + n, 1.0, 0.0)).astype(kaug_ref.dtype)
            kaug_ref[1, prows, :] = jnp.where(
                lo, jnp.where(lane == n, 1.0, 0.0), kn).astype(kaug_ref.dtype)
            vpad_ref[prows, :] = v_ref[0, rows, :]
            mean = jnp.sum(kn, axis=0, keepdims=True) * (1.0 / blk)
            kmrows_ref[0, pl.ds(half + n, 1), :] = jnp.where(lane1 < half, mean, 0.0)
            kmrows_ref[1, pl.ds(n, 1), :] = jnp.where(lane1 < half, 0.0, mean)
            return carry

        lax.fori_loop(0, nblk, build, 0)

    q = q_ref[0]
    lane1 = lax.broadcasted_iota(jnp.int32, (1, LANES), 1)
    scale = HEAD_DIM ** -0.5
    keep = (jnp.where(lane1 < half, scale, 0.0).astype(q.dtype),
            jnp.where(lane1 < half, 0.0, scale).astype(q.dtype))
    lane_f = lane.astype(jnp.float32)

    q_aug = []
    for hh in range(2):
        blkid = lane - half if hh == 0 else lane
        q_h = q * keep[hh]
        gate = lax.dot_general(q_h, kmrows_ref[hh].astype(q.dtype), _NT,
                               preferred_element_type=jnp.float32)
        route = _moba_route_mask(gate, blkid, lane_f, i)
        q_aug.append(q_h + route.astype(q.dtype))

    def scores(hh, rows):
        return lax.dot_general(q_aug[hh], kaug_ref[hh, rows, :], _NT,
                               preferred_element_type=jnp.float32)

    near = pl.ds(pl.multiple_of((i + group - 1) * blk, blk), 2 * blk)
    v_near = vpad_ref[near, :]
    state = []
    for hh in range(2):
        state.extend(_softmax_start(scores(hh, near) + tnear_ref[hh], v_near))

    def far(g, carry):
        rows = pl.ds(pl.multiple_of((i - 1 - group * g) * blk, blk), group * blk)
        v_g = vpad_ref[rows, :]
        out = []
        for hh in range(2):
            out.extend(_softmax_step(scores(hh, rows), v_g, *carry[3 * hh:3 * hh + 3]))
        return tuple(out)

    state = lax.fori_loop(0, (i + group - 2) // group, far, tuple(state))
    o_ref[0] = jnp.where(lo, state[2] / state[1], state[5] / state[4]).astype(o_ref.dtype)


def _moba_attention(qkv, rel_bias):
    b, s, _ = qkv.shape
    blk = MOBA_BLOCK
    group = MOBA_GROUP
    nblk = s // blk
    hp = N_HEADS // 2
    assert s % blk == 0 and nblk < LANES // 2 and 2 * HEAD_DIM == LANES and group >= 2
    r = np.arange(blk)[:, None]
    c = np.arange(blk)[None, :]
    shift = rel_bias[REL_BUCKETS - 1]
    t_own = _bias_table(rel_bias, r - c, r >= c, shift)
    t_adj = _bias_table(rel_bias, blk + r - c, np.ones((blk, blk), bool), shift)
    t_near = jnp.concatenate([t_adj, t_own], axis=2)
    return pl.pallas_call(
        functools.partial(_moba_kernel, blk=blk, nblk=nblk, group=group),
        out_shape=jax.ShapeDtypeStruct((b, s, N_HEADS * HEAD_DIM), qkv.dtype),
        grid=(b, hp, nblk),
        in_specs=[pl.BlockSpec((1, blk, LANES), lambda bb, p, i: (bb, i, p)),
                  pl.BlockSpec((1, s, LANES), lambda bb, p, i: (bb, 0, hp + p)),
                  pl.BlockSpec((1, s, LANES), lambda bb, p, i: (bb, 0, 2 * hp + p)),
                  pl.BlockSpec((2, blk, 2 * blk), lambda bb, p, i: (p, 0, 0))],
        out_specs=pl.BlockSpec((1, blk, LANES), lambda bb, p, i: (bb, i, p)),
        scratch_shapes=[pltpu.VMEM((2, s + group * blk, LANES), qkv.dtype),
                        pltpu.VMEM((s + group * blk, LANES), qkv.dtype),
                        pltpu.VMEM((2, LANES, LANES), jnp.float32)],
        compiler_params=_params(("parallel", "parallel", "arbitrary")),
        name="moba_attention",
    )(qkv, qkv, qkv, t_near)


def _swa_kernel(q_ref, kp_ref, kc_ref, vp_ref, vc_ref, tab_ref, sink_ref, o_ref, *, blk, group):
    n = pl.program_id(1)
    half = LANES // 2
    lane = lax.broadcasted_iota(jnp.int32, (blk, LANES), 1)
    lo = lane < half
    lane_k = lax.broadcasted_iota(jnp.int32, (2 * blk, LANES), 1)
    lo_k = lane_k < half
    col = lax.broadcasted_iota(jnp.int32, (1, 2 * blk), 1)
    colpen = jnp.where((n == 0) & (col < blk), NEG, 0.0)

    dt = q_ref.dtype
    kband = jnp.concatenate([kp_ref[0], kc_ref[0]], axis=0).astype(jnp.float32)
    vband = jnp.concatenate([vp_ref[0], vc_ref[0]], axis=0).astype(jnp.float32)
    kswap = pltpu.roll(kband, half, 1)
    vswap = pltpu.roll(vband, half, 1)
    lane1 = lax.broadcasted_iota(jnp.int32, (1, LANES), 1)
    scale = HEAD_DIM ** -0.5
    keep = (jnp.where(lane1 < half, scale, 0.0).astype(dt), jnp.where(lane1 < half, 0.0, scale).astype(dt))
    rows_g = group * blk

    for kv in range(SWA_KV_HEADS):
        k2 = (jnp.where(lo_k, kband, kswap) if kv == 0 else jnp.where(lo_k, kswap, kband)).astype(dt)
        v2 = (jnp.where(lo_k, vband, vswap) if kv == 0 else jnp.where(lo_k, vswap, vband)).astype(dt)
        parts = []
        for g in range(group):
            h = kv * group + g
            parts.append(q_ref[0, :, (h // 2) * LANES:(h // 2 + 1) * LANES] * keep[h % 2])
        qs = jnp.concatenate(parts, axis=0)
        s = lax.dot_general(qs, k2, _NT, preferred_element_type=jnp.float32)
        s = s + tab_ref[kv * rows_g:(kv + 1) * rows_g, :] + colpen
        sink = sink_ref[kv * rows_g:(kv + 1) * rows_g, 0:1]
        m = jnp.maximum(jnp.max(s, axis=1, keepdims=True), sink)
        p = jnp.exp(s - m)
        l = jnp.sum(p, axis=1, keepdims=True) + jnp.exp(sink - m)
        o = jnp.dot(p.astype(v2.dtype), v2, preferred_element_type=jnp.float32) / l
        for g in range(0, group, 2):
            h = kv * group + g
            pair = jnp.where(lo, o[g * blk:(g + 1) * blk], o[(g + 1) * blk:(g + 2) * blk])
            o_ref[0, :, (h // 2) * LANES:(h // 2 + 1) * LANES] = pair.astype(o_ref.dtype)


def _swa_attention(proj, sinks, rel_bias):
    b, s, _ = proj.shape
    blk = SWA_BLOCK
    nb = s // blk
    group = N_HEADS // SWA_KV_HEADS
    nq = N_HEADS * HEAD_DIM
    assert s % blk == 0 and SWA_KV_HEADS * HEAD_DIM == LANES and SWA_WINDOW <= blk
    kcol = nq // LANES
    dist = blk + np.arange(blk)[:, None] - np.arange(2 * blk)[None, :]
    tab = _bias_table(rel_bias, dist, (dist >= 0) & (dist < SWA_WINDOW), jnp.zeros_like(sinks))
    tab = tab.reshape(N_HEADS * blk, 2 * blk)
    sink_rows = jnp.broadcast_to(sinks.astype(jnp.float32)[:, None, None],
                                 (N_HEADS, blk, LANES)).reshape(N_HEADS * blk, LANES)
    prev = lambda bb, n: (bb, jnp.maximum(n - 1, 0), kcol)
    cur = lambda bb, n: (bb, n, kcol)
    prev_v = lambda bb, n: (bb, jnp.maximum(n - 1, 0), kcol + 1)
    cur_v = lambda bb, n: (bb, n, kcol + 1)
    return pl.pallas_call(
        functools.partial(_swa_kernel, blk=blk, group=group),
        out_shape=jax.ShapeDtypeStruct((b, s, nq), proj.dtype),
        grid=(b, nb),
        in_specs=[pl.BlockSpec((1, blk, nq), lambda bb, n: (bb, n, 0)),
                  pl.BlockSpec((1, blk, LANES), prev),
                  pl.BlockSpec((1, blk, LANES), cur),
                  pl.BlockSpec((1, blk, LANES), prev_v),
                  pl.BlockSpec((1, blk, LANES), cur_v),
                  pl.BlockSpec((N_HEADS * blk, 2 * blk), lambda bb, n: (0, 0)),
                  pl.BlockSpec((N_HEADS * blk, LANES), lambda bb, n: (0, 0))],
        out_specs=pl.BlockSpec((1, blk, nq), lambda bb, n: (bb, n, 0)),
        compiler_params=_params(("parallel", "arbitrary")),
        name="swa_attention",
    )(proj, proj, proj, proj, proj, tab, sink_rows)


def _dsa_proj_kernel(x_ref, g_ref, w_ref, wwt_ref, kvn_ref, wuk_ref,
                     ql_ref, c_ref, qi_ref, ki_ref, wt_ref, *, idx_scale):
    dt = w_ref.dtype
    nq = N_HEADS * HEAD_DIM
    r = DSA_KV_RANK
    ni = DSA_IDX_HEADS * DSA_IDX_DIM
    xn = _rmsnorm(x_ref[0], g_ref[...]).astype(dt)
    y = jnp.dot(xn, w_ref[...], preferred_element_type=jnp.float32)
    scale = HEAD_DIM ** -0.5
    for p in range(N_HEADS // 2):
        qp = y[:, p * LANES:(p + 1) * LANES].astype(dt)
        ql = jnp.dot(qp, wuk_ref[p], preferred_element_type=jnp.float32) * scale
        ql_ref[0, 2 * p] = ql[:, :r].astype(ql_ref.dtype)
        ql_ref[0, 2 * p + 1] = ql[:, r:].astype(ql_ref.dtype)
    c_ref[0] = _rmsnorm(y[:, nq:nq + r], kvn_ref[...]).astype(c_ref.dtype)
    qi_ref[0] = y[:, nq + r:nq + r + ni].astype(qi_ref.dtype)
    ki_ref[0] = y[:, nq + r + ni:nq + r + ni + LANES].astype(ki_ref.dtype)
    wt = lax.dot_general(wwt_ref[...], xn, _NT, preferred_element_type=jnp.float32)
    wt_ref[0] = wt * idx_scale


def _dsa_proj(x, g, w_in, kv_norm, w_uk, dt, *, tm=512):
    b, s, d = x.shape
    nq = N_HEADS * HEAD_DIM
    r = DSA_KV_RANK
    ni = DSA_IDX_HEADS * DSA_IDX_DIM
    di = DSA_IDX_DIM
    assert 2 * di == LANES and 2 * HEAD_DIM == LANES and s % tm == 0
    k_idx = w_in[:, nq + r + ni:nq + r + ni + di]
    w_main = jnp.concatenate([w_in[:, :nq + r + ni], k_idx, k_idx], axis=1).astype(dt)
    wwt = w_in[:, nq + r + ni + di:].T.astype(dt)
    uk = jnp.transpose(w_uk, (1, 2, 0)).reshape(N_HEADS // 2, 2, HEAD_DIM, r)
    z = jnp.zeros_like(uk[:, 0])
    wuk_bd = jnp.concatenate([jnp.concatenate([uk[:, 0], z], axis=2),
                              jnp.concatenate([z, uk[:, 1]], axis=2)], axis=1).astype(dt)
    nw = w_main.shape[1]
    idx_scale = DSA_IDX_HEADS ** -0.5 * DSA_IDX_DIM ** -0.5
    return pl.pallas_call(
        functools.partial(_dsa_proj_kernel, idx_scale=idx_scale),
        out_shape=(jax.ShapeDtypeStruct((b, N_HEADS, s, r), dt),
                   jax.ShapeDtypeStruct((b, s, r), dt),
                   jax.ShapeDtypeStruct((b, s, ni), dt),
                   jax.ShapeDtypeStruct((b, s, LANES), dt),
                   jax.ShapeDtypeStruct((b, DSA_IDX_HEADS, s), jnp.float32)),
        grid=(b, s // tm),
        in_specs=[pl.BlockSpec((1, tm, d), lambda bb, i: (bb, i, 0)),
                  pl.BlockSpec((1, d), lambda bb, i: (0, 0)),
                  pl.BlockSpec((d, nw), lambda bb, i: (0, 0)),
                  pl.BlockSpec((DSA_IDX_HEADS, d), lambda bb, i: (0, 0)),
                  pl.BlockSpec((1, r), lambda bb, i: (0, 0)),
                  pl.BlockSpec((N_HEADS // 2, LANES, 2 * r), lambda bb, i: (0, 0, 0))],
        out_specs=(pl.BlockSpec((1, N_HEADS, tm, r), lambda bb, i: (bb, 0, i, 0)),
                   pl.BlockSpec((1, tm, r), lambda bb, i: (bb, i, 0)),
                   pl.BlockSpec((1, tm, ni), lambda bb, i: (bb, i, 0)),
                   pl.BlockSpec((1, tm, LANES), lambda bb, i: (bb, i, 0)),
                   pl.BlockSpec((1, DSA_IDX_HEADS, tm), lambda bb, i: (bb, 0, i))),
        compiler_params=_params(("parallel", "parallel")),
        name="dsa_proj",
    )(x, g.reshape(1, d), w_main, wwt, kv_norm.reshape(1, r), wuk_bd)


def _dsa_kernel(qi_ref, wt_ref, ki_ref, c_ref, ct_ref, ql_ref, tdiag_ref, tadj_ref, o_ref,
                u_ref, acc_ref, m_ref, l_ref, *, tile, topk, s_len, hg):
    i = pl.program_id(1)
    half = LANES // 2
    nih = DSA_IDX_HEADS

    def tile_rows(j):
        return pl.ds(pl.multiple_of(j * tile, tile), tile)

    lane1 = lax.broadcasted_iota(jnp.int32, (1, LANES), 1)
    keep = (jnp.where(lane1 < half, 1.0, 0.0).astype(qi_ref.dtype),
            jnp.where(lane1 < half, 0.0, 1.0).astype(qi_ref.dtype))
    parts = []
    for h in range(nih):
        parts.append(qi_ref[0, :, (h // 2) * LANES:(h // 2 + 1) * LANES] * keep[h % 2])
    qs = jnp.concatenate(parts, axis=0)
    w_t = wt_ref[0]
    krow = lax.broadcasted_iota(jnp.int32, (tile, tile), 0)
    qcol = lax.broadcasted_iota(jnp.int32, (tile, tile), 1)

    def index_tile(j, carry):
        rows = tile_rows(j)
        rel = lax.dot_general(ki_ref[0, rows, :], qs, _NT, preferred_element_type=jnp.float32)
        sc = jnp.zeros((tile, tile), jnp.float32)
        for h in range(nih):
            sc = sc + jnp.maximum(rel[:, h * tile:(h + 1) * tile], 0.0) * w_t[h:h + 1, :]
        sc = jnp.where(sc == 0.0, 0.0, sc)
        bits = lax.bitcast_convert_type(sc, jnp.int32)
        u = jnp.where(bits < 0, bits ^ jnp.int32(0x7FFFFFFF), bits)
        causal = (j * tile + krow) <= (i * tile + qcol)
        u_ref[rows, :] = jnp.where(causal, u, jnp.int32(INT_MIN))
        return carry

    lax.fori_loop(0, i + 1, index_tile, 0)

    def count(pred):
        def body(j, acc8):
            ones = jnp.where(pred(j, u_ref[tile_rows(j), :]), 1.0, 0.0)
            return acc8 + jnp.sum(ones.reshape(tile // 8, 8, tile), axis=0)
        acc8 = lax.fori_loop(0, i + 1, body, jnp.zeros((8, tile), jnp.float32))
        return jnp.sum(acc8, axis=0, keepdims=True)

    def bit_step(bi, carry):
        ans, cnt_ans = carry
        cand = ans | lax.shift_left(jnp.int32(1), 31 - bi)
        thr_c = cand ^ jnp.int32(INT_MIN)
        cnt = count(lambda j, u: u >= thr_c)
        ok = cnt >= float(topk)
        return jnp.where(ok, cand, ans), jnp.where(ok, cnt, cnt_ans)

    ans, cnt_thr = lax.fori_loop(0, 32, bit_step, (jnp.zeros((1, tile), jnp.int32),
                                                   jnp.zeros((1, tile), jnp.float32)))
    thr = ans ^ jnp.int32(INT_MIN)

    nbits = int(s_len).bit_length()
    cut_all = jnp.full((1, tile), 2 ** nbits - 1, jnp.int32)

    def tie_search():
        need = float(topk) - count(lambda j, u: u > thr)

        def cut_step(bi, cut):
            cand = cut | lax.shift_left(jnp.int32(1), nbits - 1 - bi)
            cnt = count(lambda j, u: (u == thr) & ((j * tile + krow) < cand))
            return jnp.where(cnt <= need, cand, cut)

        return lax.fori_loop(0, nbits, cut_step, jnp.zeros((1, tile), jnp.int32))

    cut = lax.cond(jnp.max(cnt_thr) > float(topk), tie_search, lambda: cut_all)

    m_ref[...] = jnp.full(m_ref.shape, NEG, jnp.float32)
    l_ref[...] = jnp.zeros(l_ref.shape, jnp.float32)
    acc_ref[...] = jnp.zeros(acc_ref.shape, jnp.float32)

    def attend_tile(j, table_ref, pen):
        rows = tile_rows(j)
        u = u_ref[rows, :]
        tied = (u == thr) & ((j * tile + krow) < cut) & (u != jnp.int32(INT_MIN))
        maskadd = jnp.where((u > thr) | tied, 0.0, NEG) + pen
        c_t = c_ref[0, rows, :]
        ct_t = ct_ref[0, j]
        maskw = jnp.concatenate([maskadd] * hg, axis=1)

        def head_group(g, carry):
            qlg = ql_ref[0, pl.ds(g * hg, hg)].reshape(hg * tile, ql_ref.shape[-1])
            logit = lax.dot_general(c_t, qlg, _NT, preferred_element_type=jnp.float32) + maskw
            if table_ref is not None:
                logit = logit + table_ref[g]
            m_old = m_ref[g]
            m_new = jnp.maximum(m_old, jnp.max(logit, axis=0, keepdims=True))
            alpha = jnp.exp(m_old - m_new)
            p = jnp.exp(logit - m_new)
            l_ref[g] = alpha * l_ref[g] + jnp.sum(p, axis=0, keepdims=True)
            m_ref[g] = m_new
            pv = jnp.dot(ct_t, p.astype(ct_t.dtype), preferred_element_type=jnp.float32)
            acc_ref[g] = acc_ref[g] * alpha + pv
            return carry

        lax.fori_loop(0, N_HEADS // hg, head_group, 0)

    def far(j, carry):
        attend_tile(j, None, 0.0)
        return carry

    lax.fori_loop(0, i - 1, far, 0)
    attend_tile(jnp.maximum(i - 1, 0), tadj_ref, jnp.where(i >= 1, 0.0, NEG))
    attend_tile(i, tdiag_ref, 0.0)

    for h in range(N_HEADS):
        g, cols = h // hg, slice((h % hg) * tile, (h % hg + 1) * tile)
        o_ref[0, h] = (acc_ref[g, :, cols] / l_ref[g, :, cols]).astype(o_ref.dtype)


def _dsa_attention(ql, c, qi, ki2, wt, rel_bias):
    b, s, r = c.shape
    tile = DSA_TILE
    nt = s // tile
    topk = min(DSA_TOPK_MAX, s // 4)
    assert s % tile == 0
    ct = jnp.transpose(c.reshape(b, nt, tile, r), (0, 1, 3, 2))
    kk = np.arange(tile)[:, None]
    qq = np.arange(tile)[None, :]
    shift = rel_bias[REL_BUCKETS - 1]
    hg = DSA_HEAD_GROUP
    ng = N_HEADS // hg

    def wide(tab):
        return jnp.transpose(tab.reshape(ng, hg, tile, tile), (0, 2, 1, 3)).reshape(ng, tile, hg * tile)

    t_diag = wide(_bias_table(rel_bias, qq - kk, qq >= kk, shift))
    t_adj = wide(_bias_table(rel_bias, tile + qq - kk, np.ones((tile, tile), bool), shift))
    ni = qi.shape[2]
    once = pl.Buffered(1)
    return pl.pallas_call(
        functools.partial(_dsa_kernel, tile=tile, topk=topk, s_len=s, hg=hg),
        out_shape=jax.ShapeDtypeStruct((b, N_HEADS, r, s), c.dtype),
        grid=(b, nt),
        in_specs=[pl.BlockSpec((1, tile, ni), lambda bb, i: (bb, i, 0)),
                  pl.BlockSpec((1, DSA_IDX_HEADS, tile), lambda bb, i: (bb, 0, i)),
                  pl.BlockSpec((1, s, LANES), lambda bb, i: (bb, 0, 0), pipeline_mode=once),
                  pl.BlockSpec((1, s, r), lambda bb, i: (bb, 0, 0), pipeline_mode=once),
                  pl.BlockSpec((1, nt, r, tile), lambda bb, i: (bb, 0, 0, 0), pipeline_mode=once),
                  pl.BlockSpec((1, N_HEADS, tile, r), lambda bb, i: (bb, 0, i, 0)),
                  pl.BlockSpec((ng, tile, hg * tile), lambda bb, i: (0, 0, 0), pipeline_mode=once),
                  pl.BlockSpec((ng, tile, hg * tile), lambda bb, i: (0, 0, 0), pipeline_mode=once)],
        out_specs=pl.BlockSpec((1, N_HEADS, r, tile), lambda bb, i: (bb, 0, 0, i)),
        scratch_shapes=[pltpu.VMEM((s, tile), jnp.int32),
                        pltpu.VMEM((ng, r, hg * tile), jnp.float32),
                        pltpu.VMEM((ng, 1, hg * tile), jnp.float32),
                        pltpu.VMEM((ng, 1, hg * tile), jnp.float32)],
        compiler_params=_params(("parallel", "arbitrary")),
        name="dsa_attention",
    )(qi, wt, ki2, c, ct, ql, t_diag, t_adj)


def _dsa_out_kernel(ol_ref, wuv_ref, wo_ref, r_ref, o_ref):
    parts = []
    for p in range(N_HEADS // 2):
        olp = ol_ref[:, p * 2 * DSA_KV_RANK:(p + 1) * 2 * DSA_KV_RANK]
        parts.append(jnp.dot(olp, wuv_ref[p], preferred_element_type=jnp.float32))
    o = jnp.concatenate(parts, axis=1).astype(wo_ref.dtype)
    o_ref[...] = r_ref[...] + jnp.dot(o, wo_ref[...], preferred_element_type=jnp.float32)


def _dsa_out(ol, w_uv, w_out, res, dt, *, tm=512):
    t, n = ol.shape
    d = w_out.shape[1]
    r = DSA_KV_RANK
    uv = jnp.transpose(w_uv, (1, 0, 2)).reshape(N_HEADS // 2, 2, r, HEAD_DIM)
    z = jnp.zeros_like(uv[:, 0])
    wuv_bd = jnp.concatenate([jnp.concatenate([uv[:, 0], z], axis=2),
                              jnp.concatenate([z, uv[:, 1]], axis=2)], axis=1).astype(dt)
    return pl.pallas_call(
        _dsa_out_kernel,
        out_shape=jax.ShapeDtypeStruct((t, d), jnp.float32),
        grid=(t // tm,),
        in_specs=[pl.BlockSpec((tm, n), lambda i: (i, 0)),
                  pl.BlockSpec((N_HEADS // 2, 2 * r, LANES), lambda i: (0, 0, 0)),
                  pl.BlockSpec((N_HEADS * HEAD_DIM, d), lambda i: (0, 0)),
                  pl.BlockSpec((tm, d), lambda i: (i, 0))],
        out_specs=pl.BlockSpec((tm, d), lambda i: (i, 0)),
        compiler_params=_params(("parallel",)),
        name="dsa_out",
    )(ol, wuv_bd, w_out.astype(dt), res)


def _dsa_layer(h, g, w_in, kv_norm, w_uk, w_uv, w_out, rel_bias, mxu_dtype):
    b, s, d = h.shape
    ql, c, qi, ki2, wt = _dsa_proj(h, g, w_in, kv_norm, w_uk, mxu_dtype)
    olt = _dsa_attention(ql, c, qi, ki2, wt, rel_bias)
    ol = jnp.transpose(olt, (0, 3, 1, 2)).reshape(b * s, N_HEADS * DSA_KV_RANK)
    return _dsa_out(ol, w_uv, w_out, h.reshape(b * s, d), mxu_dtype).reshape(b, s, d)


def kernel(x, rel_bias, norm_mix, norm_mlp, mlp_up, mlp_down, a_w_in, a_w_out, b_w_in, b_sinks, b_w_out,
           c_w_in, c_kv_norm, c_w_uk, c_w_uv, c_w_out, final_norm):
    b, s, d = x.shape
    t = b * s
    dt = MXU_DTYPE
    depth = norm_mix.shape[0]
    h = x.reshape(t, d)
    for i in range(depth):
        kind, j = i % 3, i // 3
        if kind == 0:
            qkv = _norm_proj(h, norm_mix[i], a_w_in[j].astype(dt)).reshape(b, s, -1)
            o = _moba_attention(qkv, rel_bias)
            h = _proj_residual(o.reshape(t, -1), a_w_out[j].astype(dt), h)
        elif kind == 1:
            proj = _norm_proj(h, norm_mix[i], b_w_in[j].astype(dt)).reshape(b, s, -1)
            o = _swa_attention(proj, b_sinks[j], rel_bias)
            h = _proj_residual(o.reshape(t, -1), b_w_out[j].astype(dt), h)
        else:
            h = _dsa_layer(h.reshape(b, s, d), norm_mix[i], c_w_in[j], c_kv_norm[j], c_w_uk[j],
                           c_w_uv[j], c_w_out[j], rel_bias, dt).reshape(t, d)
        h = _mlp(h, norm_mlp[i], mlp_up[i].astype(dt), mlp_down[i].astype(dt), final_norm,
                 final_norm=(i == depth - 1))
    return h.reshape(b, s, d)
```

```python
import functools
import math

import numpy as np
import jax
import jax.numpy as jnp
from jax import lax
from jax.experimental import pallas as pl
from jax.experimental.pallas import tpu as pltpu

N_HEADS = 16
HEAD_DIM = 64
NORM_EPS = 1e-6
REL_BUCKETS = 32
REL_MAX_DIST = 128
MOBA_BLOCK = 256
MOBA_TOPK = 3
MOBA_GROUP = 4
MOBA_KEY_UNIT = 128
MOBA_ONES_ROWS = 16
LOG2E = math.log2(math.e)
SWA_WINDOW = 128
SWA_BLOCK = 128
SWA_KV_HEADS = 2
DSA_KV_RANK = 256
DSA_IDX_HEADS = 8
DSA_IDX_DIM = 64
DSA_TOPK_MAX = 256
DSA_TILE = 256
DSA_HEAD_GROUP = 8
DSA_ONES_ROWS = 16

LANES = 128
MXU_DTYPE = jnp.bfloat16
NEG = -1e30
INT_MIN = -2 ** 31
VMEM_LIMIT_BYTES = 56 * 1024 * 1024

_NT = (((1,), (1,)), ((), ()))


def _params(semantics):
    return pltpu.CompilerParams(dimension_semantics=semantics, vmem_limit_bytes=VMEM_LIMIT_BYTES)


def _rmsnorm(x, g):
    var = jnp.mean(x * x, axis=-1, keepdims=True)
    return x * lax.rsqrt(var + NORM_EPS) * g


def _norm_proj_kernel(x_ref, g_ref, w_ref, o_ref):
    xn = _rmsnorm(x_ref[...], g_ref[...]).astype(w_ref.dtype)
    o_ref[...] = jnp.dot(xn, w_ref[...], preferred_element_type=jnp.float32).astype(o_ref.dtype)


def _norm_proj(x, g, w, *, tm=512):
    t, d = x.shape
    n = w.shape[1]
    assert t % tm == 0
    return pl.pallas_call(
        _norm_proj_kernel,
        out_shape=jax.ShapeDtypeStruct((t, n), w.dtype),
        grid=(t // tm,),
        in_specs=[pl.BlockSpec((tm, d), lambda i: (i, 0)),
                  pl.BlockSpec((1, d), lambda i: (0, 0)),
                  pl.BlockSpec((d, n), lambda i: (0, 0))],
        out_specs=pl.BlockSpec((tm, n), lambda i: (i, 0)),
        compiler_params=_params(("parallel",)),
        name="norm_proj",
    )(x, g.reshape(1, d), w)


def _proj_residual_kernel(a_ref, w_ref, r_ref, o_ref):
    o_ref[...] = r_ref[...] + jnp.dot(a_ref[...], w_ref[...], preferred_element_type=jnp.float32)


def _proj_residual(a, w, res, *, tm=512):
    t, k = a.shape
    n = w.shape[1]
    assert t % tm == 0
    return pl.pallas_call(
        _proj_residual_kernel,
        out_shape=jax.ShapeDtypeStruct((t, n), jnp.float32),
        grid=(t // tm,),
        in_specs=[pl.BlockSpec((tm, k), lambda i: (i, 0)),
                  pl.BlockSpec((k, n), lambda i: (0, 0)),
                  pl.BlockSpec((tm, n), lambda i: (i, 0))],
        out_specs=pl.BlockSpec((tm, n), lambda i: (i, 0)),
        compiler_params=_params(("parallel",)),
        name="proj_residual",
    )(a, w, res)


def _mlp_kernel(x_ref, g_ref, wu_ref, wd_ref, gf_ref, o_ref, xn_ref, acc_ref, *, final_norm):
    f = pl.program_id(1)

    @pl.when(f == 0)
    def _():
        x = x_ref[...]
        xn_ref[...] = _rmsnorm(x, g_ref[...]).astype(xn_ref.dtype)
        acc_ref[...] = x

    u = jnp.dot(xn_ref[...], wu_ref[...], preferred_element_type=jnp.float32)
    a = jnp.square(jnp.maximum(u, 0.0)).astype(wd_ref.dtype)
    acc_ref[...] += jnp.dot(a, wd_ref[...], preferred_element_type=jnp.float32)

    @pl.when(f == pl.num_programs(1) - 1)
    def _():
        y = acc_ref[...]
        if final_norm:
            y = _rmsnorm(y, gf_ref[...])
        o_ref[...] = y


def _mlp(x, g, w_up, w_down, g_final, *, final_norm, tm=1024, tf=1024):
    t, d = x.shape
    ff = w_up.shape[1]
    assert t % tm == 0 and ff % tf == 0
    return pl.pallas_call(
        functools.partial(_mlp_kernel, final_norm=final_norm),
        out_shape=jax.ShapeDtypeStruct((t, d), jnp.float32),
        grid=(t // tm, ff // tf),
        in_specs=[pl.BlockSpec((tm, d), lambda i, f: (i, 0)),
                  pl.BlockSpec((1, d), lambda i, f: (0, 0)),
                  pl.BlockSpec((d, tf), lambda i, f: (0, f)),
                  pl.BlockSpec((tf, d), lambda i, f: (f, 0)),
                  pl.BlockSpec((1, d), lambda i, f: (0, 0))],
        out_specs=pl.BlockSpec((tm, d), lambda i, f: (i, 0)),
        scratch_shapes=[pltpu.VMEM((tm, d), w_up.dtype), pltpu.VMEM((tm, d), jnp.float32)],
        compiler_params=_params(("parallel", "arbitrary")),
        name="mlp",
    )(x, g.reshape(1, d), w_up, w_down, g_final.reshape(1, d))


def _rel_bucket_np(dist):
    n = np.maximum(dist, 0)
    max_exact = REL_BUCKETS // 2
    nf = np.maximum(n, 1).astype(np.float64)
    large = max_exact + (np.log(nf / max_exact) / math.log(REL_MAX_DIST / max_exact)
                         * (REL_BUCKETS - max_exact)).astype(np.int64)
    large = np.minimum(large, REL_BUCKETS - 1)
    return np.where(n < max_exact, n, large).astype(np.int32)


def _bias_table(rel_bias, dist, valid, shift):
    tab = jnp.moveaxis(rel_bias[_rel_bucket_np(dist)], -1, 0)
    tab = tab - shift.reshape((-1,) + (1,) * dist.ndim)
    return jnp.where(jnp.asarray(valid)[None], tab, NEG).astype(jnp.float32)


def _moba_route_mask(gate_t, i):
    blkid = lax.broadcasted_iota(jnp.int32, gate_t.shape, 0)
    blk_f = blkid.astype(jnp.float32)
    valid = blkid < i
    g = jnp.where(valid, gate_t, -jnp.inf)
    sel = blkid == i
    for _ in range(MOBA_TOPK):
        mx = jnp.max(g, axis=0, keepdims=True)
        first = jnp.min(jnp.where(g == mx, blk_f, float(LANES)), axis=0, keepdims=True)
        pick = (blk_f == first) & valid
        sel = sel | pick
        g = jnp.where(pick, -jnp.inf, g)
    return jnp.where(sel, 0.0, NEG)


def _moba_kernel(q_ref, k_ref, v_ref, tnear_ref, o_ref, kaug_ref, vt_ref, kmrows_ref,
                 *, blk, nblk, group, ku):
    i = pl.program_id(2)
    half = LANES // 2
    pad = group * blk
    lane = lax.broadcasted_iota(jnp.int32, (blk, LANES), 1)
    lo = lane < half

    @pl.when(i == 0)
    def _():
        kmrows_ref[...] = jnp.zeros(kmrows_ref.shape, kmrows_ref.dtype)
        lane1 = lax.broadcasted_iota(jnp.int32, (1, LANES), 1)
        lane_p = lax.broadcasted_iota(jnp.int32, (pad, LANES), 1)
        kaug_ref[0, 0:pad, :] = jnp.where(lane_p == LANES - 1, 1.0, 0.0).astype(kaug_ref.dtype)
        kaug_ref[1, 0:pad, :] = jnp.where(lane_p == half - 1, 1.0, 0.0).astype(kaug_ref.dtype)
        vt_ref[0:pad // ku] = jnp.zeros((pad // ku,) + vt_ref.shape[1:], vt_ref.dtype)
        ones = jnp.ones((vt_ref.shape[2] - half, ku), vt_ref.dtype)

        def build(n, carry):
            rows = pl.ds(pl.multiple_of(n * blk, blk), blk)
            prows = pl.ds(pl.multiple_of(n * blk + pad, blk), blk)
            kn = k_ref[0, rows, :].astype(jnp.float32)
            kaug_ref[0, prows, :] = jnp.where(
                lo, kn, jnp.where(lane == half + n, 1.0, 0.0)).astype(kaug_ref.dtype)
            kaug_ref[1, prows, :] = jnp.where(
                lo, jnp.where(lane == n, 1.0, 0.0), kn).astype(kaug_ref.dtype)
            for t in range(blk // ku):
                vn = v_ref[0, pl.ds(pl.multiple_of(n * blk + t * ku, ku), ku), :]
                vn_t = vn.astype(jnp.float32).T.astype(vt_ref.dtype)
                unit = (n + group) * (blk // ku) + t
                for hh in range(2):
                    vt_ref[unit, hh, 0:half, :] = vn_t[hh * half:(hh + 1) * half]
                    vt_ref[unit, hh, half:, :] = ones
            mean = jnp.sum(kn, axis=0, keepdims=True) * (1.0 / blk)
            kmrows_ref[0, pl.ds(n, 1), :] = jnp.where(lane1 < half, mean, 0.0)
            kmrows_ref[1, pl.ds(n, 1), :] = jnp.where(lane1 < half, 0.0, mean)
            return carry

        lax.fori_loop(0, nblk, build, 0)

    dt = q_ref.dtype
    q_t = q_ref[0].astype(jnp.float32).T
    q_tb = q_t.astype(dt)
    q_aug = []
    for hh in range(2):
        gate_t = jnp.dot(kmrows_ref[hh].astype(dt), q_tb,
                         preferred_element_type=jnp.float32)
        route_t = _moba_route_mask(gate_t, i)
        q_h = q_t[hh * half:(hh + 1) * half]
        q_aug.append(jnp.concatenate([q_h, route_t] if hh == 0 else [route_t, q_h], axis=0).astype(dt))

    upb = blk // ku

    def fold_span(unit0, n_units, state, bias_ref=None):
        rows = pl.ds(pl.multiple_of(unit0 * ku, ku), n_units * ku)
        s = [jnp.dot(kaug_ref[hh, rows, :], q_aug[hh], preferred_element_type=jnp.float32)
             for hh in range(2)]
        state = list(state)
        for t in range(n_units):
            for hh in range(2):
                m, acc = state[hh]
                s_t = s[hh][t * ku:(t + 1) * ku]
                if bias_ref is not None:
                    s_t = s_t + bias_ref[hh, t * ku:(t + 1) * ku, :]
                m_new = jnp.maximum(m, jnp.max(s_t, axis=0, keepdims=True))
                p = jnp.exp2(s_t - m_new).astype(dt)
                acc = jnp.exp2(m - m_new) * acc + jnp.dot(vt_ref[unit0 + t, hh], p,
                                                          preferred_element_type=jnp.float32)
                state[hh] = (m_new, acc)
        return state

    start = (jnp.full((1, blk), NEG, jnp.float32), jnp.zeros((vt_ref.shape[2], blk), jnp.float32))
    state = fold_span((i + group - 1) * upb, 2 * upb, [start, start], tnear_ref)

    def far(g, carry):
        st = fold_span((i - 1 - group * g) * upb, group * upb, [carry[0:2], carry[2:4]])
        return tuple(st[0]) + tuple(st[1])

    out = lax.fori_loop(0, (i + group - 2) // group, far, tuple(state[0]) + tuple(state[1]))
    o_t = jnp.concatenate([out[1][:half] / out[1][half:half + 1],
                           out[3][:half] / out[3][half:half + 1]], axis=0)
    o_ref[0] = o_t.T.astype(o_ref.dtype)


def _moba_in_weights(w_in, dt):
    nq = N_HEADS * HEAD_DIM
    return jnp.concatenate([w_in[:, :nq] * (LOG2E * HEAD_DIM ** -0.5), w_in[:, nq:]], axis=1).astype(dt)


def _moba_attention(qkv, rel_bias):
    b, s, _ = qkv.shape
    blk = MOBA_BLOCK
    group = MOBA_GROUP
    nblk = s // blk
    hp = N_HEADS // 2
    ku = MOBA_KEY_UNIT
    assert s % blk == 0 and nblk < LANES // 2 and 2 * HEAD_DIM == LANES and blk % ku == 0
    kk = np.arange(blk)[:, None]
    qq = np.arange(blk)[None, :]
    shift = rel_bias[REL_BUCKETS - 1]
    t_own = _bias_table(rel_bias, qq - kk, qq >= kk, shift)
    t_adj = _bias_table(rel_bias, blk + qq - kk, np.ones((blk, blk), bool), shift)
    t_near = jnp.concatenate([t_adj, t_own], axis=1) * LOG2E
    return pl.pallas_call(
        functools.partial(_moba_kernel, blk=blk, nblk=nblk, group=group, ku=ku),
        out_shape=jax.ShapeDtypeStruct((b, s, N_HEADS * HEAD_DIM), qkv.dtype),
        grid=(b, hp, nblk),
        in_specs=[pl.BlockSpec((1, blk, LANES), lambda bb, p, i: (bb, i, p)),
                  pl.BlockSpec((1, s, LANES), lambda bb, p, i: (bb, 0, hp + p)),
                  pl.BlockSpec((1, s, LANES), lambda bb, p, i: (bb, 0, 2 * hp + p)),
                  pl.BlockSpec((2, 2 * blk, blk), lambda bb, p, i: (p, 0, 0))],
        out_specs=pl.BlockSpec((1, blk, LANES), lambda bb, p, i: (bb, i, p)),
        scratch_shapes=[pltpu.VMEM((2, s + group * blk, LANES), qkv.dtype),
                        pltpu.VMEM(((s + group * blk) // ku, 2, HEAD_DIM + MOBA_ONES_ROWS, ku), qkv.dtype),
                        pltpu.VMEM((2, LANES // 2, LANES), jnp.float32)],
        compiler_params=_params(("parallel", "parallel", "arbitrary")),
        name="moba_attention",
    )(qkv, qkv, qkv, t_near)


def _swa_kernel(q_ref, kp_ref, kc_ref, vp_ref, vc_ref, tab_ref, sink_ref, o_ref, *, blk, group):
    n = pl.program_id(1)
    half = LANES // 2
    lane = lax.broadcasted_iota(jnp.int32, (blk, LANES), 1)
    lo = lane < half
    lane_k = lax.broadcasted_iota(jnp.int32, (2 * blk, LANES), 1)
    lo_k = lane_k < half
    col = lax.broadcasted_iota(jnp.int32, (1, 2 * blk), 1)
    colpen = jnp.where((n == 0) & (col < blk), NEG, 0.0)

    dt = q_ref.dtype
    kband = jnp.concatenate([kp_ref[0], kc_ref[0]], axis=0).astype(jnp.float32)
    vband = jnp.concatenate([vp_ref[0], vc_ref[0]], axis=0).astype(jnp.float32)
    kswap = pltpu.roll(kband, half, 1)
    vswap = pltpu.roll(vband, half, 1)
    lane1 = lax.broadcasted_iota(jnp.int32, (1, LANES), 1)
    scale = HEAD_DIM ** -0.5
    keep = (jnp.where(lane1 < half, scale, 0.0).astype(dt), jnp.where(lane1 < half, 0.0, scale).astype(dt))
    rows_g = group * blk

    for kv in range(SWA_KV_HEADS):
        k2 = (jnp.where(lo_k, kband, kswap) if kv == 0 else jnp.where(lo_k, kswap, kband)).astype(dt)
        v2 = (jnp.where(lo_k, vband, vswap) if kv == 0 else jnp.where(lo_k, vswap, vband)).astype(dt)
        parts = []
        for g in range(group):
            h = kv * group + g
            parts.append(q_ref[0, :, (h // 2) * LANES:(h // 2 + 1) * LANES] * keep[h % 2])
        qs = jnp.concatenate(parts, axis=0)
        s = lax.dot_general(qs, k2, _NT, preferred_element_type=jnp.float32)
        s = s + tab_ref[kv * rows_g:(kv + 1) * rows_g, :] + colpen
        sink = sink_ref[kv * rows_g:(kv + 1) * rows_g, 0:1]
        m = jnp.maximum(jnp.max(s, axis=1, keepdims=True), sink)
        p = jnp.exp(s - m)
        l = jnp.sum(p, axis=1, keepdims=True) + jnp.exp(sink - m)
        o = jnp.dot(p.astype(v2.dtype), v2, preferred_element_type=jnp.float32) / l
        for g in range(0, group, 2):
            h = kv * group + g
            pair = jnp.where(lo, o[g * blk:(g + 1) * blk], o[(g + 1) * blk:(g + 2) * blk])
            o_ref[0, :, (h // 2) * LANES:(h // 2 + 1) * LANES] = pair.astype(o_ref.dtype)


def _swa_attention(proj, sinks, rel_bias):
    b, s, _ = proj.shape
    blk = SWA_BLOCK
    nb = s // blk
    group = N_HEADS // SWA_KV_HEADS
    nq = N_HEADS * HEAD_DIM
    assert s % blk == 0 and SWA_KV_HEADS * HEAD_DIM == LANES and SWA_WINDOW <= blk
    kcol = nq // LANES
    dist = blk + np.arange(blk)[:, None] - np.arange(2 * blk)[None, :]
    tab = _bias_table(rel_bias, dist, (dist >= 0) & (dist < SWA_WINDOW), jnp.zeros_like(sinks))
    tab = tab.reshape(N_HEADS * blk, 2 * blk)
    sink_rows = jnp.broadcast_to(sinks.astype(jnp.float32)[:, None, None],
                                 (N_HEADS, blk, LANES)).reshape(N_HEADS * blk, LANES)
    prev = lambda bb, n: (bb, jnp.maximum(n - 1, 0), kcol)
    cur = lambda bb, n: (bb, n, kcol)
    prev_v = lambda bb, n: (bb, jnp.maximum(n - 1, 0), kcol + 1)
    cur_v = lambda bb, n: (bb, n, kcol + 1)
    return pl.pallas_call(
        functools.partial(_swa_kernel, blk=blk, group=group),
        out_shape=jax.ShapeDtypeStruct((b, s, nq), proj.dtype),
        grid=(b, nb),
        in_specs=[pl.BlockSpec((1, blk, nq), lambda bb, n: (bb, n, 0)),
                  pl.BlockSpec((1, blk, LANES), prev),
                  pl.BlockSpec((1, blk, LANES), cur),
                  pl.BlockSpec((1, blk, LANES), prev_v),
                  pl.BlockSpec((1, blk, LANES), cur_v),
                  pl.BlockSpec((N_HEADS * blk, 2 * blk), lambda bb, n: (0, 0)),
                  pl.BlockSpec((N_HEADS * blk, LANES), lambda bb, n: (0, 0))],
        out_specs=pl.BlockSpec((1, blk, nq), lambda bb, n: (bb, n, 0)),
        compiler_params=_params(("parallel", "arbitrary")),
        name="swa_attention",
    )(proj, proj, proj, proj, proj, tab, sink_rows)


def _dsa_proj_kernel(x_ref, g_ref, w_ref, wwt_ref, kvn_ref, wuk_ref,
                     ql_ref, c_ref, qi_ref, ki_ref, wt_ref, *, idx_scale):
    dt = w_ref.dtype
    nq = N_HEADS * HEAD_DIM
    r = DSA_KV_RANK
    ni = DSA_IDX_HEADS * DSA_IDX_DIM
    xn = _rmsnorm(x_ref[0], g_ref[...]).astype(dt)
    y = jnp.dot(xn, w_ref[...], preferred_element_type=jnp.float32)
    scale = LOG2E * HEAD_DIM ** -0.5
    for p in range(N_HEADS // 2):
        qp = y[:, p * LANES:(p + 1) * LANES].astype(dt)
        ql = jnp.dot(qp, wuk_ref[p], preferred_element_type=jnp.float32) * scale
        ql_ref[0, 2 * p] = ql[:, :r].astype(ql_ref.dtype)
        ql_ref[0, 2 * p + 1] = ql[:, r:].astype(ql_ref.dtype)
    c_ref[0] = _rmsnorm(y[:, nq:nq + r], kvn_ref[...]).astype(c_ref.dtype)
    qi_ref[0] = y[:, nq + r:nq + r + ni].astype(qi_ref.dtype)
    ki_ref[0] = y[:, nq + r + ni:nq + r + ni + LANES].astype(ki_ref.dtype)
    wt = lax.dot_general(wwt_ref[...], xn, _NT, preferred_element_type=jnp.float32)
    wt_ref[0] = wt * idx_scale


def _dsa_proj(x, g, w_in, kv_norm, w_uk, dt, *, tm=512):
    b, s, d = x.shape
    nq = N_HEADS * HEAD_DIM
    r = DSA_KV_RANK
    ni = DSA_IDX_HEADS * DSA_IDX_DIM
    di = DSA_IDX_DIM
    assert 2 * di == LANES and 2 * HEAD_DIM == LANES and s % tm == 0
    k_idx = w_in[:, nq + r + ni:nq + r + ni + di]
    w_main = jnp.concatenate([w_in[:, :nq + r + ni], k_idx, k_idx], axis=1).astype(dt)
    wwt = w_in[:, nq + r + ni + di:].T.astype(dt)
    uk = jnp.transpose(w_uk, (1, 2, 0)).reshape(N_HEADS // 2, 2, HEAD_DIM, r)
    z = jnp.zeros_like(uk[:, 0])
    wuk_bd = jnp.concatenate([jnp.concatenate([uk[:, 0], z], axis=2),
                              jnp.concatenate([z, uk[:, 1]], axis=2)], axis=1).astype(dt)
    nw = w_main.shape[1]
    idx_scale = DSA_IDX_HEADS ** -0.5 * DSA_IDX_DIM ** -0.5
    return pl.pallas_call(
        functools.partial(_dsa_proj_kernel, idx_scale=idx_scale),
        out_shape=(jax.ShapeDtypeStruct((b, N_HEADS, s, r), dt),
                   jax.ShapeDtypeStruct((b, s, r), dt),
                   jax.ShapeDtypeStruct((b, s, ni), dt),
                   jax.ShapeDtypeStruct((b, s, LANES), dt),
                   jax.ShapeDtypeStruct((b, DSA_IDX_HEADS, s), jnp.float32)),
        grid=(b, s // tm),
        in_specs=[pl.BlockSpec((1, tm, d), lambda bb, i: (bb, i, 0)),
                  pl.BlockSpec((1, d), lambda bb, i: (0, 0)),
                  pl.BlockSpec((d, nw), lambda bb, i: (0, 0)),
                  pl.BlockSpec((DSA_IDX_HEADS, d), lambda bb, i: (0, 0)),
                  pl.BlockSpec((1, r), lambda bb, i: (0, 0)),
                  pl.BlockSpec((N_HEADS // 2, LANES, 2 * r), lambda bb, i: (0, 0, 0))],
        out_specs=(pl.BlockSpec((1, N_HEADS, tm, r), lambda bb, i: (bb, 0, i, 0)),
                   pl.BlockSpec((1, tm, r), lambda bb, i: (bb, i, 0)),
                   pl.BlockSpec((1, tm, ni), lambda bb, i: (bb, i, 0)),
                   pl.BlockSpec((1, tm, LANES), lambda bb, i: (bb, i, 0)),
                   pl.BlockSpec((1, DSA_IDX_HEADS, tm), lambda bb, i: (bb, 0, i))),
        compiler_params=_params(("parallel", "parallel")),
        name="dsa_proj",
    )(x, g.reshape(1, d), w_main, wwt, kv_norm.reshape(1, r), wuk_bd)


def _dsa_kernel(qi_ref, wt_ref, ki_ref, c_ref, ct_ref, ql_ref, tdiag_ref, tadj_ref, o_ref,
                u_ref, acc_ref, m_ref, *, tile, topk, s_len, hg):
    i = pl.program_id(1)
    half = LANES // 2
    nih = DSA_IDX_HEADS

    def tile_rows(j):
        return pl.ds(pl.multiple_of(j * tile, tile), tile)

    lane1 = lax.broadcasted_iota(jnp.int32, (1, LANES), 1)
    keep = (jnp.where(lane1 < half, 1.0, 0.0).astype(qi_ref.dtype),
            jnp.where(lane1 < half, 0.0, 1.0).astype(qi_ref.dtype))
    parts = []
    for h in range(nih):
        parts.append(qi_ref[0, :, (h // 2) * LANES:(h // 2 + 1) * LANES] * keep[h % 2])
    qs = jnp.concatenate(parts, axis=0)
    w_t = wt_ref[0]
    krow = lax.broadcasted_iota(jnp.int32, (tile, tile), 0)
    qcol = lax.broadcasted_iota(jnp.int32, (tile, tile), 1)

    def index_tile(j, carry):
        rows = tile_rows(j)
        rel = lax.dot_general(ki_ref[0, rows, :], qs, _NT, preferred_element_type=jnp.float32)
        sc = jnp.zeros((tile, tile), jnp.float32)
        for h in range(nih):
            sc = sc + jnp.maximum(rel[:, h * tile:(h + 1) * tile], 0.0) * w_t[h:h + 1, :]
        sc = jnp.where(sc == 0.0, 0.0, sc)
        bits = lax.bitcast_convert_type(sc, jnp.int32)
        u = jnp.where(bits < 0, bits ^ jnp.int32(0x7FFFFFFF), bits)
        causal = (j * tile + krow) <= (i * tile + qcol)
        u_ref[rows, :] = jnp.where(causal, u, jnp.int32(INT_MIN))
        return carry

    lax.fori_loop(0, i + 1, index_tile, 0)

    def count(pred):
        def body(j, acc8):
            ones = jnp.where(pred(j, u_ref[tile_rows(j), :]), 1.0, 0.0)
            return acc8 + jnp.sum(ones.reshape(tile // 8, 8, tile), axis=0)
        acc8 = lax.fori_loop(0, i + 1, body, jnp.zeros((8, tile), jnp.float32))
        return jnp.sum(acc8, axis=0, keepdims=True)

    def bit_step(bi, carry):
        ans, cnt_ans = carry
        cand = ans | lax.shift_left(jnp.int32(1), 31 - bi)
        thr_c = cand ^ jnp.int32(INT_MIN)
        cnt = count(lambda j, u: u >= thr_c)
        ok = cnt >= float(topk)
        return jnp.where(ok, cand, ans), jnp.where(ok, cnt, cnt_ans)

    ans, cnt_thr = lax.fori_loop(0, 32, bit_step, (jnp.zeros((1, tile), jnp.int32),
                                                   jnp.zeros((1, tile), jnp.float32)))
    thr = ans ^ jnp.int32(INT_MIN)

    nbits = int(s_len).bit_length()
    cut_all = jnp.full((1, tile), 2 ** nbits - 1, jnp.int32)

    def tie_search():
        need = float(topk) - count(lambda j, u: u > thr)

        def cut_step(bi, cut):
            cand = cut | lax.shift_left(jnp.int32(1), nbits - 1 - bi)
            cnt = count(lambda j, u: (u == thr) & ((j * tile + krow) < cand))
            return jnp.where(cnt <= need, cand, cut)

        return lax.fori_loop(0, nbits, cut_step, jnp.zeros((1, tile), jnp.int32))

    cut = lax.cond(jnp.max(cnt_thr) > float(topk), tie_search, lambda: cut_all)

    m_ref[...] = jnp.full(m_ref.shape, NEG, jnp.float32)
    acc_ref[...] = jnp.zeros(acc_ref.shape, jnp.float32)

    def attend_tile(j, table_ref, pen):
        rows = tile_rows(j)
        u = u_ref[rows, :]
        tied = (u == thr) & ((j * tile + krow) < cut) & (u != jnp.int32(INT_MIN))
        maskadd = jnp.where((u > thr) | tied, 0.0, NEG) + pen
        c_t = c_ref[0, rows, :]
        ct_t = ct_ref[0, j]

        for g in range(N_HEADS // hg):
            qlg = ql_ref[0, g * hg:(g + 1) * hg].reshape(hg * tile, ql_ref.shape[-1])
            logits = lax.dot_general(c_t, qlg, _NT, preferred_element_type=jnp.float32)
            for hl in range(hg):
                cols = slice(hl * tile, (hl + 1) * tile)
                s = logits[:, cols] + maskadd
                if table_ref is not None:
                    s = s + table_ref[g, :, cols]
                m_old = m_ref[g, :, cols]
                m_new = jnp.maximum(m_old, jnp.max(s, axis=0, keepdims=True))
                p = jnp.exp2(s - m_new).astype(ct_t.dtype)
                pv = jnp.dot(ct_t, p, preferred_element_type=jnp.float32)
                acc_ref[g, :, cols] = jnp.exp2(m_old - m_new) * acc_ref[g, :, cols] + pv
                m_ref[g, :, cols] = m_new

    def far(j, carry):
        attend_tile(j, None, 0.0)
        return carry

    lax.fori_loop(0, i - 1, far, 0)
    attend_tile(jnp.maximum(i - 1, 0), tadj_ref, jnp.where(i >= 1, 0.0, NEG))
    attend_tile(i, tdiag_ref, 0.0)

    rank = o_ref.shape[2]
    for h in range(N_HEADS):
        g, cols = h // hg, slice((h % hg) * tile, (h % hg + 1) * tile)
        o_ref[0, h] = (acc_ref[g, 0:rank, cols] / acc_ref[g, rank:rank + 1, cols]).astype(o_ref.dtype)


def _dsa_attention(ql, c, qi, ki2, wt, rel_bias):
    b, s, r = c.shape
    tile = DSA_TILE
    nt = s // tile
    topk = min(DSA_TOPK_MAX, s // 4)
    assert s % tile == 0
    ct = jnp.transpose(c.reshape(b, nt, tile, r), (0, 1, 3, 2))
    ct = jnp.concatenate([ct, jnp.ones((b, nt, DSA_ONES_ROWS, tile), ct.dtype)], axis=2)
    rp = r + DSA_ONES_ROWS
    kk = np.arange(tile)[:, None]
    qq = np.arange(tile)[None, :]
    shift = rel_bias[REL_BUCKETS - 1]
    hg = DSA_HEAD_GROUP
    ng = N_HEADS // hg

    def wide(tab):
        return jnp.transpose(tab.reshape(ng, hg, tile, tile), (0, 2, 1, 3)).reshape(ng, tile, hg * tile)

    t_diag = wide(_bias_table(rel_bias, qq - kk, qq >= kk, shift)) * LOG2E
    t_adj = wide(_bias_table(rel_bias, tile + qq - kk, np.ones((tile, tile), bool), shift)) * LOG2E
    ni = qi.shape[2]
    once = pl.Buffered(1)
    return pl.pallas_call(
        functools.partial(_dsa_kernel, tile=tile, topk=topk, s_len=s, hg=hg),
        out_shape=jax.ShapeDtypeStruct((b, N_HEADS, r, s), c.dtype),
        grid=(b, nt),
        in_specs=[pl.BlockSpec((1, tile, ni), lambda bb, i: (bb, i, 0)),
                  pl.BlockSpec((1, DSA_IDX_HEADS, tile), lambda bb, i: (bb, 0, i)),
                  pl.BlockSpec((1, s, LANES), lambda bb, i: (bb, 0, 0), pipeline_mode=once),
                  pl.BlockSpec((1, s, r), lambda bb, i: (bb, 0, 0), pipeline_mode=once),
                  pl.BlockSpec((1, nt, rp, tile), lambda bb, i: (bb, 0, 0, 0), pipeline_mode=once),
                  pl.BlockSpec((1, N_HEADS, tile, r), lambda bb, i: (bb, 0, i, 0)),
                  pl.BlockSpec((ng, tile, hg * tile), lambda bb, i: (0, 0, 0), pipeline_mode=once),
                  pl.BlockSpec((ng, tile, hg * tile), lambda bb, i: (0, 0, 0), pipeline_mode=once)],
        out_specs=pl.BlockSpec((1, N_HEADS, r, tile), lambda bb, i: (bb, 0, 0, i)),
        scratch_shapes=[pltpu.VMEM((s, tile), jnp.int32),
                        pltpu.VMEM((ng, rp, hg * tile), jnp.float32),
                        pltpu.VMEM((ng, 1, hg * tile), jnp.float32)],
        compiler_params=_params(("parallel", "arbitrary")),
        name="dsa_attention",
    )(qi, wt, ki2, c, ct, ql, t_diag, t_adj)


def _dsa_out_kernel(ol_ref, wuv_ref, wo_ref, r_ref, o_ref):
    parts = []
    for p in range(N_HEADS // 2):
        olp = ol_ref[:, p * 2 * DSA_KV_RANK:(p + 1) * 2 * DSA_KV_RANK]
        parts.append(jnp.dot(olp, wuv_ref[p], preferred_element_type=jnp.float32))
    o = jnp.concatenate(parts, axis=1).astype(wo_ref.dtype)
    o_ref[...] = r_ref[...] + jnp.dot(o, wo_ref[...], preferred_element_type=jnp.float32)


def _dsa_out(ol, w_uv, w_out, res, dt, *, tm=512):
    t, n = ol.shape
    d = w_out.shape[1]
    r = DSA_KV_RANK
    uv = jnp.transpose(w_uv, (1, 0, 2)).reshape(N_HEADS // 2, 2, r, HEAD_DIM)
    z = jnp.zeros_like(uv[:, 0])
    wuv_bd = jnp.concatenate([jnp.concatenate([uv[:, 0], z], axis=2),
                              jnp.concatenate([z, uv[:, 1]], axis=2)], axis=1).astype(dt)
    return pl.pallas_call(
        _dsa_out_kernel,
        out_shape=jax.ShapeDtypeStruct((t, d), jnp.float32),
        grid=(t // tm,),
        in_specs=[pl.BlockSpec((tm, n), lambda i: (i, 0)),
                  pl.BlockSpec((N_HEADS // 2, 2 * r, LANES), lambda i: (0, 0, 0)),
                  pl.BlockSpec((N_HEADS * HEAD_DIM, d), lambda i: (0, 0)),
                  pl.BlockSpec((tm, d), lambda i: (i, 0))],
        out_specs=pl.BlockSpec((tm, d), lambda i: (i, 0)),
        compiler_params=_params(("parallel",)),
        name="dsa_out",
    )(ol, wuv_bd, w_out.astype(dt), res)


def _dsa_layer(h, g, w_in, kv_norm, w_uk, w_uv, w_out, rel_bias, mxu_dtype):
    b, s, d = h.shape
    ql, c, qi, ki2, wt = _dsa_proj(h, g, w_in, kv_norm, w_uk, mxu_dtype)
    olt = _dsa_attention(ql, c, qi, ki2, wt, rel_bias)
    ol = jnp.transpose(olt, (0, 3, 1, 2)).reshape(b * s, N_HEADS * DSA_KV_RANK)
    return _dsa_out(ol, w_uv, w_out, h.reshape(b * s, d), mxu_dtype).reshape(b, s, d)


def kernel(x, rel_bias, norm_mix, norm_mlp, mlp_up, mlp_down, a_w_in, a_w_out, b_w_in, b_sinks, b_w_out,
           c_w_in, c_kv_norm, c_w_uk, c_w_uv, c_w_out, final_norm):
    b, s, d = x.shape
    t = b * s
    dt = MXU_DTYPE
    depth = norm_mix.shape[0]
    h = x.reshape(t, d)
    for i in range(depth):
        kind, j = i % 3, i // 3
        if kind == 0:
            qkv = _norm_proj(h, norm_mix[i], _moba_in_weights(a_w_in[j], dt)).reshape(b, s, -1)
            o = _moba_attention(qkv, rel_bias)
            h = _proj_residual(o.reshape(t, -1), a_w_out[j].astype(dt), h)
        elif kind == 1:
            proj = _norm_proj(h, norm_mix[i], b_w_in[j].astype(dt)).reshape(b, s, -1)
            o = _swa_attention(proj, b_sinks[j], rel_bias)
            h = _proj_residual(o.reshape(t, -1), b_w_out[j].astype(dt), h)
        else:
            h = _dsa_layer(h.reshape(b, s, d), norm_mix[i], c_w_in[j], c_kv_norm[j], c_w_uk[j],
                           c_w_uv[j], c_w_out[j], rel_bias, dt).reshape(t, d)
        h = _mlp(h, norm_mlp[i], mlp_up[i].astype(dt), mlp_down[i].astype(dt), final_norm,
                 final_norm=(i == depth - 1))
    return h.reshape(b, s, d)
```

```python
import functools
import math

import numpy as np
import jax
import jax.numpy as jnp
from jax import lax
from jax.experimental import pallas as pl
from jax.experimental.pallas import tpu as pltpu

N_HEADS = 16
HEAD_DIM = 64
NORM_EPS = 1e-6
REL_BUCKETS = 32
REL_MAX_DIST = 128
MOBA_BLOCK = 256
MOBA_TOPK = 3
MOBA_GROUP = 4
MOBA_KEY_UNIT = 128
MOBA_ONES_ROWS = 16
LOG2E = math.log2(math.e)
SWA_WINDOW = 128
SWA_BLOCK = 128
SWA_KV_HEADS = 2
SWA_ONES_ROWS = 16
DSA_KV_RANK = 256
DSA_IDX_HEADS = 8
DSA_IDX_DIM = 64
DSA_TOPK_MAX = 256
DSA_TILE = 256
DSA_HEAD_GROUP = 8
DSA_ONES_ROWS = 16
DSA_COUNT_UNROLL = 4
DSA_BITS_PER_CHECK = 4

LANES = 128
MXU_DTYPE = jnp.bfloat16
NEG = -1e30
INT_MIN = -2 ** 31
VMEM_LIMIT_BYTES = 56 * 1024 * 1024

_NT = (((1,), (1,)), ((), ()))


def _params(semantics):
    return pltpu.CompilerParams(dimension_semantics=semantics, vmem_limit_bytes=VMEM_LIMIT_BYTES)


def _rmsnorm(x, g):
    var = jnp.mean(x * x, axis=-1, keepdims=True)
    return x * lax.rsqrt(var + NORM_EPS) * g


def _norm_proj_kernel(x_ref, g_ref, w_ref, o_ref):
    xn = _rmsnorm(x_ref[...], g_ref[...]).astype(w_ref.dtype)
    o_ref[...] = jnp.dot(xn, w_ref[...], preferred_element_type=jnp.float32).astype(o_ref.dtype)


def _norm_proj(x, g, w, *, tm=512):
    t, d = x.shape
    n = w.shape[1]
    assert t % tm == 0
    return pl.pallas_call(
        _norm_proj_kernel,
        out_shape=jax.ShapeDtypeStruct((t, n), w.dtype),
        grid=(t // tm,),
        in_specs=[pl.BlockSpec((tm, d), lambda i: (i, 0)),
                  pl.BlockSpec((1, d), lambda i: (0, 0)),
                  pl.BlockSpec((d, n), lambda i: (0, 0))],
        out_specs=pl.BlockSpec((tm, n), lambda i: (i, 0)),
        compiler_params=_params(("parallel",)),
        name="norm_proj",
    )(x, g.reshape(1, d), w)


def _proj_residual_kernel(a_ref, w_ref, r_ref, o_ref):
    o_ref[...] = r_ref[...] + jnp.dot(a_ref[...], w_ref[...], preferred_element_type=jnp.float32)


def _proj_residual(a, w, res, *, tm=512):
    t, k = a.shape
    n = w.shape[1]
    assert t % tm == 0
    return pl.pallas_call(
        _proj_residual_kernel,
        out_shape=jax.ShapeDtypeStruct((t, n), jnp.float32),
        grid=(t // tm,),
        in_specs=[pl.BlockSpec((tm, k), lambda i: (i, 0)),
                  pl.BlockSpec((k, n), lambda i: (0, 0)),
                  pl.BlockSpec((tm, n), lambda i: (i, 0))],
        out_specs=pl.BlockSpec((tm, n), lambda i: (i, 0)),
        compiler_params=_params(("parallel",)),
        name="proj_residual",
    )(a, w, res)


def _mlp_kernel(x_ref, g_ref, wu_ref, wd_ref, gf_ref, o_ref, xn_ref, acc_ref, *, final_norm):
    f = pl.program_id(1)

    @pl.when(f == 0)
    def _():
        x = x_ref[...]
        xn_ref[...] = _rmsnorm(x, g_ref[...]).astype(xn_ref.dtype)
        acc_ref[...] = x

    u = jnp.dot(xn_ref[...], wu_ref[...], preferred_element_type=jnp.float32)
    a = jnp.square(jnp.maximum(u, 0.0)).astype(wd_ref.dtype)
    acc_ref[...] += jnp.dot(a, wd_ref[...], preferred_element_type=jnp.float32)

    @pl.when(f == pl.num_programs(1) - 1)
    def _():
        y = acc_ref[...]
        if final_norm:
            y = _rmsnorm(y, gf_ref[...])
        o_ref[...] = y


def _mlp(x, g, w_up, w_down, g_final, *, final_norm, tm=1024, tf=1024):
    t, d = x.shape
    ff = w_up.shape[1]
    assert t % tm == 0 and ff % tf == 0
    return pl.pallas_call(
        functools.partial(_mlp_kernel, final_norm=final_norm),
        out_shape=jax.ShapeDtypeStruct((t, d), jnp.float32),
        grid=(t // tm, ff // tf),
        in_specs=[pl.BlockSpec((tm, d), lambda i, f: (i, 0)),
                  pl.BlockSpec((1, d), lambda i, f: (0, 0)),
                  pl.BlockSpec((d, tf), lambda i, f: (0, f)),
                  pl.BlockSpec((tf, d), lambda i, f: (f, 0)),
                  pl.BlockSpec((1, d), lambda i, f: (0, 0))],
        out_specs=pl.BlockSpec((tm, d), lambda i, f: (i, 0)),
        scratch_shapes=[pltpu.VMEM((tm, d), w_up.dtype), pltpu.VMEM((tm, d), jnp.float32)],
        compiler_params=_params(("parallel", "arbitrary")),
        name="mlp",
    )(x, g.reshape(1, d), w_up, w_down, g_final.reshape(1, d))


def _rel_bucket_np(dist):
    n = np.maximum(dist, 0)
    max_exact = REL_BUCKETS // 2
    nf = np.maximum(n, 1).astype(np.float64)
    large = max_exact + (np.log(nf / max_exact) / math.log(REL_MAX_DIST / max_exact)
                         * (REL_BUCKETS - max_exact)).astype(np.int64)
    large = np.minimum(large, REL_BUCKETS - 1)
    return np.where(n < max_exact, n, large).astype(np.int32)


def _bias_table(rel_bias, dist, valid, shift):
    bucket = jnp.asarray(_rel_bucket_np(dist).reshape(1, -1))
    onehot = (bucket == jnp.arange(REL_BUCKETS, dtype=bucket.dtype).reshape(-1, 1)).astype(jnp.float32)
    tab = jnp.einsum("bh,bn->hn", rel_bias, onehot, precision=lax.Precision.HIGHEST)
    tab = tab.reshape((-1,) + dist.shape) - shift.reshape((-1,) + (1,) * dist.ndim)
    return jnp.where(jnp.asarray(valid)[None], tab, NEG).astype(jnp.float32)


def _moba_route_mask(gate_t, i):
    blkid = lax.broadcasted_iota(jnp.int32, gate_t.shape, 0)
    blk_f = blkid.astype(jnp.float32)
    valid = blkid < i
    g = jnp.where(valid, gate_t, -jnp.inf)
    sel = blkid == i
    for _ in range(MOBA_TOPK):
        mx = jnp.max(g, axis=0, keepdims=True)
        first = jnp.min(jnp.where(g == mx, blk_f, float(LANES)), axis=0, keepdims=True)
        pick = (blk_f == first) & valid
        sel = sel | pick
        g = jnp.where(pick, -jnp.inf, g)
    return jnp.where(sel, 0.0, NEG)


def _moba_kernel(q_ref, k_ref, v_ref, tnear_ref, o_ref, kaug_ref, vt_ref, kmrows_ref,
                 *, blk, nblk, group, ku):
    i = pl.program_id(2)
    half = LANES // 2
    pad = group * blk
    lane = lax.broadcasted_iota(jnp.int32, (blk, LANES), 1)
    lo = lane < half

    @pl.when(i == 0)
    def _():
        kmrows_ref[...] = jnp.zeros(kmrows_ref.shape, kmrows_ref.dtype)
        lane1 = lax.broadcasted_iota(jnp.int32, (1, LANES), 1)
        lane_p = lax.broadcasted_iota(jnp.int32, (pad, LANES), 1)
        kaug_ref[0, 0:pad, :] = jnp.where(lane_p == LANES - 1, 1.0, 0.0).astype(kaug_ref.dtype)
        kaug_ref[1, 0:pad, :] = jnp.where(lane_p == half - 1, 1.0, 0.0).astype(kaug_ref.dtype)
        vt_ref[0:pad // ku] = jnp.zeros((pad // ku,) + vt_ref.shape[1:], vt_ref.dtype)
        ones = jnp.ones((vt_ref.shape[2] - half, ku), vt_ref.dtype)

        def build(n, carry):
            rows = pl.ds(pl.multiple_of(n * blk, blk), blk)
            prows = pl.ds(pl.multiple_of(n * blk + pad, blk), blk)
            kn = k_ref[0, rows, :].astype(jnp.float32)
            kaug_ref[0, prows, :] = jnp.where(
                lo, kn, jnp.where(lane == half + n, 1.0, 0.0)).astype(kaug_ref.dtype)
            kaug_ref[1, prows, :] = jnp.where(
                lo, jnp.where(lane == n, 1.0, 0.0), kn).astype(kaug_ref.dtype)
            for t in range(blk // ku):
                vn = v_ref[0, pl.ds(pl.multiple_of(n * blk + t * ku, ku), ku), :]
                vn_t = vn.astype(jnp.float32).T.astype(vt_ref.dtype)
                unit = (n + group) * (blk // ku) + t
                for hh in range(2):
                    vt_ref[unit, hh, 0:half, :] = vn_t[hh * half:(hh + 1) * half]
                    vt_ref[unit, hh, half:, :] = ones
            mean = jnp.sum(kn, axis=0, keepdims=True) * (1.0 / blk)
            kmrows_ref[0, pl.ds(n, 1), :] = jnp.where(lane1 < half, mean, 0.0)
            kmrows_ref[1, pl.ds(n, 1), :] = jnp.where(lane1 < half, 0.0, mean)
            return carry

        lax.fori_loop(0, nblk, build, 0)

    dt = q_ref.dtype
    q_t = q_ref[0].astype(jnp.float32).T
    q_tb = q_t.astype(dt)
    q_aug = []
    for hh in range(2):
        gate_t = jnp.dot(kmrows_ref[hh].astype(dt), q_tb,
                         preferred_element_type=jnp.float32)
        route_t = _moba_route_mask(gate_t, i)
        q_h = q_t[hh * half:(hh + 1) * half]
        q_aug.append(jnp.concatenate([q_h, route_t] if hh == 0 else [route_t, q_h], axis=0).astype(dt))

    upb = blk // ku

    def fold_span(unit0, n_units, state, bias_ref=None):
        rows = pl.ds(pl.multiple_of(unit0 * ku, ku), n_units * ku)
        s = [jnp.dot(kaug_ref[hh, rows, :], q_aug[hh], preferred_element_type=jnp.float32)
             for hh in range(2)]
        state = list(state)
        for t in range(n_units):
            for hh in range(2):
                m, acc = state[hh]
                s_t = s[hh][t * ku:(t + 1) * ku]
                if bias_ref is not None:
                    s_t = s_t + bias_ref[hh, t * ku:(t + 1) * ku, :]
                m_new = jnp.maximum(m, jnp.max(s_t, axis=0, keepdims=True))
                p = jnp.exp2(s_t - m_new).astype(dt)
                acc = jnp.exp2(m - m_new) * acc + jnp.dot(vt_ref[unit0 + t, hh], p,
                                                          preferred_element_type=jnp.float32)
                state[hh] = (m_new, acc)
        return state

    start = (jnp.full((1, blk), NEG, jnp.float32), jnp.zeros((vt_ref.shape[2], blk), jnp.float32))
    state = fold_span((i + group - 1) * upb, 2 * upb, [start, start], tnear_ref)

    def far(g, carry):
        st = fold_span((i - 1 - group * g) * upb, group * upb, [carry[0:2], carry[2:4]])
        return tuple(st[0]) + tuple(st[1])

    out = lax.fori_loop(0, (i + group - 2) // group, far, tuple(state[0]) + tuple(state[1]))
    o_t = jnp.concatenate([out[1][:half] / out[1][half:half + 1],
                           out[3][:half] / out[3][half:half + 1]], axis=0)
    o_ref[0] = o_t.T.astype(o_ref.dtype)


def _moba_in_weights(w_in, dt):
    nq = N_HEADS * HEAD_DIM
    return jnp.concatenate([w_in[:, :nq] * (LOG2E * HEAD_DIM ** -0.5), w_in[:, nq:]], axis=1).astype(dt)


def _moba_attention(qkv, rel_bias):
    b, s, _ = qkv.shape
    blk = MOBA_BLOCK
    group = MOBA_GROUP
    nblk = s // blk
    hp = N_HEADS // 2
    ku = MOBA_KEY_UNIT
    assert s % blk == 0 and nblk < LANES // 2 and 2 * HEAD_DIM == LANES and blk % ku == 0
    kk = np.arange(blk)[:, None]
    qq = np.arange(blk)[None, :]
    shift = rel_bias[REL_BUCKETS - 1]
    t_own = _bias_table(rel_bias, qq - kk, qq >= kk, shift)
    t_adj = _bias_table(rel_bias, blk + qq - kk, np.ones((blk, blk), bool), shift)
    t_near = jnp.concatenate([t_adj, t_own], axis=1) * LOG2E
    return pl.pallas_call(
        functools.partial(_moba_kernel, blk=blk, nblk=nblk, group=group, ku=ku),
        out_shape=jax.ShapeDtypeStruct((b, s, N_HEADS * HEAD_DIM), qkv.dtype),
        grid=(b, hp, nblk),
        in_specs=[pl.BlockSpec((1, blk, LANES), lambda bb, p, i: (bb, i, p)),
                  pl.BlockSpec((1, s, LANES), lambda bb, p, i: (bb, 0, hp + p)),
                  pl.BlockSpec((1, s, LANES), lambda bb, p, i: (bb, 0, 2 * hp + p)),
                  pl.BlockSpec((2, 2 * blk, blk), lambda bb, p, i: (p, 0, 0))],
        out_specs=pl.BlockSpec((1, blk, LANES), lambda bb, p, i: (bb, i, p)),
        scratch_shapes=[pltpu.VMEM((2, s + group * blk, LANES), qkv.dtype),
                        pltpu.VMEM(((s + group * blk) // ku, 2, HEAD_DIM + MOBA_ONES_ROWS, ku), qkv.dtype),
                        pltpu.VMEM((2, LANES // 2, LANES), jnp.float32)],
        compiler_params=_params(("parallel", "parallel", "arbitrary")),
        name="moba_attention",
    )(qkv, qkv, qkv, t_near)


def _swa_kernel(q_ref, kp_ref, kc_ref, vp_ref, vc_ref, tab_ref, sink_ref, o_ref, *, blk, group):
    n = pl.program_id(1)
    half = LANES // 2
    dt = q_ref.dtype
    q_t = q_ref[0].astype(jnp.float32).T.astype(dt)
    kband = jnp.concatenate([kp_ref[0], kc_ref[0]], axis=0)
    vband = jnp.concatenate([vp_ref[0], vc_ref[0]], axis=0)
    v_t = vband.astype(jnp.float32).T.astype(dt)
    ones = jnp.ones((SWA_ONES_ROWS, 2 * blk), dt)
    zeros = jnp.zeros((half, group * blk), dt)
    tab_n = jnp.minimum(n, 1)

    outs = []
    for kv in range(SWA_KV_HEADS):
        heads = range(kv * group, (kv + 1) * group)
        qg = jnp.concatenate([q_t[h * half:(h + 1) * half] for h in heads], axis=1)
        qg = jnp.concatenate([qg, zeros] if kv == 0 else [zeros, qg], axis=0)
        s = jnp.dot(kband, qg, preferred_element_type=jnp.float32) + tab_ref[tab_n, kv]
        sink = sink_ref[kv]
        m = jnp.maximum(jnp.max(s, axis=0, keepdims=True), sink)
        p = jnp.exp2(s - m).astype(dt)
        v1 = jnp.concatenate([v_t[kv * half:(kv + 1) * half], ones], axis=0)
        acc = jnp.dot(v1, p, preferred_element_type=jnp.float32)
        o = acc[0:half] / (acc[half:half + 1] + jnp.exp2(sink - m))
        outs.extend(o[:, g * blk:(g + 1) * blk] for g in range(group))
    o_ref[0] = jnp.concatenate(outs, axis=0).T.astype(o_ref.dtype)


def _swa_in_weights(w_in, dt):
    nq = N_HEADS * HEAD_DIM
    return jnp.concatenate([w_in[:, :nq] * (LOG2E * HEAD_DIM ** -0.5), w_in[:, nq:]], axis=1).astype(dt)


def _swa_attention(proj, sinks, rel_bias):
    b, s, _ = proj.shape
    blk = SWA_BLOCK
    nb = s // blk
    group = N_HEADS // SWA_KV_HEADS
    nq = N_HEADS * HEAD_DIM
    assert s % blk == 0 and SWA_KV_HEADS * HEAD_DIM == LANES and SWA_WINDOW <= blk
    kcol = nq // LANES
    dist = blk + np.arange(blk)[None, :] - np.arange(2 * blk)[:, None]
    window = (dist >= 0) & (dist < SWA_WINDOW)
    first = window & (np.arange(2 * blk)[:, None] >= blk)
    zero = jnp.zeros_like(sinks)

    def wide(valid):
        tab = _bias_table(rel_bias, dist, valid, zero).reshape(SWA_KV_HEADS, group, 2 * blk, blk)
        return jnp.transpose(tab, (0, 2, 1, 3)).reshape(SWA_KV_HEADS, 2 * blk, group * blk) * LOG2E

    tabs = jnp.stack([wide(first), wide(window)])
    sink_w = jnp.broadcast_to(sinks.astype(jnp.float32).reshape(SWA_KV_HEADS, 1, group, 1) * LOG2E,
                              (SWA_KV_HEADS, 1, group, blk)).reshape(SWA_KV_HEADS, 1, group * blk)
    prev = lambda bb, n: (bb, jnp.maximum(n - 1, 0), kcol)
    cur = lambda bb, n: (bb, n, kcol)
    prev_v = lambda bb, n: (bb, jnp.maximum(n - 1, 0), kcol + 1)
    cur_v = lambda bb, n: (bb, n, kcol + 1)
    return pl.pallas_call(
        functools.partial(_swa_kernel, blk=blk, group=group),
        out_shape=jax.ShapeDtypeStruct((b, s, nq), proj.dtype),
        grid=(b, nb),
        in_specs=[pl.BlockSpec((1, blk, nq), lambda bb, n: (bb, n, 0)),
                  pl.BlockSpec((1, blk, LANES), prev),
                  pl.BlockSpec((1, blk, LANES), cur),
                  pl.BlockSpec((1, blk, LANES), prev_v),
                  pl.BlockSpec((1, blk, LANES), cur_v),
                  pl.BlockSpec(tabs.shape, lambda bb, n: (0, 0, 0, 0)),
                  pl.BlockSpec(sink_w.shape, lambda bb, n: (0, 0, 0))],
        out_specs=pl.BlockSpec((1, blk, nq), lambda bb, n: (bb, n, 0)),
        compiler_params=_params(("parallel", "arbitrary")),
        name="swa_attention",
    )(proj, proj, proj, proj, proj, tabs, sink_w)


def _dsa_proj_kernel(x_ref, g_ref, w_ref, wwt_ref, kvn_ref, wuk_ref,
                     ql_ref, c_ref, ct_ref, qi_ref, ki_ref, wt_ref, *, idx_scale):
    dt = w_ref.dtype
    nq = N_HEADS * HEAD_DIM
    r = DSA_KV_RANK
    ni = DSA_IDX_HEADS * DSA_IDX_DIM
    xn = _rmsnorm(x_ref[0], g_ref[...]).astype(dt)
    y = jnp.dot(xn, w_ref[...], preferred_element_type=jnp.float32)
    scale = LOG2E * HEAD_DIM ** -0.5
    for p in range(N_HEADS // 2):
        qp = y[:, p * LANES:(p + 1) * LANES].astype(dt)
        ql = jnp.dot(qp, wuk_ref[p], preferred_element_type=jnp.float32) * scale
        ql_ref[0, 2 * p] = ql[:, :r].astype(ql_ref.dtype)
        ql_ref[0, 2 * p + 1] = ql[:, r:].astype(ql_ref.dtype)
    c = _rmsnorm(y[:, nq:nq + r], kvn_ref[...])
    c_ref[0] = c.astype(c_ref.dtype)
    kt = ct_ref.shape[3]
    for t in range(ct_ref.shape[1]):
        ct_ref[0, t, 0:r, :] = c[t * kt:(t + 1) * kt].T.astype(ct_ref.dtype)
        ct_ref[0, t, r:, :] = jnp.ones((ct_ref.shape[2] - r, kt), ct_ref.dtype)
    qi_ref[0] = y[:, nq + r:nq + r + ni].astype(qi_ref.dtype)
    ki_ref[0] = y[:, nq + r + ni:nq + r + ni + LANES].astype(ki_ref.dtype)
    wt = lax.dot_general(wwt_ref[...], xn, _NT, preferred_element_type=jnp.float32)
    wt_ref[0] = wt * idx_scale


def _dsa_proj(x, g, w_in, kv_norm, w_uk, dt, *, tm=512):
    b, s, d = x.shape
    nq = N_HEADS * HEAD_DIM
    r = DSA_KV_RANK
    ni = DSA_IDX_HEADS * DSA_IDX_DIM
    di = DSA_IDX_DIM
    assert 2 * di == LANES and 2 * HEAD_DIM == LANES and s % tm == 0
    k_idx = w_in[:, nq + r + ni:nq + r + ni + di]
    w_main = jnp.concatenate([w_in[:, :nq + r + ni], k_idx, k_idx], axis=1).astype(dt)
    wwt = w_in[:, nq + r + ni + di:].T.astype(dt)
    uk = jnp.transpose(w_uk, (1, 2, 0)).reshape(N_HEADS // 2, 2, HEAD_DIM, r)
    z = jnp.zeros_like(uk[:, 0])
    wuk_bd = jnp.concatenate([jnp.concatenate([uk[:, 0], z], axis=2),
                              jnp.concatenate([z, uk[:, 1]], axis=2)], axis=1).astype(dt)
    nw = w_main.shape[1]
    idx_scale = DSA_IDX_HEADS ** -0.5 * DSA_IDX_DIM ** -0.5
    kt = DSA_TILE
    rp = r + DSA_ONES_ROWS
    assert tm % kt == 0
    return pl.pallas_call(
        functools.partial(_dsa_proj_kernel, idx_scale=idx_scale),
        out_shape=(jax.ShapeDtypeStruct((b, N_HEADS, s, r), dt),
                   jax.ShapeDtypeStruct((b, s, r), dt),
                   jax.ShapeDtypeStruct((b, s // kt, rp, kt), dt),
                   jax.ShapeDtypeStruct((b, s, ni), dt),
                   jax.ShapeDtypeStruct((b, s, LANES), dt),
                   jax.ShapeDtypeStruct((b, DSA_IDX_HEADS, s), jnp.float32)),
        grid=(b, s // tm),
        in_specs=[pl.BlockSpec((1, tm, d), lambda bb, i: (bb, i, 0)),
                  pl.BlockSpec((1, d), lambda bb, i: (0, 0)),
                  pl.BlockSpec((d, nw), lambda bb, i: (0, 0)),
                  pl.BlockSpec((DSA_IDX_HEADS, d), lambda bb, i: (0, 0)),
                  pl.BlockSpec((1, r), lambda bb, i: (0, 0)),
                  pl.BlockSpec((N_HEADS // 2, LANES, 2 * r), lambda bb, i: (0, 0, 0))],
        out_specs=(pl.BlockSpec((1, N_HEADS, tm, r), lambda bb, i: (bb, 0, i, 0)),
                   pl.BlockSpec((1, tm, r), lambda bb, i: (bb, i, 0)),
                   pl.BlockSpec((1, tm // kt, rp, kt), lambda bb, i: (bb, i, 0, 0)),
                   pl.BlockSpec((1, tm, ni), lambda bb, i: (bb, i, 0)),
                   pl.BlockSpec((1, tm, LANES), lambda bb, i: (bb, i, 0)),
                   pl.BlockSpec((1, DSA_IDX_HEADS, tm), lambda bb, i: (bb, 0, i))),
        compiler_params=_params(("parallel", "parallel")),
        name="dsa_proj",
    )(x, g.reshape(1, d), w_main, wwt, kv_norm.reshape(1, r), wuk_bd)


def _dsa_kernel(qi_ref, wt_ref, ki_ref, c_ref, ct_ref, ql_ref, tdiag_ref, tadj_ref, wuvt_ref, o_ref,
                u_ref, acc_ref, m_ref, *, tile, topk, s_len, hg):
    i = pl.program_id(1)
    half = LANES // 2
    nih = DSA_IDX_HEADS

    def tile_rows(j):
        return pl.ds(pl.multiple_of(j * tile, tile), tile)

    lane1 = lax.broadcasted_iota(jnp.int32, (1, LANES), 1)
    keep = (jnp.where(lane1 < half, 1.0, 0.0).astype(qi_ref.dtype),
            jnp.where(lane1 < half, 0.0, 1.0).astype(qi_ref.dtype))
    parts = []
    for h in range(nih):
        parts.append(qi_ref[0, :, (h // 2) * LANES:(h // 2 + 1) * LANES] * keep[h % 2])
    qs = jnp.concatenate(parts, axis=0)
    w_t = wt_ref[0]
    krow = lax.broadcasted_iota(jnp.int32, (tile, tile), 0)
    qcol = lax.broadcasted_iota(jnp.int32, (tile, tile), 1)

    def index_tile(j, carry):
        rows = tile_rows(j)
        rel = lax.dot_general(ki_ref[0, rows, :], qs, _NT, preferred_element_type=jnp.float32)
        sc = jnp.zeros((tile, tile), jnp.float32)
        for h in range(nih):
            sc = sc + jnp.maximum(rel[:, h * tile:(h + 1) * tile], 0.0) * w_t[h:h + 1, :]
        sc = jnp.where(sc == 0.0, 0.0, sc)
        bits = lax.bitcast_convert_type(sc, jnp.int32)
        u = jnp.where(bits < 0, bits ^ jnp.int32(0x7FFFFFFF), bits)
        causal = (j * tile + krow) <= (i * tile + qcol)
        u_ref[rows, :] = jnp.where(causal, u, jnp.int32(INT_MIN))
        return carry

    lax.fori_loop(0, i + 1, index_tile, 0)

    cu = DSA_COUNT_UNROLL
    for k in range(1, cu):
        @pl.when(i + k < s_len // tile)
        def _(k=k):
            u_ref[tile_rows(i + k), :] = jnp.full((tile, tile), INT_MIN, jnp.int32)

    def count(pred):
        def body(jg, acc8):
            for k in range(cu):
                j = jg * cu + k
                ones = jnp.where(pred(j, u_ref[tile_rows(j), :]), 1.0, 0.0)
                acc8 = acc8 + jnp.sum(ones.reshape(tile // 8, 8, tile), axis=0)
            return acc8
        acc8 = lax.fori_loop(0, (i + cu) // cu, body, jnp.zeros((8, tile), jnp.float32))
        return jnp.sum(acc8, axis=0, keepdims=True)

    few = (i * tile + lax.broadcasted_iota(jnp.int32, (1, tile), 1)) < topk

    def unsettled(carry):
        bi, _, cnt_ans = carry
        pending = jnp.logical_not(few) & (cnt_ans != float(topk))
        return (bi < 32) & (jnp.max(jnp.where(pending, 1.0, 0.0)) > 0.0)

    def bit_steps(carry):
        bi, ans, cnt_ans = carry
        for k in range(DSA_BITS_PER_CHECK):
            cand = ans | lax.shift_left(jnp.int32(1), 31 - (bi + k))
            thr_c = cand ^ jnp.int32(INT_MIN)
            cnt = count(lambda j, u, thr_c=thr_c: u >= thr_c)
            ok = cnt >= float(topk)
            ans, cnt_ans = jnp.where(ok, cand, ans), jnp.where(ok, cnt, cnt_ans)
        return bi + DSA_BITS_PER_CHECK, ans, cnt_ans

    _, ans, cnt_thr = lax.while_loop(unsettled, bit_steps,
                                     (jnp.int32(0), jnp.zeros((1, tile), jnp.int32),
                                      jnp.zeros((1, tile), jnp.float32)))
    thr = ans ^ jnp.int32(INT_MIN)

    nbits = int(s_len).bit_length()
    cut_all = jnp.full((1, tile), 2 ** nbits - 1, jnp.int32)

    def tie_search():
        need = float(topk) - count(lambda j, u: u > thr)

        def cut_step(bi, cut):
            cand = cut | lax.shift_left(jnp.int32(1), nbits - 1 - bi)
            cnt = count(lambda j, u: (u == thr) & ((j * tile + krow) < cand))
            return jnp.where(cnt <= need, cand, cut)

        return lax.fori_loop(0, nbits, cut_step, jnp.zeros((1, tile), jnp.int32))

    cut = lax.cond(jnp.max(cnt_thr) > float(topk), tie_search, lambda: cut_all)

    m_ref[...] = jnp.full(m_ref.shape, NEG, jnp.float32)
    acc_ref[...] = jnp.zeros(acc_ref.shape, jnp.float32)

    def attend_tile(j, table_ref, pen):
        rows = tile_rows(j)
        u = u_ref[rows, :]
        tied = (u == thr) & ((j * tile + krow) < cut) & (u != jnp.int32(INT_MIN))
        maskadd = jnp.where((u > thr) | tied, 0.0, NEG) + pen
        c_t = c_ref[0, rows, :]
        ct_t = ct_ref[0, j]

        for g in range(N_HEADS // hg):
            qlg = ql_ref[0, g * hg:(g + 1) * hg].reshape(hg * tile, ql_ref.shape[-1])
            logits = lax.dot_general(c_t, qlg, _NT, preferred_element_type=jnp.float32)
            for hl in range(hg):
                cols = slice(hl * tile, (hl + 1) * tile)
                s = logits[:, cols] + maskadd
                if table_ref is not None:
                    s = s + table_ref[g, :, cols]
                m_old = m_ref[g, :, cols]
                m_new = jnp.maximum(m_old, jnp.max(s, axis=0, keepdims=True))
                p = jnp.exp2(s - m_new).astype(ct_t.dtype)
                pv = jnp.dot(ct_t, p, preferred_element_type=jnp.float32)
                acc_ref[g, :, cols] = jnp.exp2(m_old - m_new) * acc_ref[g, :, cols] + pv
                m_ref[g, :, cols] = m_new

    def far(j, carry):
        attend_tile(j, None, 0.0)
        return carry

    lax.fori_loop(0, i - 1, far, 0)
    attend_tile(jnp.maximum(i - 1, 0), tadj_ref, jnp.where(i >= 1, 0.0, NEG))
    attend_tile(i, tdiag_ref, 0.0)

    rank = wuvt_ref.shape[2]
    parts = []
    for h in range(N_HEADS):
        g, cols = h // hg, slice((h % hg) * tile, (h % hg + 1) * tile)
        ol = (acc_ref[g, 0:rank, cols] / acc_ref[g, rank:rank + 1, cols]).astype(wuvt_ref.dtype)
        parts.append(jnp.dot(wuvt_ref[h], ol, preferred_element_type=jnp.float32))
    o_ref[0] = jnp.concatenate(parts, axis=0).T.astype(o_ref.dtype)


def _dsa_attention(ql, c, ct, qi, ki2, wt, w_uv, rel_bias):
    b, s, r = c.shape
    tile = DSA_TILE
    nt = s // tile
    topk = min(DSA_TOPK_MAX, s // 4)
    assert s % (tile * DSA_COUNT_UNROLL) == 0 and ct.shape[3] == tile
    rp = ct.shape[2]
    wuvt = jnp.transpose(w_uv, (1, 2, 0)).astype(c.dtype)
    kk = np.arange(tile)[:, None]
    qq = np.arange(tile)[None, :]
    shift = rel_bias[REL_BUCKETS - 1]
    hg = DSA_HEAD_GROUP
    ng = N_HEADS // hg

    def wide(tab):
        return jnp.transpose(tab.reshape(ng, hg, tile, tile), (0, 2, 1, 3)).reshape(ng, tile, hg * tile)

    t_diag = wide(_bias_table(rel_bias, qq - kk, qq >= kk, shift)) * LOG2E
    t_adj = wide(_bias_table(rel_bias, tile + qq - kk, np.ones((tile, tile), bool), shift)) * LOG2E
    ni = qi.shape[2]
    once = pl.Buffered(1)
    return pl.pallas_call(
        functools.partial(_dsa_kernel, tile=tile, topk=topk, s_len=s, hg=hg),
        out_shape=jax.ShapeDtypeStruct((b, s, N_HEADS * HEAD_DIM), c.dtype),
        grid=(b, nt),
        in_specs=[pl.BlockSpec((1, tile, ni), lambda bb, i: (bb, i, 0)),
                  pl.BlockSpec((1, DSA_IDX_HEADS, tile), lambda bb, i: (bb, 0, i)),
                  pl.BlockSpec((1, s, LANES), lambda bb, i: (bb, 0, 0), pipeline_mode=once),
                  pl.BlockSpec((1, s, r), lambda bb, i: (bb, 0, 0), pipeline_mode=once),
                  pl.BlockSpec((1, nt, rp, tile), lambda bb, i: (bb, 0, 0, 0), pipeline_mode=once),
                  pl.BlockSpec((1, N_HEADS, tile, r), lambda bb, i: (bb, 0, i, 0)),
                  pl.BlockSpec((ng, tile, hg * tile), lambda bb, i: (0, 0, 0), pipeline_mode=once),
                  pl.BlockSpec((ng, tile, hg * tile), lambda bb, i: (0, 0, 0), pipeline_mode=once),
                  pl.BlockSpec((N_HEADS, HEAD_DIM, r), lambda bb, i: (0, 0, 0), pipeline_mode=once)],
        out_specs=pl.BlockSpec((1, tile, N_HEADS * HEAD_DIM), lambda bb, i: (bb, i, 0)),
        scratch_shapes=[pltpu.VMEM((s, tile), jnp.int32),
                        pltpu.VMEM((ng, rp, hg * tile), jnp.float32),
                        pltpu.VMEM((ng, 1, hg * tile), jnp.float32)],
        compiler_params=_params(("parallel", "arbitrary")),
        name="dsa_attention",
    )(qi, wt, ki2, c, ct, ql, t_diag, t_adj, wuvt)


def _dsa_layer(h, g, w_in, kv_norm, w_uk, w_uv, w_out, rel_bias, mxu_dtype):
    b, s, d = h.shape
    ql, c, ct, qi, ki2, wt = _dsa_proj(h, g, w_in, kv_norm, w_uk, mxu_dtype)
    o = _dsa_attention(ql, c, ct, qi, ki2, wt, w_uv, rel_bias)
    return _proj_residual(o.reshape(b * s, -1), w_out.astype(mxu_dtype),
                          h.reshape(b * s, d)).reshape(b, s, d)


def kernel(x, rel_bias, norm_mix, norm_mlp, mlp_up, mlp_down, a_w_in, a_w_out, b_w_in, b_sinks, b_w_out,
           c_w_in, c_kv_norm, c_w_uk, c_w_uv, c_w_out, final_norm):
    b, s, d = x.shape
    t = b * s
    dt = MXU_DTYPE
    depth = norm_mix.shape[0]
    h = x.reshape(t, d)
    for i in range(depth):
        kind, j = i % 3, i // 3
        if kind == 0:
            qkv = _norm_proj(h, norm_mix[i], _moba_in_weights(a_w_in[j], dt)).reshape(b, s, -1)
            o = _moba_attention(qkv, rel_bias)
            h = _proj_residual(o.reshape(t, -1), a_w_out[j].astype(dt), h)
        elif kind == 1:
            proj = _norm_proj(h, norm_mix[i], _swa_in_weights(b_w_in[j], dt)).reshape(b, s, -1)
            o = _swa_attention(proj, b_sinks[j], rel_bias)
            h = _proj_residual(o.reshape(t, -1), b_w_out[j].astype(dt), h)
        else:
            h = _dsa_layer(h.reshape(b, s, d), norm_mix[i], c_w_in[j], c_kv_norm[j], c_w_uk[j],
                           c_w_uv[j], c_w_out[j], rel_bias, dt).reshape(t, d)
        h = _mlp(h, norm_mlp[i], mlp_up[i].astype(dt), mlp_down[i].astype(dt), final_norm,
                 final_norm=(i == depth - 1))
    return h.reshape(b, s, d)
```

```python
import functools
import math

import numpy as np
import jax
import jax.numpy as jnp
from jax import lax
from jax.experimental import pallas as pl
from jax.experimental.pallas import tpu as pltpu

N_HEADS = 16
HEAD_DIM = 64
NORM_EPS = 1e-6
REL_BUCKETS = 32
REL_MAX_DIST = 128
MOBA_BLOCK = 256
MOBA_TOPK = 3
MOBA_GROUP = 4
MOBA_KEY_UNIT = 128
MOBA_ONES_ROWS = 16
LOG2E = math.log2(math.e)
SWA_WINDOW = 128
SWA_BLOCK = 128
SWA_KV_HEADS = 2
SWA_ONES_ROWS = 16
DSA_KV_RANK = 256
DSA_IDX_HEADS = 8
DSA_IDX_DIM = 64
DSA_TOPK_MAX = 256
DSA_TILE = 256
DSA_HEAD_GROUP = 8
DSA_ONES_ROWS = 16
DSA_COUNT_UNROLL = 4
DSA_BITS_PER_CHECK = 4

LANES = 128
MXU_DTYPE = jnp.bfloat16
NEG = -1e30
INT_MIN = -2 ** 31
I16_SPAN = 2 ** 16
VMEM_LIMIT_BYTES = 56 * 1024 * 1024

_NT = (((1,), (1,)), ((), ()))


def _params(semantics):
    return pltpu.CompilerParams(dimension_semantics=semantics, vmem_limit_bytes=VMEM_LIMIT_BYTES)


def _rmsnorm(x, g):
    var = jnp.mean(x * x, axis=-1, keepdims=True)
    return x * lax.rsqrt(var + NORM_EPS) * g


def _norm_proj_kernel(x_ref, g_ref, w_ref, o_ref):
    xn = _rmsnorm(x_ref[...], g_ref[...]).astype(w_ref.dtype)
    o_ref[...] = jnp.dot(xn, w_ref[...], preferred_element_type=jnp.float32).astype(o_ref.dtype)


def _norm_proj(x, g, w, *, tm=512):
    t, d = x.shape
    n = w.shape[1]
    assert t % tm == 0
    return pl.pallas_call(
        _norm_proj_kernel,
        out_shape=jax.ShapeDtypeStruct((t, n), w.dtype),
        grid=(t // tm,),
        in_specs=[pl.BlockSpec((tm, d), lambda i: (i, 0)),
                  pl.BlockSpec((1, d), lambda i: (0, 0)),
                  pl.BlockSpec((d, n), lambda i: (0, 0))],
        out_specs=pl.BlockSpec((tm, n), lambda i: (i, 0)),
        compiler_params=_params(("parallel",)),
        name="norm_proj",
    )(x, g.reshape(1, d), w)


def _proj_residual_kernel(a_ref, w_ref, r_ref, o_ref):
    o_ref[...] = r_ref[...] + jnp.dot(a_ref[...], w_ref[...], preferred_element_type=jnp.float32)


def _proj_residual(a, w, res, *, tm=512):
    t, k = a.shape
    n = w.shape[1]
    assert t % tm == 0
    return pl.pallas_call(
        _proj_residual_kernel,
        out_shape=jax.ShapeDtypeStruct((t, n), jnp.float32),
        grid=(t // tm,),
        in_specs=[pl.BlockSpec((tm, k), lambda i: (i, 0)),
                  pl.BlockSpec((k, n), lambda i: (0, 0)),
                  pl.BlockSpec((tm, n), lambda i: (i, 0))],
        out_specs=pl.BlockSpec((tm, n), lambda i: (i, 0)),
        compiler_params=_params(("parallel",)),
        name="proj_residual",
    )(a, w, res)


def _mlp_kernel(x_ref, g_ref, wu_ref, wd_ref, gf_ref, o_ref, xn_ref, acc_ref, *, final_norm):
    f = pl.program_id(1)

    @pl.when(f == 0)
    def _():
        x = x_ref[...]
        xn_ref[...] = _rmsnorm(x, g_ref[...]).astype(xn_ref.dtype)
        acc_ref[...] = x

    u = jnp.dot(xn_ref[...], wu_ref[...], preferred_element_type=jnp.float32)
    a = jnp.square(jnp.maximum(u, 0.0)).astype(wd_ref.dtype)
    acc_ref[...] += jnp.dot(a, wd_ref[...], preferred_element_type=jnp.float32)

    @pl.when(f == pl.num_programs(1) - 1)
    def _():
        y = acc_ref[...]
        if final_norm:
            y = _rmsnorm(y, gf_ref[...])
        o_ref[...] = y


def _mlp(x, g, w_up, w_down, g_final, *, final_norm, tm=1024, tf=1024):
    t, d = x.shape
    ff = w_up.shape[1]
    assert t % tm == 0 and ff % tf == 0
    return pl.pallas_call(
        functools.partial(_mlp_kernel, final_norm=final_norm),
        out_shape=jax.ShapeDtypeStruct((t, d), jnp.float32),
        grid=(t // tm, ff // tf),
        in_specs=[pl.BlockSpec((tm, d), lambda i, f: (i, 0)),
                  pl.BlockSpec((1, d), lambda i, f: (0, 0)),
                  pl.BlockSpec((d, tf), lambda i, f: (0, f)),
                  pl.BlockSpec((tf, d), lambda i, f: (f, 0)),
                  pl.BlockSpec((1, d), lambda i, f: (0, 0))],
        out_specs=pl.BlockSpec((tm, d), lambda i, f: (i, 0)),
        scratch_shapes=[pltpu.VMEM((tm, d), w_up.dtype), pltpu.VMEM((tm, d), jnp.float32)],
        compiler_params=_params(("parallel", "arbitrary")),
        name="mlp",
    )(x, g.reshape(1, d), w_up, w_down, g_final.reshape(1, d))


def _rel_bucket_np(dist):
    n = np.maximum(dist, 0)
    max_exact = REL_BUCKETS // 2
    nf = np.maximum(n, 1).astype(np.float64)
    large = max_exact + (np.log(nf / max_exact) / math.log(REL_MAX_DIST / max_exact)
                         * (REL_BUCKETS - max_exact)).astype(np.int64)
    large = np.minimum(large, REL_BUCKETS - 1)
    return np.where(n < max_exact, n, large).astype(np.int32)


def _bias_table(rel_bias, dist, valid, shift):
    bucket = jnp.asarray(_rel_bucket_np(dist).reshape(1, -1))
    onehot = (bucket == jnp.arange(REL_BUCKETS, dtype=bucket.dtype).reshape(-1, 1)).astype(jnp.float32)
    tab = jnp.einsum("bh,bn->hn", rel_bias, onehot, precision=lax.Precision.HIGHEST)
    tab = tab.reshape((-1,) + dist.shape) - shift.reshape((-1,) + (1,) * dist.ndim)
    return jnp.where(jnp.asarray(valid)[None], tab, NEG).astype(jnp.float32)


def _moba_route_mask(gate_t, i):
    blkid = lax.broadcasted_iota(jnp.int32, gate_t.shape, 0)
    blk_f = blkid.astype(jnp.float32)
    valid = blkid < i
    g = jnp.where(valid, gate_t, -jnp.inf)
    sel = blkid == i
    for _ in range(MOBA_TOPK):
        mx = jnp.max(g, axis=0, keepdims=True)
        first = jnp.min(jnp.where(g == mx, blk_f, float(LANES)), axis=0, keepdims=True)
        pick = (blk_f == first) & valid
        sel = sel | pick
        g = jnp.where(pick, -jnp.inf, g)
    return jnp.where(sel, 0.0, NEG)


def _moba_kernel(q_ref, k_ref, v_ref, tnear_ref, o_ref, kaug_ref, vt_ref, kmrows_ref,
                 *, blk, nblk, group, ku):
    i = pl.program_id(2)
    half = LANES // 2
    pad = group * blk
    lane = lax.broadcasted_iota(jnp.int32, (blk, LANES), 1)
    lo = lane < half

    @pl.when(i == 0)
    def _():
        kmrows_ref[...] = jnp.zeros(kmrows_ref.shape, kmrows_ref.dtype)
        lane1 = lax.broadcasted_iota(jnp.int32, (1, LANES), 1)
        lane_p = lax.broadcasted_iota(jnp.int32, (pad, LANES), 1)
        kaug_ref[0, 0:pad, :] = jnp.where(lane_p == LANES - 1, 1.0, 0.0).astype(kaug_ref.dtype)
        kaug_ref[1, 0:pad, :] = jnp.where(lane_p == half - 1, 1.0, 0.0).astype(kaug_ref.dtype)
        vt_ref[0:pad // ku] = jnp.zeros((pad // ku,) + vt_ref.shape[1:], vt_ref.dtype)
        ones = jnp.ones((vt_ref.shape[2] - half, ku), vt_ref.dtype)

        def build(n, carry):
            rows = pl.ds(pl.multiple_of(n * blk, blk), blk)
            prows = pl.ds(pl.multiple_of(n * blk + pad, blk), blk)
            kn = k_ref[0, rows, :].astype(jnp.float32)
            kaug_ref[0, prows, :] = jnp.where(
                lo, kn, jnp.where(lane == half + n, 1.0, 0.0)).astype(kaug_ref.dtype)
            kaug_ref[1, prows, :] = jnp.where(
                lo, jnp.where(lane == n, 1.0, 0.0), kn).astype(kaug_ref.dtype)
            for t in range(blk // ku):
                vn = v_ref[0, pl.ds(pl.multiple_of(n * blk + t * ku, ku), ku), :]
                vn_t = vn.astype(jnp.float32).T.astype(vt_ref.dtype)
                unit = (n + group) * (blk // ku) + t
                for hh in range(2):
                    vt_ref[unit, hh, 0:half, :] = vn_t[hh * half:(hh + 1) * half]
                    vt_ref[unit, hh, half:, :] = ones
            mean = jnp.sum(kn, axis=0, keepdims=True) * (1.0 / blk)
            kmrows_ref[0, pl.ds(n, 1), :] = jnp.where(lane1 < half, mean, 0.0)
            kmrows_ref[1, pl.ds(n, 1), :] = jnp.where(lane1 < half, 0.0, mean)
            return carry

        lax.fori_loop(0, nblk, build, 0)

    dt = q_ref.dtype
    q_t = q_ref[0].astype(jnp.float32).T
    q_tb = q_t.astype(dt)
    q_aug = []
    for hh in range(2):
        gate_t = jnp.dot(kmrows_ref[hh].astype(dt), q_tb,
                         preferred_element_type=jnp.float32)
        route_t = _moba_route_mask(gate_t, i)
        q_h = q_t[hh * half:(hh + 1) * half]
        q_aug.append(jnp.concatenate([q_h, route_t] if hh == 0 else [route_t, q_h], axis=0).astype(dt))

    upb = blk // ku

    def fold_span(unit0, n_units, state, bias_ref=None):
        rows = pl.ds(pl.multiple_of(unit0 * ku, ku), n_units * ku)
        s = [jnp.dot(kaug_ref[hh, rows, :], q_aug[hh], preferred_element_type=jnp.float32)
             for hh in range(2)]
        state = list(state)
        for t in range(n_units):
            for hh in range(2):
                m, acc = state[hh]
                s_t = s[hh][t * ku:(t + 1) * ku]
                if bias_ref is not None:
                    s_t = s_t + bias_ref[hh, t * ku:(t + 1) * ku, :]
                m_new = jnp.maximum(m, jnp.max(s_t, axis=0, keepdims=True))
                p = jnp.exp2(s_t - m_new).astype(dt)
                acc = jnp.exp2(m - m_new) * acc + jnp.dot(vt_ref[unit0 + t, hh], p,
                                                          preferred_element_type=jnp.float32)
                state[hh] = (m_new, acc)
        return state

    start = (jnp.full((1, blk), NEG, jnp.float32), jnp.zeros((vt_ref.shape[2], blk), jnp.float32))
    state = fold_span((i + group - 1) * upb, 2 * upb, [start, start], tnear_ref)

    def far(g, carry):
        st = fold_span((i - 1 - group * g) * upb, group * upb, [carry[0:2], carry[2:4]])
        return tuple(st[0]) + tuple(st[1])

    out = lax.fori_loop(0, (i + group - 2) // group, far, tuple(state[0]) + tuple(state[1]))
    o_t = jnp.concatenate([out[1][:half] / out[1][half:half + 1],
                           out[3][:half] / out[3][half:half + 1]], axis=0)
    o_ref[0] = o_t.T.astype(o_ref.dtype)


def _moba_in_weights(w_in, dt):
    nq = N_HEADS * HEAD_DIM
    return jnp.concatenate([w_in[:, :nq] * (LOG2E * HEAD_DIM ** -0.5), w_in[:, nq:]], axis=1).astype(dt)


def _moba_attention(qkv, rel_bias):
    b, s, _ = qkv.shape
    blk = MOBA_BLOCK
    group = MOBA_GROUP
    nblk = s // blk
    hp = N_HEADS // 2
    ku = MOBA_KEY_UNIT
    assert s % blk == 0 and nblk < LANES // 2 and 2 * HEAD_DIM == LANES and blk % ku == 0
    kk = np.arange(blk)[:, None]
    qq = np.arange(blk)[None, :]
    shift = rel_bias[REL_BUCKETS - 1]
    t_own = _bias_table(rel_bias, qq - kk, qq >= kk, shift)
    t_adj = _bias_table(rel_bias, blk + qq - kk, np.ones((blk, blk), bool), shift)
    t_near = jnp.concatenate([t_adj, t_own], axis=1) * LOG2E
    return pl.pallas_call(
        functools.partial(_moba_kernel, blk=blk, nblk=nblk, group=group, ku=ku),
        out_shape=jax.ShapeDtypeStruct((b, s, N_HEADS * HEAD_DIM), qkv.dtype),
        grid=(b, hp, nblk),
        in_specs=[pl.BlockSpec((1, blk, LANES), lambda bb, p, i: (bb, i, p)),
                  pl.BlockSpec((1, s, LANES), lambda bb, p, i: (bb, 0, hp + p)),
                  pl.BlockSpec((1, s, LANES), lambda bb, p, i: (bb, 0, 2 * hp + p)),
                  pl.BlockSpec((2, 2 * blk, blk), lambda bb, p, i: (p, 0, 0))],
        out_specs=pl.BlockSpec((1, blk, LANES), lambda bb, p, i: (bb, i, p)),
        scratch_shapes=[pltpu.VMEM((2, s + group * blk, LANES), qkv.dtype),
                        pltpu.VMEM(((s + group * blk) // ku, 2, HEAD_DIM + MOBA_ONES_ROWS, ku), qkv.dtype),
                        pltpu.VMEM((2, LANES // 2, LANES), jnp.float32)],
        compiler_params=_params(("parallel", "parallel", "arbitrary")),
        name="moba_attention",
    )(qkv, qkv, qkv, t_near)


def _swa_kernel(q_ref, kp_ref, kc_ref, vp_ref, vc_ref, tab_ref, sink_ref, o_ref, *, blk, group):
    n = pl.program_id(1)
    half = LANES // 2
    dt = q_ref.dtype
    q_t = q_ref[0].astype(jnp.float32).T.astype(dt)
    kband = jnp.concatenate([kp_ref[0], kc_ref[0]], axis=0)
    vband = jnp.concatenate([vp_ref[0], vc_ref[0]], axis=0)
    v_t = vband.astype(jnp.float32).T.astype(dt)
    ones = jnp.ones((SWA_ONES_ROWS, 2 * blk), dt)
    zeros = jnp.zeros((half, group * blk), dt)
    tab_n = jnp.minimum(n, 1)

    outs = []
    for kv in range(SWA_KV_HEADS):
        heads = range(kv * group, (kv + 1) * group)
        qg = jnp.concatenate([q_t[h * half:(h + 1) * half] for h in heads], axis=1)
        qg = jnp.concatenate([qg, zeros] if kv == 0 else [zeros, qg], axis=0)
        s = jnp.dot(kband, qg, preferred_element_type=jnp.float32) + tab_ref[tab_n, kv]
        sink = sink_ref[kv]
        m = jnp.maximum(jnp.max(s, axis=0, keepdims=True), sink)
        p = jnp.exp2(s - m).astype(dt)
        v1 = jnp.concatenate([v_t[kv * half:(kv + 1) * half], ones], axis=0)
        acc = jnp.dot(v1, p, preferred_element_type=jnp.float32)
        o = acc[0:half] / (acc[half:half + 1] + jnp.exp2(sink - m))
        outs.extend(o[:, g * blk:(g + 1) * blk] for g in range(group))
    o_ref[0] = jnp.concatenate(outs, axis=0).T.astype(o_ref.dtype)


def _swa_in_weights(w_in, dt):
    nq = N_HEADS * HEAD_DIM
    return jnp.concatenate([w_in[:, :nq] * (LOG2E * HEAD_DIM ** -0.5), w_in[:, nq:]], axis=1).astype(dt)


def _swa_attention(proj, sinks, rel_bias):
    b, s, _ = proj.shape
    blk = SWA_BLOCK
    nb = s // blk
    group = N_HEADS // SWA_KV_HEADS
    nq = N_HEADS * HEAD_DIM
    assert s % blk == 0 and SWA_KV_HEADS * HEAD_DIM == LANES and SWA_WINDOW <= blk
    kcol = nq // LANES
    dist = blk + np.arange(blk)[None, :] - np.arange(2 * blk)[:, None]
    window = (dist >= 0) & (dist < SWA_WINDOW)
    first = window & (np.arange(2 * blk)[:, None] >= blk)
    zero = jnp.zeros_like(sinks)

    def wide(valid):
        tab = _bias_table(rel_bias, dist, valid, zero).reshape(SWA_KV_HEADS, group, 2 * blk, blk)
        return jnp.transpose(tab, (0, 2, 1, 3)).reshape(SWA_KV_HEADS, 2 * blk, group * blk) * LOG2E

    tabs = jnp.stack([wide(first), wide(window)])
    sink_w = jnp.broadcast_to(sinks.astype(jnp.float32).reshape(SWA_KV_HEADS, 1, group, 1) * LOG2E,
                              (SWA_KV_HEADS, 1, group, blk)).reshape(SWA_KV_HEADS, 1, group * blk)
    prev = lambda bb, n: (bb, jnp.maximum(n - 1, 0), kcol)
    cur = lambda bb, n: (bb, n, kcol)
    prev_v = lambda bb, n: (bb, jnp.maximum(n - 1, 0), kcol + 1)
    cur_v = lambda bb, n: (bb, n, kcol + 1)
    return pl.pallas_call(
        functools.partial(_swa_kernel, blk=blk, group=group),
        out_shape=jax.ShapeDtypeStruct((b, s, nq), proj.dtype),
        grid=(b, nb),
        in_specs=[pl.BlockSpec((1, blk, nq), lambda bb, n: (bb, n, 0)),
                  pl.BlockSpec((1, blk, LANES), prev),
                  pl.BlockSpec((1, blk, LANES), cur),
                  pl.BlockSpec((1, blk, LANES), prev_v),
                  pl.BlockSpec((1, blk, LANES), cur_v),
                  pl.BlockSpec(tabs.shape, lambda bb, n: (0, 0, 0, 0)),
                  pl.BlockSpec(sink_w.shape, lambda bb, n: (0, 0, 0))],
        out_specs=pl.BlockSpec((1, blk, nq), lambda bb, n: (bb, n, 0)),
        compiler_params=_params(("parallel", "arbitrary")),
        name="swa_attention",
    )(proj, proj, proj, proj, proj, tabs, sink_w)


def _dsa_proj_kernel(x_ref, g_ref, w_ref, wwt_ref, kvn_ref, wuk_ref,
                     ql_ref, c_ref, ct_ref, qi_ref, ki_ref, wt_ref, *, idx_scale):
    dt = w_ref.dtype
    nq = N_HEADS * HEAD_DIM
    r = DSA_KV_RANK
    ni = DSA_IDX_HEADS * DSA_IDX_DIM
    xn = _rmsnorm(x_ref[0], g_ref[...]).astype(dt)
    y = jnp.dot(xn, w_ref[...], preferred_element_type=jnp.float32)
    scale = LOG2E * HEAD_DIM ** -0.5
    for p in range(N_HEADS // 2):
        qp = y[:, p * LANES:(p + 1) * LANES].astype(dt)
        ql = jnp.dot(qp, wuk_ref[p], preferred_element_type=jnp.float32) * scale
        ql_ref[0, 2 * p] = ql[:, :r].astype(ql_ref.dtype)
        ql_ref[0, 2 * p + 1] = ql[:, r:].astype(ql_ref.dtype)
    c = _rmsnorm(y[:, nq:nq + r], kvn_ref[...])
    c_ref[0] = c.astype(c_ref.dtype)
    kt = ct_ref.shape[3]
    for t in range(ct_ref.shape[1]):
        ct_ref[0, t, 0:r, :] = c[t * kt:(t + 1) * kt].T.astype(ct_ref.dtype)
        ct_ref[0, t, r:, :] = jnp.ones((ct_ref.shape[2] - r, kt), ct_ref.dtype)
    qi_ref[0] = y[:, nq + r:nq + r + ni].astype(qi_ref.dtype)
    ki_ref[0] = y[:, nq + r + ni:nq + r + ni + LANES].astype(ki_ref.dtype)
    wt = lax.dot_general(wwt_ref[...], xn, _NT, preferred_element_type=jnp.float32)
    wt_ref[0] = wt * idx_scale


def _dsa_proj(x, g, w_in, kv_norm, w_uk, dt, *, tm=512):
    b, s, d = x.shape
    nq = N_HEADS * HEAD_DIM
    r = DSA_KV_RANK
    ni = DSA_IDX_HEADS * DSA_IDX_DIM
    di = DSA_IDX_DIM
    assert 2 * di == LANES and 2 * HEAD_DIM == LANES and s % tm == 0
    k_idx = w_in[:, nq + r + ni:nq + r + ni + di]
    w_main = jnp.concatenate([w_in[:, :nq + r + ni], k_idx, k_idx], axis=1).astype(dt)
    wwt = w_in[:, nq + r + ni + di:].T.astype(dt)
    uk = jnp.transpose(w_uk, (1, 2, 0)).reshape(N_HEADS // 2, 2, HEAD_DIM, r)
    z = jnp.zeros_like(uk[:, 0])
    wuk_bd = jnp.concatenate([jnp.concatenate([uk[:, 0], z], axis=2),
                              jnp.concatenate([z, uk[:, 1]], axis=2)], axis=1).astype(dt)
    nw = w_main.shape[1]
    idx_scale = DSA_IDX_HEADS ** -0.5 * DSA_IDX_DIM ** -0.5
    kt = DSA_TILE
    rp = r + DSA_ONES_ROWS
    assert tm % kt == 0
    return pl.pallas_call(
        functools.partial(_dsa_proj_kernel, idx_scale=idx_scale),
        out_shape=(jax.ShapeDtypeStruct((b, N_HEADS, s, r), dt),
                   jax.ShapeDtypeStruct((b, s, r), dt),
                   jax.ShapeDtypeStruct((b, s // kt, rp, kt), dt),
                   jax.ShapeDtypeStruct((b, s, ni), dt),
                   jax.ShapeDtypeStruct((b, s, LANES), dt),
                   jax.ShapeDtypeStruct((b, DSA_IDX_HEADS, s), jnp.float32)),
        grid=(b, s // tm),
        in_specs=[pl.BlockSpec((1, tm, d), lambda bb, i: (bb, i, 0)),
                  pl.BlockSpec((1, d), lambda bb, i: (0, 0)),
                  pl.BlockSpec((d, nw), lambda bb, i: (0, 0)),
                  pl.BlockSpec((DSA_IDX_HEADS, d), lambda bb, i: (0, 0)),
                  pl.BlockSpec((1, r), lambda bb, i: (0, 0)),
                  pl.BlockSpec((N_HEADS // 2, LANES, 2 * r), lambda bb, i: (0, 0, 0))],
        out_specs=(pl.BlockSpec((1, N_HEADS, tm, r), lambda bb, i: (bb, 0, i, 0)),
                   pl.BlockSpec((1, tm, r), lambda bb, i: (bb, i, 0)),
                   pl.BlockSpec((1, tm // kt, rp, kt), lambda bb, i: (bb, i, 0, 0)),
                   pl.BlockSpec((1, tm, ni), lambda bb, i: (bb, i, 0)),
                   pl.BlockSpec((1, tm, LANES), lambda bb, i: (bb, i, 0)),
                   pl.BlockSpec((1, DSA_IDX_HEADS, tm), lambda bb, i: (bb, 0, i))),
        compiler_params=_params(("parallel", "parallel")),
        name="dsa_proj",
    )(x, g.reshape(1, d), w_main, wwt, kv_norm.reshape(1, r), wuk_bd)


def _dsa_kernel(qi_ref, wt_ref, ki_ref, c_ref, ct_ref, ql_ref, tdiag_ref, tadj_ref, wuvt_ref, o_ref,
                hi_ref, lo_ref, acc_ref, m_ref, qlt_ref, *, tile, topk, s_len, hg):
    i = pl.program_id(1)
    half = LANES // 2
    nih = DSA_IDX_HEADS

    def tile_rows(j):
        return pl.ds(pl.multiple_of(j * tile, tile), tile)

    dt = qi_ref.dtype
    zeros = jnp.zeros((half, tile), jnp.float32)
    parts = []
    for pair in range(nih // 2):
        pair_t = qi_ref[0, :, pair * LANES:(pair + 1) * LANES].astype(jnp.float32).T
        parts.append(jnp.concatenate([pair_t[:half], zeros], axis=0))
        parts.append(jnp.concatenate([zeros, pair_t[half:]], axis=0))
    qs_t = jnp.concatenate(parts, axis=1).astype(dt)
    w_t = wt_ref[0]
    krow = lax.broadcasted_iota(jnp.int32, (tile, tile), 0)
    qcol = lax.broadcasted_iota(jnp.int32, (tile, tile), 1)

    def index_tile(j, carry):
        rows = tile_rows(j)
        rel = jnp.dot(ki_ref[0, rows, :], qs_t, preferred_element_type=jnp.float32)
        sc = jnp.zeros((tile, tile), jnp.float32)
        for h in range(nih):
            sc = sc + jnp.maximum(rel[:, h * tile:(h + 1) * tile], 0.0) * w_t[h:h + 1, :]
        sc = jnp.where(sc == 0.0, 0.0, sc)
        bits = lax.bitcast_convert_type(sc, jnp.int32)
        u = jnp.where(bits < 0, bits ^ jnp.int32(0x7FFFFFFF), bits)
        causal = (j * tile + krow) <= (i * tile + qcol)
        u = jnp.where(causal, u, jnp.int32(INT_MIN))
        hi_ref[rows, :] = lax.shift_right_arithmetic(u, 16).astype(jnp.int16)
        lo_ref[rows, :] = ((u & jnp.int32(0xFFFF)) - I16_SPAN // 2).astype(jnp.int16)
        return carry

    lax.fori_loop(0, i + 1, index_tile, 0)

    cu = DSA_COUNT_UNROLL
    i16_min = jnp.int16(-I16_SPAN // 2)
    for k in range(1, cu):
        @pl.when(i + k < s_len // tile)
        def _(k=k):
            hi_ref[tile_rows(i + k), :] = jnp.full((tile, tile), i16_min, jnp.int16)
            lo_ref[tile_rows(i + k), :] = jnp.full((tile, tile), i16_min, jnp.int16)

    def count(pred):
        def body(jg, acc):
            for k in range(cu):
                j = jg * cu + k
                ones = jnp.where(pred(j, tile_rows(j)), jnp.int16(1), jnp.int16(0))
                for r in range(0, tile, 16):
                    acc = acc + ones[r:r + 16]
            return acc
        acc = lax.fori_loop(0, (i + cu) // cu, body, jnp.zeros((16, tile), jnp.int16))
        return jnp.sum(acc.astype(jnp.float32), axis=0, keepdims=True)

    def to_i16(x):
        return x.astype(jnp.int16)

    few = (i * tile + lax.broadcasted_iota(jnp.int32, (1, tile), 1)) < topk

    def search16(ref, target):
        def unsettled(carry):
            bi, _, cnt_ans = carry
            pending = jnp.logical_not(few) & (cnt_ans != target)
            return (bi < 16) & (jnp.max(jnp.where(pending, 1.0, 0.0)) > 0.0)

        def bit_steps(carry):
            bi, ans, cnt_ans = carry
            for k in range(DSA_BITS_PER_CHECK):
                cand = ans | lax.shift_left(jnp.int32(1), 15 - (bi + k))
                t16 = to_i16(cand - I16_SPAN // 2)
                cnt = count(lambda j, rows, t16=t16: ref[rows, :] >= t16)
                ok = cnt >= target
                ans, cnt_ans = jnp.where(ok, cand, ans), jnp.where(ok, cnt, cnt_ans)
            return bi + DSA_BITS_PER_CHECK, ans, cnt_ans

        _, ans, cnt = lax.while_loop(unsettled, bit_steps,
                                     (jnp.int32(0), jnp.zeros((1, tile), jnp.int32),
                                      jnp.zeros((1, tile), jnp.float32)))
        return ans - I16_SPAN // 2, cnt

    def count_above(ref, t):
        cnt = count(lambda j, rows: ref[rows, :] >= to_i16(jnp.minimum(t + 1, I16_SPAN // 2 - 1)))
        return jnp.where(t + 1 > I16_SPAN // 2 - 1, 0.0, cnt)

    topk_f = jnp.full((1, tile), float(topk), jnp.float32)
    thr_hi, cnt_hi = search16(hi_ref, topk_f)
    cnt_above = count_above(hi_ref, thr_hi)
    need_lo = topk_f - cnt_above
    h16 = to_i16(thr_hi)

    def keep_low(jg, carry):
        for k in range(cu):
            rows = tile_rows(jg * cu + k)
            lo_ref[rows, :] = jnp.where(hi_ref[rows, :] == h16, lo_ref[rows, :], i16_min)
        return carry

    lax.fori_loop(0, (i + cu) // cu, keep_low, 0)
    thr_lo, cnt_lo = search16(lo_ref, need_lo)
    cnt_lo = jnp.where(cnt_lo == 0.0, cnt_hi - cnt_above, cnt_lo)

    nbits = int(s_len).bit_length()
    cut_all = jnp.full((1, tile), 2 ** nbits - 1, jnp.int32)
    krow16 = lax.broadcasted_iota(jnp.int16, (tile, tile), 0)
    l16 = to_i16(thr_lo)

    def tie_search():
        need = need_lo - count_above(lo_ref, thr_lo)

        def cut_step(bi, cut):
            cand = cut | lax.shift_left(jnp.int32(1), nbits - 1 - bi)
            cnt = count(lambda j, rows: (lo_ref[rows, :] == l16) & (hi_ref[rows, :] == h16)
                        & (krow16 < to_i16(jnp.minimum(cand - j * tile, I16_SPAN // 2 - 1))))
            return jnp.where(cnt <= need, cand, cut)

        return lax.fori_loop(0, nbits, cut_step, jnp.zeros((1, tile), jnp.int32))

    tie = jnp.logical_not(few) & (cnt_lo > need_lo)
    cut = lax.cond(jnp.max(jnp.where(tie, 1.0, 0.0)) > 0.0, tie_search, lambda: cut_all)
    l16_sel = to_i16(jnp.where(thr_hi == -I16_SPAN // 2, I16_SPAN // 2 - 1, thr_lo))
    mask_0 = jnp.zeros((), c_ref.dtype)
    mask_neg = jnp.asarray(NEG, c_ref.dtype)

    m_ref[...] = jnp.full(m_ref.shape, NEG, jnp.float32)
    acc_ref[...] = jnp.zeros(acc_ref.shape, jnp.float32)
    for h in range(N_HEADS):
        qlt_ref[h // hg, :, (h % hg) * tile:(h % hg + 1) * tile] = (
            ql_ref[0, h].astype(jnp.float32).T.astype(qlt_ref.dtype))

    def attend_tile(j, table_ref, pen):
        rows = tile_rows(j)
        hi = hi_ref[rows, :]
        lo = lo_ref[rows, :]
        cut16 = to_i16(jnp.clip(cut - j * tile, -I16_SPAN // 2, I16_SPAN // 2 - 1))
        tie_m = jnp.where(krow16 < cut16, mask_0, mask_neg)
        low_m = jnp.where(lo > l16_sel, mask_0, jnp.where(lo == l16_sel, tie_m, mask_neg))
        mask16 = jnp.where(hi > h16, mask_0, jnp.where(hi == h16, low_m, mask_neg))
        maskadd = mask16.astype(jnp.float32) + pen
        c_t = c_ref[0, rows, :]
        ct_t = ct_ref[0, j]

        for g in range(N_HEADS // hg):
            logits = jnp.dot(c_t, qlt_ref[g], preferred_element_type=jnp.float32)
            for hl in range(hg):
                cols = slice(hl * tile, (hl + 1) * tile)
                s = logits[:, cols] + maskadd
                if table_ref is not None:
                    s = s + table_ref[g, :, cols]
                m_old = m_ref[g, :, cols]
                m_new = jnp.maximum(m_old, jnp.max(s, axis=0, keepdims=True))
                p = jnp.exp2(s - m_new).astype(ct_t.dtype)
                pv = jnp.dot(ct_t, p, preferred_element_type=jnp.float32)
                acc_ref[g, :, cols] = jnp.exp2(m_old - m_new) * acc_ref[g, :, cols] + pv
                m_ref[g, :, cols] = m_new

    def far(j, carry):
        attend_tile(j, None, 0.0)
        return carry

    lax.fori_loop(0, i - 1, far, 0)
    attend_tile(jnp.maximum(i - 1, 0), tadj_ref, jnp.where(i >= 1, 0.0, NEG))
    attend_tile(i, tdiag_ref, 0.0)

    rank = wuvt_ref.shape[2]
    parts = []
    for h in range(N_HEADS):
        g, cols = h // hg, slice((h % hg) * tile, (h % hg + 1) * tile)
        ol = (acc_ref[g, 0:rank, cols] / acc_ref[g, rank:rank + 1, cols]).astype(wuvt_ref.dtype)
        parts.append(jnp.dot(wuvt_ref[h], ol, preferred_element_type=jnp.float32))
    o_ref[0] = jnp.concatenate(parts, axis=0).T.astype(o_ref.dtype)


def _dsa_attention(ql, c, ct, qi, ki2, wt, w_uv, rel_bias):
    b, s, r = c.shape
    tile = DSA_TILE
    nt = s // tile
    topk = min(DSA_TOPK_MAX, s // 4)
    assert s % (tile * DSA_COUNT_UNROLL) == 0 and ct.shape[3] == tile and s < I16_SPAN // 2
    rp = ct.shape[2]
    wuvt = jnp.transpose(w_uv, (1, 2, 0)).astype(c.dtype)
    kk = np.arange(tile)[:, None]
    qq = np.arange(tile)[None, :]
    shift = rel_bias[REL_BUCKETS - 1]
    hg = DSA_HEAD_GROUP
    ng = N_HEADS // hg

    def wide(tab):
        return jnp.transpose(tab.reshape(ng, hg, tile, tile), (0, 2, 1, 3)).reshape(ng, tile, hg * tile)

    t_diag = wide(_bias_table(rel_bias, qq - kk, qq >= kk, shift)) * LOG2E
    t_adj = wide(_bias_table(rel_bias, tile + qq - kk, np.ones((tile, tile), bool), shift)) * LOG2E
    ni = qi.shape[2]
    once = pl.Buffered(1)
    return pl.pallas_call(
        functools.partial(_dsa_kernel, tile=tile, topk=topk, s_len=s, hg=hg),
        out_shape=jax.ShapeDtypeStruct((b, s, N_HEADS * HEAD_DIM), c.dtype),
        grid=(b, nt),
        in_specs=[pl.BlockSpec((1, tile, ni), lambda bb, i: (bb, i, 0)),
                  pl.BlockSpec((1, DSA_IDX_HEADS, tile), lambda bb, i: (bb, 0, i)),
                  pl.BlockSpec((1, s, LANES), lambda bb, i: (bb, 0, 0), pipeline_mode=once),
                  pl.BlockSpec((1, s, r), lambda bb, i: (bb, 0, 0), pipeline_mode=once),
                  pl.BlockSpec((1, nt, rp, tile), lambda bb, i: (bb, 0, 0, 0), pipeline_mode=once),
                  pl.BlockSpec((1, N_HEADS, tile, r), lambda bb, i: (bb, 0, i, 0)),
                  pl.BlockSpec((ng, tile, hg * tile), lambda bb, i: (0, 0, 0), pipeline_mode=once),
                  pl.BlockSpec((ng, tile, hg * tile), lambda bb, i: (0, 0, 0), pipeline_mode=once),
                  pl.BlockSpec((N_HEADS, HEAD_DIM, r), lambda bb, i: (0, 0, 0), pipeline_mode=once)],
        out_specs=pl.BlockSpec((1, tile, N_HEADS * HEAD_DIM), lambda bb, i: (bb, i, 0)),
        scratch_shapes=[pltpu.VMEM((s, tile), jnp.int16),
                        pltpu.VMEM((s, tile), jnp.int16),
                        pltpu.VMEM((ng, rp, hg * tile), jnp.float32),
                        pltpu.VMEM((ng, 1, hg * tile), jnp.float32),
                        pltpu.VMEM((ng, r, hg * tile), c.dtype)],
        compiler_params=_params(("parallel", "arbitrary")),
        name="dsa_attention",
    )(qi, wt, ki2, c, ct, ql, t_diag, t_adj, wuvt)


def _dsa_layer(h, g, w_in, kv_norm, w_uk, w_uv, w_out, rel_bias, mxu_dtype):
    b, s, d = h.shape
    ql, c, ct, qi, ki2, wt = _dsa_proj(h, g, w_in, kv_norm, w_uk, mxu_dtype)
    o = _dsa_attention(ql, c, ct, qi, ki2, wt, w_uv, rel_bias)
    return _proj_residual(o.reshape(b * s, -1), w_out.astype(mxu_dtype),
                          h.reshape(b * s, d)).reshape(b, s, d)


def kernel(x, rel_bias, norm_mix, norm_mlp, mlp_up, mlp_down, a_w_in, a_w_out, b_w_in, b_sinks, b_w_out,
           c_w_in, c_kv_norm, c_w_uk, c_w_uv, c_w_out, final_norm):
    b, s, d = x.shape
    t = b * s
    dt = MXU_DTYPE
    depth = norm_mix.shape[0]
    h = x.reshape(t, d)
    for i in range(depth):
        kind, j = i % 3, i // 3
        if kind == 0:
            qkv = _norm_proj(h, norm_mix[i], _moba_in_weights(a_w_in[j], dt)).reshape(b, s, -1)
            o = _moba_attention(qkv, rel_bias)
            h = _proj_residual(o.reshape(t, -1), a_w_out[j].astype(dt), h)
        elif kind == 1:
            proj = _norm_proj(h, norm_mix[i], _swa_in_weights(b_w_in[j], dt)).reshape(b, s, -1)
            o = _swa_attention(proj, b_sinks[j], rel_bias)
            h = _proj_residual(o.reshape(t, -1), b_w_out[j].astype(dt), h)
        else:
            h = _dsa_layer(h.reshape(b, s, d), norm_mix[i], c_w_in[j], c_kv_norm[j], c_w_uk[j],
                           c_w_uv[j], c_w_out[j], rel_bias, dt).reshape(t, d)
        h = _mlp(h, norm_mlp[i], mlp_up[i].astype(dt), mlp_down[i].astype(dt), final_norm,
                 final_norm=(i == depth - 1))
    return h.reshape(b, s, d)
```

```python
import functools
import math

import numpy as np
import jax
import jax.numpy as jnp
from jax import lax
from jax.experimental import pallas as pl
from jax.experimental.pallas import tpu as pltpu

LANES = 128
PACKED_SUBLANES = 16
VMEM_LIMIT_BYTES = 56 * 1024 * 1024
MXU_DTYPE = jnp.bfloat16
NEG = -1e30
LOG2E = math.log2(math.e)
INT_MIN = -2 ** 31
I16_BITS = 16
I16_SPAN = 2 ** I16_BITS

N_HEADS = 16
HEAD_DIM = 64
NORM_EPS = 1e-6
REL_BUCKETS = 32
REL_MAX_DIST = 128
MOBA_BLOCK = 256
MOBA_TOPK = 3
MOBA_GROUP = 4
MOBA_KEY_UNIT = 128
SWA_WINDOW = 128
SWA_BLOCK = 128
SWA_KV_HEADS = 2
DSA_KV_RANK = 256
DSA_IDX_HEADS = 8
DSA_IDX_DIM = 64
DSA_TOPK_MAX = 256
DSA_TILE = 256
DSA_HEAD_GROUP = 8
DSA_COUNT_UNROLL = 4
DSA_BITS_PER_CHECK = 4
MOBA_ONES_ROWS = SWA_ONES_ROWS = DSA_ONES_ROWS = PACKED_SUBLANES

_NT = (((1,), (1,)), ((), ()))


def _params(semantics):
    return pltpu.CompilerParams(dimension_semantics=semantics, vmem_limit_bytes=VMEM_LIMIT_BYTES)


def _rmsnorm(x, g):
    var = jnp.mean(x * x, axis=-1, keepdims=True)
    return x * lax.rsqrt(var + NORM_EPS) * g


def _norm_proj_kernel(x_ref, g_ref, w_ref, o_ref):
    xn = _rmsnorm(x_ref[...], g_ref[...]).astype(w_ref.dtype)
    o_ref[...] = jnp.dot(xn, w_ref[...], preferred_element_type=jnp.float32).astype(o_ref.dtype)


def _norm_proj(x, g, w, *, tm=512):
    t, d = x.shape
    n = w.shape[1]
    assert t % tm == 0
    return pl.pallas_call(
        _norm_proj_kernel,
        out_shape=jax.ShapeDtypeStruct((t, n), w.dtype),
        grid=(t // tm,),
        in_specs=[pl.BlockSpec((tm, d), lambda i: (i, 0)),
                  pl.BlockSpec((1, d), lambda i: (0, 0)),
                  pl.BlockSpec((d, n), lambda i: (0, 0))],
        out_specs=pl.BlockSpec((tm, n), lambda i: (i, 0)),
        compiler_params=_params(("parallel",)),
        name="norm_proj",
    )(x, g.reshape(1, d), w)


def _proj_residual_kernel(a_ref, w_ref, r_ref, o_ref):
    o_ref[...] = r_ref[...] + jnp.dot(a_ref[...], w_ref[...], preferred_element_type=jnp.float32)


def _proj_residual(a, w, res, *, tm=512):
    t, k = a.shape
    n = w.shape[1]
    assert t % tm == 0
    return pl.pallas_call(
        _proj_residual_kernel,
        out_shape=jax.ShapeDtypeStruct((t, n), jnp.float32),
        grid=(t // tm,),
        in_specs=[pl.BlockSpec((tm, k), lambda i: (i, 0)),
                  pl.BlockSpec((k, n), lambda i: (0, 0)),
                  pl.BlockSpec((tm, n), lambda i: (i, 0))],
        out_specs=pl.BlockSpec((tm, n), lambda i: (i, 0)),
        compiler_params=_params(("parallel",)),
        name="proj_residual",
    )(a, w, res)


def _mlp_kernel(x_ref, g_ref, wu_ref, wd_ref, gf_ref, o_ref, xn_ref, acc_ref, *, final_norm):
    f = pl.program_id(1)

    @pl.when(f == 0)
    def _():
        x = x_ref[...]
        xn_ref[...] = _rmsnorm(x, g_ref[...]).astype(xn_ref.dtype)
        acc_ref[...] = x

    u = jnp.dot(xn_ref[...], wu_ref[...], preferred_element_type=jnp.float32)
    a = jnp.square(jnp.maximum(u, 0.0)).astype(wd_ref.dtype)
    acc_ref[...] += jnp.dot(a, wd_ref[...], preferred_element_type=jnp.float32)

    @pl.when(f == pl.num_programs(1) - 1)
    def _():
        y = acc_ref[...]
        if final_norm:
            y = _rmsnorm(y, gf_ref[...])
        o_ref[...] = y


def _mlp(x, g, w_up, w_down, g_final, *, final_norm, tm=1024, tf=1024):
    t, d = x.shape
    ff = w_up.shape[1]
    assert t % tm == 0 and ff % tf == 0
    return pl.pallas_call(
        functools.partial(_mlp_kernel, final_norm=final_norm),
        out_shape=jax.ShapeDtypeStruct((t, d), jnp.float32),
        grid=(t // tm, ff // tf),
        in_specs=[pl.BlockSpec((tm, d), lambda i, f: (i, 0)),
                  pl.BlockSpec((1, d), lambda i, f: (0, 0)),
                  pl.BlockSpec((d, tf), lambda i, f: (0, f)),
                  pl.BlockSpec((tf, d), lambda i, f: (f, 0)),
                  pl.BlockSpec((1, d), lambda i, f: (0, 0))],
        out_specs=pl.BlockSpec((tm, d), lambda i, f: (i, 0)),
        scratch_shapes=[pltpu.VMEM((tm, d), w_up.dtype), pltpu.VMEM((tm, d), jnp.float32)],
        compiler_params=_params(("parallel", "arbitrary")),
        name="mlp",
    )(x, g.reshape(1, d), w_up, w_down, g_final.reshape(1, d))


def _rel_bucket_np(dist):
    n = np.maximum(dist, 0)
    max_exact = REL_BUCKETS // 2
    nf = np.maximum(n, 1).astype(np.float64)
    large = max_exact + (np.log(nf / max_exact) / math.log(REL_MAX_DIST / max_exact)
                         * (REL_BUCKETS - max_exact)).astype(np.int64)
    large = np.minimum(large, REL_BUCKETS - 1)
    return np.where(n < max_exact, n, large).astype(np.int32)


def _bias_table(rel_bias, dist, valid, shift):
    bucket = jnp.asarray(_rel_bucket_np(dist).reshape(1, -1))
    onehot = (bucket == jnp.arange(REL_BUCKETS, dtype=bucket.dtype).reshape(-1, 1)).astype(jnp.float32)
    tab = jnp.einsum("bh,bn->hn", rel_bias, onehot, precision=lax.Precision.HIGHEST)
    tab = tab.reshape((-1,) + dist.shape) - shift.reshape((-1,) + (1,) * dist.ndim)
    return jnp.where(jnp.asarray(valid)[None], tab, NEG).astype(jnp.float32)


def _moba_route_mask(gate_t, i):
    blkid = lax.broadcasted_iota(jnp.int32, gate_t.shape, 0)
    blk_f = blkid.astype(jnp.float32)
    valid = blkid < i
    g = jnp.where(valid, gate_t, -jnp.inf)
    sel = blkid == i
    for _ in range(MOBA_TOPK):
        mx = jnp.max(g, axis=0, keepdims=True)
        first = jnp.min(jnp.where(g == mx, blk_f, float(LANES)), axis=0, keepdims=True)
        pick = (blk_f == first) & valid
        sel = sel | pick
        g = jnp.where(pick, -jnp.inf, g)
    return jnp.where(sel, 0.0, NEG)


def _moba_kernel(q_ref, k_ref, v_ref, tnear_ref, o_ref, kaug_ref, vt_ref, kmrows_ref,
                 *, blk, nblk, group, ku):
    i = pl.program_id(2)
    half = LANES // 2
    pad = group * blk
    lane = lax.broadcasted_iota(jnp.int32, (blk, LANES), 1)
    lo = lane < half

    @pl.when(i == 0)
    def _():
        kmrows_ref[...] = jnp.zeros(kmrows_ref.shape, kmrows_ref.dtype)
        lane1 = lax.broadcasted_iota(jnp.int32, (1, LANES), 1)
        lane_p = lax.broadcasted_iota(jnp.int32, (pad, LANES), 1)
        kaug_ref[0, 0:pad, :] = jnp.where(lane_p == LANES - 1, 1.0, 0.0).astype(kaug_ref.dtype)
        kaug_ref[1, 0:pad, :] = jnp.where(lane_p == half - 1, 1.0, 0.0).astype(kaug_ref.dtype)
        vt_ref[0:pad // ku] = jnp.zeros((pad // ku,) + vt_ref.shape[1:], vt_ref.dtype)
        ones = jnp.ones((vt_ref.shape[2] - half, ku), vt_ref.dtype)

        def build(n, carry):
            rows = pl.ds(pl.multiple_of(n * blk, blk), blk)
            prows = pl.ds(pl.multiple_of(n * blk + pad, blk), blk)
            kn = k_ref[0, rows, :].astype(jnp.float32)
            kaug_ref[0, prows, :] = jnp.where(
                lo, kn, jnp.where(lane == half + n, 1.0, 0.0)).astype(kaug_ref.dtype)
            kaug_ref[1, prows, :] = jnp.where(
                lo, jnp.where(lane == n, 1.0, 0.0), kn).astype(kaug_ref.dtype)
            for t in range(blk // ku):
                vn = v_ref[0, pl.ds(pl.multiple_of(n * blk + t * ku, ku), ku), :]
                vn_t = vn.astype(jnp.float32).T.astype(vt_ref.dtype)
                unit = (n + group) * (blk // ku) + t
                for hh in range(2):
                    vt_ref[unit, hh, 0:half, :] = vn_t[hh * half:(hh + 1) * half]
                    vt_ref[unit, hh, half:, :] = ones
            mean = jnp.sum(kn, axis=0, keepdims=True) * (1.0 / blk)
            kmrows_ref[0, pl.ds(n, 1), :] = jnp.where(lane1 < half, mean, 0.0)
            kmrows_ref[1, pl.ds(n, 1), :] = jnp.where(lane1 < half, 0.0, mean)
            return carry

        lax.fori_loop(0, nblk, build, 0)

    dt = q_ref.dtype
    q_t = q_ref[0].astype(jnp.float32).T
    q_tb = q_t.astype(dt)
    q_aug = []
    for hh in range(2):
        gate_t = jnp.dot(kmrows_ref[hh].astype(dt), q_tb,
                         preferred_element_type=jnp.float32)
        route_t = _moba_route_mask(gate_t, i)
        q_h = q_t[hh * half:(hh + 1) * half]
        q_aug.append(jnp.concatenate([q_h, route_t] if hh == 0 else [route_t, q_h], axis=0).astype(dt))

    upb = blk // ku

    def fold_group(unit0, state, bias_ref=None):
        n_units = group * upb
        rows = pl.ds(pl.multiple_of(unit0 * ku, ku), n_units * ku)
        s = [jnp.dot(kaug_ref[hh, rows, :], q_aug[hh], preferred_element_type=jnp.float32)
             for hh in range(2)]
        state = list(state)
        unbiased = n_units if bias_ref is None else n_units - bias_ref.shape[1] // ku
        for t in range(n_units):
            for hh in range(2):
                m, acc = state[hh]
                s_t = s[hh][t * ku:(t + 1) * ku]
                if t >= unbiased:
                    s_t = s_t + bias_ref[hh, (t - unbiased) * ku:(t - unbiased + 1) * ku, :]
                m_new = jnp.maximum(m, jnp.max(s_t, axis=0, keepdims=True))
                p = jnp.exp2(s_t - m_new).astype(dt)
                acc = jnp.exp2(m - m_new) * acc + jnp.dot(vt_ref[unit0 + t, hh], p,
                                                          preferred_element_type=jnp.float32)
                state[hh] = (m_new, acc)
        return state

    start = (jnp.full((1, blk), NEG, jnp.float32), jnp.zeros((vt_ref.shape[2], blk), jnp.float32))
    state = fold_group((i + 1) * upb, [start, start], tnear_ref)

    def far(g, carry):
        st = fold_group((i + 1 - group * g) * upb, [carry[0:2], carry[2:4]])
        return tuple(st[0]) + tuple(st[1])

    out = lax.fori_loop(1, (i + group) // group, far, tuple(state[0]) + tuple(state[1]))
    o_t = jnp.concatenate([out[1][:half] / out[1][half:half + 1],
                           out[3][:half] / out[3][half:half + 1]], axis=0)
    o_ref[0] = o_t.T.astype(o_ref.dtype)


def _moba_in_weights(w_in, dt):
    nq = N_HEADS * HEAD_DIM
    return jnp.concatenate([w_in[:, :nq] * (LOG2E * HEAD_DIM ** -0.5), w_in[:, nq:]], axis=1).astype(dt)


def _moba_attention(qkv, rel_bias):
    b, s, _ = qkv.shape
    blk = MOBA_BLOCK
    group = MOBA_GROUP
    nblk = s // blk
    hp = N_HEADS // 2
    ku = MOBA_KEY_UNIT
    assert s % blk == 0 and nblk < LANES // 2 and 2 * HEAD_DIM == LANES and blk % ku == 0
    assert group >= 2
    kk = np.arange(blk)[:, None]
    qq = np.arange(blk)[None, :]
    shift = rel_bias[REL_BUCKETS - 1]
    t_own = _bias_table(rel_bias, qq - kk, qq >= kk, shift)
    t_adj = _bias_table(rel_bias, blk + qq - kk, np.ones((blk, blk), bool), shift)
    t_near = jnp.concatenate([t_adj, t_own], axis=1) * LOG2E
    return pl.pallas_call(
        functools.partial(_moba_kernel, blk=blk, nblk=nblk, group=group, ku=ku),
        out_shape=jax.ShapeDtypeStruct((b, s, N_HEADS * HEAD_DIM), qkv.dtype),
        grid=(b, hp, nblk),
        in_specs=[pl.BlockSpec((1, blk, LANES), lambda bb, p, i: (bb, i, p)),
                  pl.BlockSpec((1, s, LANES), lambda bb, p, i: (bb, 0, hp + p)),
                  pl.BlockSpec((1, s, LANES), lambda bb, p, i: (bb, 0, 2 * hp + p)),
                  pl.BlockSpec((2, 2 * blk, blk), lambda bb, p, i: (p, 0, 0))],
        out_specs=pl.BlockSpec((1, blk, LANES), lambda bb, p, i: (bb, i, p)),
        scratch_shapes=[pltpu.VMEM((2, s + group * blk, LANES), qkv.dtype),
                        pltpu.VMEM(((s + group * blk) // ku, 2, HEAD_DIM + MOBA_ONES_ROWS, ku), qkv.dtype),
                        pltpu.VMEM((2, LANES // 2, LANES), jnp.float32)],
        compiler_params=_params(("parallel", "parallel", "arbitrary")),
        name="moba_attention",
    )(qkv, qkv, qkv, t_near)


def _swa_kernel(q_ref, kp_ref, kc_ref, vp_ref, vc_ref, tab_ref, sink_ref, o_ref, *, blk, group):
    n = pl.program_id(1)
    half = LANES // 2
    dt = q_ref.dtype
    q_t = q_ref[0].astype(jnp.float32).T.astype(dt)
    kband = jnp.concatenate([kp_ref[0], kc_ref[0]], axis=0)
    vband = jnp.concatenate([vp_ref[0], vc_ref[0]], axis=0)
    v_t = vband.astype(jnp.float32).T.astype(dt)
    ones = jnp.ones((SWA_ONES_ROWS, 2 * blk), dt)
    zeros = jnp.zeros((half, group * blk), dt)
    tab_n = jnp.minimum(n, 1)

    outs = []
    for kv in range(SWA_KV_HEADS):
        heads = range(kv * group, (kv + 1) * group)
        qg = jnp.concatenate([q_t[h * half:(h + 1) * half] for h in heads], axis=1)
        qg = jnp.concatenate([qg, zeros] if kv == 0 else [zeros, qg], axis=0)
        s = jnp.dot(kband, qg, preferred_element_type=jnp.float32) + tab_ref[tab_n, kv]
        sink = sink_ref[kv]
        m = jnp.maximum(jnp.max(s, axis=0, keepdims=True), sink)
        p = jnp.exp2(s - m).astype(dt)
        v1 = jnp.concatenate([v_t[kv * half:(kv + 1) * half], ones], axis=0)
        acc = jnp.dot(v1, p, preferred_element_type=jnp.float32)
        o = acc[0:half] / (acc[half:half + 1] + jnp.exp2(sink - m))
        outs.extend(o[:, g * blk:(g + 1) * blk] for g in range(group))
    o_ref[0] = jnp.concatenate(outs, axis=0).T.astype(o_ref.dtype)


def _swa_in_weights(w_in, dt):
    nq = N_HEADS * HEAD_DIM
    return jnp.concatenate([w_in[:, :nq] * (LOG2E * HEAD_DIM ** -0.5), w_in[:, nq:]], axis=1).astype(dt)


def _swa_attention(proj, sinks, rel_bias):
    b, s, _ = proj.shape
    blk = SWA_BLOCK
    nb = s // blk
    group = N_HEADS // SWA_KV_HEADS
    nq = N_HEADS * HEAD_DIM
    assert s % blk == 0 and SWA_KV_HEADS * HEAD_DIM == LANES and SWA_WINDOW <= blk
    kcol = nq // LANES
    dist = blk + np.arange(blk)[None, :] - np.arange(2 * blk)[:, None]
    window = (dist >= 0) & (dist < SWA_WINDOW)
    first = window & (np.arange(2 * blk)[:, None] >= blk)
    zero = jnp.zeros_like(sinks)

    def wide(valid):
        tab = _bias_table(rel_bias, dist, valid, zero).reshape(SWA_KV_HEADS, group, 2 * blk, blk)
        return jnp.transpose(tab, (0, 2, 1, 3)).reshape(SWA_KV_HEADS, 2 * blk, group * blk) * LOG2E

    tabs = jnp.stack([wide(first), wide(window)])
    sink_w = jnp.broadcast_to(sinks.astype(jnp.float32).reshape(SWA_KV_HEADS, 1, group, 1) * LOG2E,
                              (SWA_KV_HEADS, 1, group, blk)).reshape(SWA_KV_HEADS, 1, group * blk)
    prev = lambda bb, n: (bb, jnp.maximum(n - 1, 0), kcol)
    cur = lambda bb, n: (bb, n, kcol)
    prev_v = lambda bb, n: (bb, jnp.maximum(n - 1, 0), kcol + 1)
    cur_v = lambda bb, n: (bb, n, kcol + 1)
    return pl.pallas_call(
        functools.partial(_swa_kernel, blk=blk, group=group),
        out_shape=jax.ShapeDtypeStruct((b, s, nq), proj.dtype),
        grid=(b, nb),
        in_specs=[pl.BlockSpec((1, blk, nq), lambda bb, n: (bb, n, 0)),
                  pl.BlockSpec((1, blk, LANES), prev),
                  pl.BlockSpec((1, blk, LANES), cur),
                  pl.BlockSpec((1, blk, LANES), prev_v),
                  pl.BlockSpec((1, blk, LANES), cur_v),
                  pl.BlockSpec(tabs.shape, lambda bb, n: (0, 0, 0, 0)),
                  pl.BlockSpec(sink_w.shape, lambda bb, n: (0, 0, 0))],
        out_specs=pl.BlockSpec((1, blk, nq), lambda bb, n: (bb, n, 0)),
        compiler_params=_params(("parallel", "arbitrary")),
        name="swa_attention",
    )(proj, proj, proj, proj, proj, tabs, sink_w)


def _dsa_proj_kernel(x_ref, g_ref, w_ref, wwt_ref, kvn_ref, wuk_ref,
                     ql_ref, c_ref, ct_ref, qi_ref, ki_ref, wt_ref, *, idx_scale):
    dt = w_ref.dtype
    nq = N_HEADS * HEAD_DIM
    r = DSA_KV_RANK
    ni = DSA_IDX_HEADS * DSA_IDX_DIM
    xn = _rmsnorm(x_ref[0], g_ref[...]).astype(dt)
    y = jnp.dot(xn, w_ref[...], preferred_element_type=jnp.float32)
    scale = LOG2E * HEAD_DIM ** -0.5
    for p in range(N_HEADS // 2):
        qp = y[:, p * LANES:(p + 1) * LANES].astype(dt)
        ql = jnp.dot(qp, wuk_ref[p], preferred_element_type=jnp.float32) * scale
        ql_ref[0, 2 * p] = ql[:, :r].astype(ql_ref.dtype)
        ql_ref[0, 2 * p + 1] = ql[:, r:].astype(ql_ref.dtype)
    c = _rmsnorm(y[:, nq:nq + r], kvn_ref[...])
    c_ref[0] = c.astype(c_ref.dtype)
    kt = ct_ref.shape[3]
    for t in range(ct_ref.shape[1]):
        ct_ref[0, t, 0:r, :] = c[t * kt:(t + 1) * kt].T.astype(ct_ref.dtype)
        ct_ref[0, t, r:, :] = jnp.ones((ct_ref.shape[2] - r, kt), ct_ref.dtype)
    qi_ref[0] = y[:, nq + r:nq + r + ni].astype(qi_ref.dtype)
    ki_ref[0] = y[:, nq + r + ni:nq + r + ni + LANES].astype(ki_ref.dtype)
    wt = lax.dot_general(wwt_ref[...], xn, _NT, preferred_element_type=jnp.float32)
    wt_ref[0] = wt * idx_scale


def _dsa_proj(x, g, w_in, kv_norm, w_uk, dt, *, tm=512):
    b, s, d = x.shape
    nq = N_HEADS * HEAD_DIM
    r = DSA_KV_RANK
    ni = DSA_IDX_HEADS * DSA_IDX_DIM
    di = DSA_IDX_DIM
    assert 2 * di == LANES and 2 * HEAD_DIM == LANES and s % tm == 0
    k_idx = w_in[:, nq + r + ni:nq + r + ni + di]
    w_main = jnp.concatenate([w_in[:, :nq + r + ni], k_idx, k_idx], axis=1).astype(dt)
    wwt = w_in[:, nq + r + ni + di:].T.astype(dt)
    uk = jnp.transpose(w_uk, (1, 2, 0)).reshape(N_HEADS // 2, 2, HEAD_DIM, r)
    z = jnp.zeros_like(uk[:, 0])
    wuk_bd = jnp.concatenate([jnp.concatenate([uk[:, 0], z], axis=2),
                              jnp.concatenate([z, uk[:, 1]], axis=2)], axis=1).astype(dt)
    nw = w_main.shape[1]
    idx_scale = DSA_IDX_HEADS ** -0.5 * DSA_IDX_DIM ** -0.5
    kt = DSA_TILE
    rp = r + DSA_ONES_ROWS
    assert tm % kt == 0
    return pl.pallas_call(
        functools.partial(_dsa_proj_kernel, idx_scale=idx_scale),
        out_shape=(jax.ShapeDtypeStruct((b, N_HEADS, s, r), dt),
                   jax.ShapeDtypeStruct((b, s, r), dt),
                   jax.ShapeDtypeStruct((b, s // kt, rp, kt), dt),
                   jax.ShapeDtypeStruct((b, s, ni), dt),
                   jax.ShapeDtypeStruct((b, s, LANES), dt),
                   jax.ShapeDtypeStruct((b, DSA_IDX_HEADS, s), jnp.float32)),
        grid=(b, s // tm),
        in_specs=[pl.BlockSpec((1, tm, d), lambda bb, i: (bb, i, 0)),
                  pl.BlockSpec((1, d), lambda bb, i: (0, 0)),
                  pl.BlockSpec((d, nw), lambda bb, i: (0, 0)),
                  pl.BlockSpec((DSA_IDX_HEADS, d), lambda bb, i: (0, 0)),
                  pl.BlockSpec((1, r), lambda bb, i: (0, 0)),
                  pl.BlockSpec((N_HEADS // 2, LANES, 2 * r), lambda bb, i: (0, 0, 0))],
        out_specs=(pl.BlockSpec((1, N_HEADS, tm, r), lambda bb, i: (bb, 0, i, 0)),
                   pl.BlockSpec((1, tm, r), lambda bb, i: (bb, i, 0)),
                   pl.BlockSpec((1, tm // kt, rp, kt), lambda bb, i: (bb, i, 0, 0)),
                   pl.BlockSpec((1, tm, ni), lambda bb, i: (bb, i, 0)),
                   pl.BlockSpec((1, tm, LANES), lambda bb, i: (bb, i, 0)),
                   pl.BlockSpec((1, DSA_IDX_HEADS, tm), lambda bb, i: (bb, 0, i))),
        compiler_params=_params(("parallel", "parallel")),
        name="dsa_proj",
    )(x, g.reshape(1, d), w_main, wwt, kv_norm.reshape(1, r), wuk_bd)


def _dsa_kernel(qi_ref, wt_ref, ki_ref, c_ref, ct_ref, ql_ref, tdiag_ref, tadj_ref, wuvt_ref, o_ref,
                hi_ref, lo_ref, acc_ref, m_ref, qlt_ref, *, tile, topk, s_len, hg):
    i = pl.program_id(1)
    half = LANES // 2
    nih = DSA_IDX_HEADS

    def tile_rows(j):
        return pl.ds(pl.multiple_of(j * tile, tile), tile)

    dt = qi_ref.dtype
    zeros = jnp.zeros((half, tile), jnp.float32)
    parts = []
    for pair in range(nih // 2):
        pair_t = qi_ref[0, :, pair * LANES:(pair + 1) * LANES].astype(jnp.float32).T
        parts.append(jnp.concatenate([pair_t[:half], zeros], axis=0))
        parts.append(jnp.concatenate([zeros, pair_t[half:]], axis=0))
    qs_t = jnp.concatenate(parts, axis=1).astype(dt)
    w_t = wt_ref[0]
    krow = lax.broadcasted_iota(jnp.int32, (tile, tile), 0)
    qcol = lax.broadcasted_iota(jnp.int32, (tile, tile), 1)

    def index_tile(j, carry):
        rows = tile_rows(j)
        rel = jnp.dot(ki_ref[0, rows, :], qs_t, preferred_element_type=jnp.float32)
        sc = jnp.zeros((tile, tile), jnp.float32)
        for h in range(nih):
            sc = sc + jnp.maximum(rel[:, h * tile:(h + 1) * tile], 0.0) * w_t[h:h + 1, :]
        sc = jnp.where(sc == 0.0, 0.0, sc)
        bits = lax.bitcast_convert_type(sc, jnp.int32)
        u = jnp.where(bits < 0, bits ^ jnp.int32(0x7FFFFFFF), bits)
        causal = (j * tile + krow) <= (i * tile + qcol)
        u = jnp.where(causal, u, jnp.int32(INT_MIN))
        hi_ref[rows, :] = lax.shift_right_arithmetic(u, I16_BITS).astype(jnp.int16)
        lo_ref[rows, :] = ((u & jnp.int32(I16_SPAN - 1)) - I16_SPAN // 2).astype(jnp.int16)
        return carry

    lax.fori_loop(0, i + 1, index_tile, 0)

    cu = DSA_COUNT_UNROLL
    i16_min = jnp.int16(-I16_SPAN // 2)
    for k in range(1, cu):
        @pl.when(i + k < s_len // tile)
        def _(k=k):
            hi_ref[tile_rows(i + k), :] = jnp.full((tile, tile), i16_min, jnp.int16)
            lo_ref[tile_rows(i + k), :] = jnp.full((tile, tile), i16_min, jnp.int16)

    def count(pred):
        def body(jg, acc):
            for k in range(cu):
                j = jg * cu + k
                ones = jnp.where(pred(j, tile_rows(j)), jnp.int16(1), jnp.int16(0))
                for r in range(0, tile, PACKED_SUBLANES):
                    acc = acc + ones[r:r + PACKED_SUBLANES]
            return acc
        acc = lax.fori_loop(0, (i + cu) // cu, body, jnp.zeros((PACKED_SUBLANES, tile), jnp.int16))
        return jnp.sum(acc.astype(jnp.float32), axis=0, keepdims=True)

    def to_i16(x):
        return x.astype(jnp.int16)

    few = (i * tile + lax.broadcasted_iota(jnp.int32, (1, tile), 1)) < topk

    def search16(ref, target):
        def unsettled(carry):
            bi, _, cnt_ans = carry
            pending = jnp.logical_not(few) & (cnt_ans != target)
            return (bi < I16_BITS) & (jnp.max(jnp.where(pending, 1.0, 0.0)) > 0.0)

        def bit_steps(carry):
            bi, ans, cnt_ans = carry
            for k in range(DSA_BITS_PER_CHECK):
                cand = ans | lax.shift_left(jnp.int32(1), I16_BITS - 1 - (bi + k))
                t16 = to_i16(cand - I16_SPAN // 2)
                cnt = count(lambda j, rows, t16=t16: ref[rows, :] >= t16)
                ok = cnt >= target
                ans, cnt_ans = jnp.where(ok, cand, ans), jnp.where(ok, cnt, cnt_ans)
            return bi + DSA_BITS_PER_CHECK, ans, cnt_ans

        _, ans, cnt = lax.while_loop(unsettled, bit_steps,
                                     (jnp.int32(0), jnp.zeros((1, tile), jnp.int32),
                                      jnp.zeros((1, tile), jnp.float32)))
        return ans - I16_SPAN // 2, cnt

    def count_above(ref, t):
        cnt = count(lambda j, rows: ref[rows, :] >= to_i16(jnp.minimum(t + 1, I16_SPAN // 2 - 1)))
        return jnp.where(t + 1 > I16_SPAN // 2 - 1, 0.0, cnt)

    topk_f = jnp.full((1, tile), float(topk), jnp.float32)
    thr_hi, cnt_hi = search16(hi_ref, topk_f)
    cnt_above = count_above(hi_ref, thr_hi)
    need_lo = topk_f - cnt_above
    h16 = to_i16(thr_hi)

    def keep_low(jg, carry):
        for k in range(cu):
            rows = tile_rows(jg * cu + k)
            lo_ref[rows, :] = jnp.where(hi_ref[rows, :] == h16, lo_ref[rows, :], i16_min)
        return carry

    lax.fori_loop(0, (i + cu) // cu, keep_low, 0)
    thr_lo, cnt_lo = search16(lo_ref, need_lo)
    cnt_lo = jnp.where(cnt_lo == 0.0, cnt_hi - cnt_above, cnt_lo)

    nbits = int(s_len).bit_length()
    cut_all = jnp.full((1, tile), 2 ** nbits - 1, jnp.int32)
    krow16 = lax.broadcasted_iota(jnp.int16, (tile, tile), 0)
    l16 = to_i16(thr_lo)

    def tie_search():
        need = need_lo - count_above(lo_ref, thr_lo)

        def cut_step(bi, cut):
            cand = cut | lax.shift_left(jnp.int32(1), nbits - 1 - bi)
            cnt = count(lambda j, rows: (lo_ref[rows, :] == l16) & (hi_ref[rows, :] == h16)
                        & (krow16 < to_i16(jnp.minimum(cand - j * tile, I16_SPAN // 2 - 1))))
            return jnp.where(cnt <= need, cand, cut)

        return lax.fori_loop(0, nbits, cut_step, jnp.zeros((1, tile), jnp.int32))

    tie = jnp.logical_not(few) & (cnt_lo > need_lo)
    cut = lax.cond(jnp.max(jnp.where(tie, 1.0, 0.0)) > 0.0, tie_search, lambda: cut_all)
    l16_sel = to_i16(jnp.where(thr_hi == -I16_SPAN // 2, I16_SPAN // 2 - 1, thr_lo))
    mask_0 = jnp.zeros((), c_ref.dtype)
    mask_neg = jnp.asarray(NEG, c_ref.dtype)

    m_ref[...] = jnp.full(m_ref.shape, NEG, jnp.float32)
    acc_ref[...] = jnp.zeros(acc_ref.shape, jnp.float32)
    for h in range(N_HEADS):
        qlt_ref[h // hg, :, (h % hg) * tile:(h % hg + 1) * tile] = (
            ql_ref[0, h].astype(jnp.float32).T.astype(qlt_ref.dtype))

    def attend_tile(j, table_ref, pen):
        rows = tile_rows(j)
        hi = hi_ref[rows, :]
        lo = lo_ref[rows, :]
        cut16 = to_i16(jnp.clip(cut - j * tile, -I16_SPAN // 2, I16_SPAN // 2 - 1))
        tie_m = jnp.where(krow16 < cut16, mask_0, mask_neg)
        low_m = jnp.where(lo > l16_sel, mask_0, jnp.where(lo == l16_sel, tie_m, mask_neg))
        mask16 = jnp.where(hi > h16, mask_0, jnp.where(hi == h16, low_m, mask_neg))
        maskadd = mask16.astype(jnp.float32) + pen
        c_t = c_ref[0, rows, :]
        ct_t = ct_ref[0, j]

        for g in range(N_HEADS // hg):
            logits = jnp.dot(c_t, qlt_ref[g], preferred_element_type=jnp.float32)
            for hl in range(hg):
                cols = slice(hl * tile, (hl + 1) * tile)
                s = logits[:, cols] + maskadd
                if table_ref is not None:
                    s = s + table_ref[g, :, cols]
                m_old = m_ref[g, :, cols]
                m_new = jnp.maximum(m_old, jnp.max(s, axis=0, keepdims=True))
                p = jnp.exp2(s - m_new).astype(ct_t.dtype)
                pv = jnp.dot(ct_t, p, preferred_element_type=jnp.float32)
                acc_ref[g, :, cols] = jnp.exp2(m_old - m_new) * acc_ref[g, :, cols] + pv
                m_ref[g, :, cols] = m_new

    def far(j, carry):
        attend_tile(j, None, 0.0)
        return carry

    lax.fori_loop(0, i - 1, far, 0)
    attend_tile(jnp.maximum(i - 1, 0), tadj_ref, jnp.where(i >= 1, 0.0, NEG))
    attend_tile(i, tdiag_ref, 0.0)

    rank = wuvt_ref.shape[2]
    parts = []
    for h in range(N_HEADS):
        g, cols = h // hg, slice((h % hg) * tile, (h % hg + 1) * tile)
        ol = (acc_ref[g, 0:rank, cols] / acc_ref[g, rank:rank + 1, cols]).astype(wuvt_ref.dtype)
        parts.append(jnp.dot(wuvt_ref[h], ol, preferred_element_type=jnp.float32))
    o_ref[0] = jnp.concatenate(parts, axis=0).T.astype(o_ref.dtype)


def _dsa_attention(ql, c, ct, qi, ki2, wt, w_uv, rel_bias):
    b, s, r = c.shape
    tile = DSA_TILE
    nt = s // tile
    topk = min(DSA_TOPK_MAX, s // 4)
    assert s % (tile * DSA_COUNT_UNROLL) == 0 and ct.shape[3] == tile and s < I16_SPAN // 2
    rp = ct.shape[2]
    wuvt = jnp.transpose(w_uv, (1, 2, 0)).astype(c.dtype)
    kk = np.arange(tile)[:, None]
    qq = np.arange(tile)[None, :]
    shift = rel_bias[REL_BUCKETS - 1]
    hg = DSA_HEAD_GROUP
    ng = N_HEADS // hg

    def wide(tab):
        return jnp.transpose(tab.reshape(ng, hg, tile, tile), (0, 2, 1, 3)).reshape(ng, tile, hg * tile)

    t_diag = wide(_bias_table(rel_bias, qq - kk, qq >= kk, shift)) * LOG2E
    t_adj = wide(_bias_table(rel_bias, tile + qq - kk, np.ones((tile, tile), bool), shift)) * LOG2E
    ni = qi.shape[2]
    once = pl.Buffered(1)
    return pl.pallas_call(
        functools.partial(_dsa_kernel, tile=tile, topk=topk, s_len=s, hg=hg),
        out_shape=jax.ShapeDtypeStruct((b, s, N_HEADS * HEAD_DIM), c.dtype),
        grid=(b, nt),
        in_specs=[pl.BlockSpec((1, tile, ni), lambda bb, i: (bb, i, 0)),
                  pl.BlockSpec((1, DSA_IDX_HEADS, tile), lambda bb, i: (bb, 0, i)),
                  pl.BlockSpec((1, s, LANES), lambda bb, i: (bb, 0, 0), pipeline_mode=once),
                  pl.BlockSpec((1, s, r), lambda bb, i: (bb, 0, 0), pipeline_mode=once),
                  pl.BlockSpec((1, nt, rp, tile), lambda bb, i: (bb, 0, 0, 0), pipeline_mode=once),
                  pl.BlockSpec((1, N_HEADS, tile, r), lambda bb, i: (bb, 0, i, 0)),
                  pl.BlockSpec((ng, tile, hg * tile), lambda bb, i: (0, 0, 0), pipeline_mode=once),
                  pl.BlockSpec((ng, tile, hg * tile), lambda bb, i: (0, 0, 0), pipeline_mode=once),
                  pl.BlockSpec((N_HEADS, HEAD_DIM, r), lambda bb, i: (0, 0, 0), pipeline_mode=once)],
        out_specs=pl.BlockSpec((1, tile, N_HEADS * HEAD_DIM), lambda bb, i: (bb, i, 0)),
        scratch_shapes=[pltpu.VMEM((s, tile), jnp.int16),
                        pltpu.VMEM((s, tile), jnp.int16),
                        pltpu.VMEM((ng, rp, hg * tile), jnp.float32),
                        pltpu.VMEM((ng, 1, hg * tile), jnp.float32),
                        pltpu.VMEM((ng, r, hg * tile), c.dtype)],
        compiler_params=_params(("parallel", "arbitrary")),
        name="dsa_attention",
    )(qi, wt, ki2, c, ct, ql, t_diag, t_adj, wuvt)


def _dsa_layer(h, g, w_in, kv_norm, w_uk, w_uv, w_out, rel_bias, mxu_dtype):
    b, s, d = h.shape
    ql, c, ct, qi, ki2, wt = _dsa_proj(h, g, w_in, kv_norm, w_uk, mxu_dtype)
    o = _dsa_attention(ql, c, ct, qi, ki2, wt, w_uv, rel_bias)
    return _proj_residual(o.reshape(b * s, -1), w_out.astype(mxu_dtype),
                          h.reshape(b * s, d)).reshape(b, s, d)


def kernel(x, rel_bias, norm_mix, norm_mlp, mlp_up, mlp_down, a_w_in, a_w_out, b_w_in, b_sinks, b_w_out,
           c_w_in, c_kv_norm, c_w_uk, c_w_uv, c_w_out, final_norm):
    b, s, d = x.shape
    t = b * s
    dt = MXU_DTYPE
    depth = norm_mix.shape[0]
    h = x.reshape(t, d)
    for i in range(depth):
        kind, j = i % 3, i // 3
        if kind == 0:
            qkv = _norm_proj(h, norm_mix[i], _moba_in_weights(a_w_in[j], dt)).reshape(b, s, -1)
            o = _moba_attention(qkv, rel_bias)
            h = _proj_residual(o.reshape(t, -1), a_w_out[j].astype(dt), h)
        elif kind == 1:
            proj = _norm_proj(h, norm_mix[i], _swa_in_weights(b_w_in[j], dt)).reshape(b, s, -1)
            o = _swa_attention(proj, b_sinks[j], rel_bias)
            h = _proj_residual(o.reshape(t, -1), b_w_out[j].astype(dt), h)
        else:
            h = _dsa_layer(h.reshape(b, s, d), norm_mix[i], c_w_in[j], c_kv_norm[j], c_w_uk[j],
                           c_w_uv[j], c_w_out[j], rel_bias, dt).reshape(t, d)
        h = _mlp(h, norm_mlp[i], mlp_up[i].astype(dt), mlp_down[i].astype(dt), final_norm,
                 final_norm=(i == depth - 1))
    return h.reshape(b, s, d)
```

```python
import functools
import math

import numpy as np
import jax
import jax.numpy as jnp
from jax import lax
from jax.experimental import pallas as pl
from jax.experimental.pallas import tpu as pltpu

LANES = 128
PACKED_SUBLANES = 16
VMEM_LIMIT_BYTES = 56 * 1024 * 1024
MXU_DTYPE = jnp.bfloat16
NEG = -1e30
LOG2E = math.log2(math.e)
INT_MIN = -2 ** 31
I16_BITS = 16
I16_SPAN = 2 ** I16_BITS

N_HEADS = 16
HEAD_DIM = 64
NORM_EPS = 1e-6
REL_BUCKETS = 32
REL_MAX_DIST = 128
MOBA_BLOCK = 256
MOBA_TOPK = 3
MOBA_GROUP = 4
MOBA_KEY_UNIT = 128
SWA_WINDOW = 128
SWA_BLOCK = 128
SWA_KV_HEADS = 2
DSA_KV_RANK = 256
DSA_IDX_HEADS = 8
DSA_IDX_DIM = 64
DSA_TOPK_MAX = 256
DSA_TILE = 256
DSA_HEAD_GROUP = 8
DSA_COUNT_UNROLL = 4
DSA_BITS_PER_CHECK = 4
MOBA_ONES_ROWS = SWA_ONES_ROWS = DSA_ONES_ROWS = PACKED_SUBLANES

_NT = (((1,), (1,)), ((), ()))


def _params(semantics):
    return pltpu.CompilerParams(dimension_semantics=semantics, vmem_limit_bytes=VMEM_LIMIT_BYTES)


def _rmsnorm(x, g):
    var = jnp.mean(x * x, axis=-1, keepdims=True)
    return x * lax.rsqrt(var + NORM_EPS) * g


def _norm_proj_kernel(x_ref, g_ref, w_ref, o_ref):
    xn = _rmsnorm(x_ref[...], g_ref[...]).astype(w_ref.dtype)
    o_ref[...] = jnp.dot(xn, w_ref[...], preferred_element_type=jnp.float32).astype(o_ref.dtype)


def _norm_proj(x, g, w, *, tm=512):
    t, d = x.shape
    n = w.shape[1]
    assert t % tm == 0
    return pl.pallas_call(
        _norm_proj_kernel,
        out_shape=jax.ShapeDtypeStruct((t, n), w.dtype),
        grid=(t // tm,),
        in_specs=[pl.BlockSpec((tm, d), lambda i: (i, 0)),
                  pl.BlockSpec((1, d), lambda i: (0, 0)),
                  pl.BlockSpec((d, n), lambda i: (0, 0))],
        out_specs=pl.BlockSpec((tm, n), lambda i: (i, 0)),
        compiler_params=_params(("parallel",)),
        name="norm_proj",
    )(x, g.reshape(1, d), w)


def _mlp_kernel(a_ref, wo_ref, r_ref, g_ref, wu_ref, wd_ref, gf_ref, o_ref, xn_ref, acc_ref,
                *, final_norm):
    f = pl.program_id(1)

    @pl.when(f == 0)
    def _():
        x = r_ref[...] + jnp.dot(a_ref[...], wo_ref[...], preferred_element_type=jnp.float32)
        xn_ref[...] = _rmsnorm(x, g_ref[...]).astype(xn_ref.dtype)
        acc_ref[...] = x

    u = jnp.dot(xn_ref[...], wu_ref[...], preferred_element_type=jnp.float32)
    a = jnp.square(jnp.maximum(u, 0.0)).astype(wd_ref.dtype)
    acc_ref[...] += jnp.dot(a, wd_ref[...], preferred_element_type=jnp.float32)

    @pl.when(f == pl.num_programs(1) - 1)
    def _():
        y = acc_ref[...]
        if final_norm:
            y = _rmsnorm(y, gf_ref[...])
        o_ref[...] = y


def _out_proj_mlp(a, w_out, res, g, w_up, w_down, g_final, *, final_norm, tm=1024, tf=1024):
    t, d = res.shape
    k = a.shape[1]
    ff = w_up.shape[1]
    assert t % tm == 0 and ff % tf == 0
    return pl.pallas_call(
        functools.partial(_mlp_kernel, final_norm=final_norm),
        out_shape=jax.ShapeDtypeStruct((t, d), jnp.float32),
        grid=(t // tm, ff // tf),
        in_specs=[pl.BlockSpec((tm, k), lambda i, f: (i, 0)),
                  pl.BlockSpec((k, d), lambda i, f: (0, 0)),
                  pl.BlockSpec((tm, d), lambda i, f: (i, 0)),
                  pl.BlockSpec((1, d), lambda i, f: (0, 0)),
                  pl.BlockSpec((d, tf), lambda i, f: (0, f)),
                  pl.BlockSpec((tf, d), lambda i, f: (f, 0)),
                  pl.BlockSpec((1, d), lambda i, f: (0, 0))],
        out_specs=pl.BlockSpec((tm, d), lambda i, f: (i, 0)),
        scratch_shapes=[pltpu.VMEM((tm, d), w_up.dtype), pltpu.VMEM((tm, d), jnp.float32)],
        compiler_params=_params(("parallel", "arbitrary")),
        name="out_proj_mlp",
    )(a, w_out, res, g.reshape(1, d), w_up, w_down, g_final.reshape(1, d))


def _rel_bucket_np(dist):
    n = np.maximum(dist, 0)
    max_exact = REL_BUCKETS // 2
    nf = np.maximum(n, 1).astype(np.float64)
    large = max_exact + (np.log(nf / max_exact) / math.log(REL_MAX_DIST / max_exact)
                         * (REL_BUCKETS - max_exact)).astype(np.int64)
    large = np.minimum(large, REL_BUCKETS - 1)
    return np.where(n < max_exact, n, large).astype(np.int32)


def _bias_table(rel_bias, dist, valid, shift):
    bucket = jnp.asarray(_rel_bucket_np(dist).reshape(1, -1))
    onehot = (bucket == jnp.arange(REL_BUCKETS, dtype=bucket.dtype).reshape(-1, 1)).astype(jnp.float32)
    tab = jnp.einsum("bh,bn->hn", rel_bias, onehot, precision=lax.Precision.HIGHEST)
    tab = tab.reshape((-1,) + dist.shape) - shift.reshape((-1,) + (1,) * dist.ndim)
    return jnp.where(jnp.asarray(valid)[None], tab, NEG).astype(jnp.float32)


def _moba_route_mask(gate_t, i):
    blkid = lax.broadcasted_iota(jnp.int32, gate_t.shape, 0)
    blk_f = blkid.astype(jnp.float32)
    valid = blkid < i
    g = jnp.where(valid, gate_t, -jnp.inf)
    sel = blkid == i
    for _ in range(MOBA_TOPK):
        mx = jnp.max(g, axis=0, keepdims=True)
        first = jnp.min(jnp.where(g == mx, blk_f, float(LANES)), axis=0, keepdims=True)
        pick = (blk_f == first) & valid
        sel = sel | pick
        g = jnp.where(pick, -jnp.inf, g)
    return jnp.where(sel, 0.0, NEG)


def _moba_kernel(q_ref, k_ref, v_ref, tnear_ref, o_ref, kaug_ref, vt_ref, kmrows_ref,
                 *, blk, nblk, group, ku):
    i = pl.program_id(2)
    half = LANES // 2
    pad = group * blk
    lane = lax.broadcasted_iota(jnp.int32, (blk, LANES), 1)
    lo = lane < half

    @pl.when(i == 0)
    def _():
        kmrows_ref[...] = jnp.zeros(kmrows_ref.shape, kmrows_ref.dtype)
        lane1 = lax.broadcasted_iota(jnp.int32, (1, LANES), 1)
        lane_p = lax.broadcasted_iota(jnp.int32, (pad, LANES), 1)
        kaug_ref[0, 0:pad, :] = jnp.where(lane_p == LANES - 1, 1.0, 0.0).astype(kaug_ref.dtype)
        kaug_ref[1, 0:pad, :] = jnp.where(lane_p == half - 1, 1.0, 0.0).astype(kaug_ref.dtype)
        vt_ref[0:pad // ku] = jnp.zeros((pad // ku,) + vt_ref.shape[1:], vt_ref.dtype)
        ones = jnp.ones((vt_ref.shape[2] - half, ku), vt_ref.dtype)

        def build(n, carry):
            rows = pl.ds(pl.multiple_of(n * blk, blk), blk)
            prows = pl.ds(pl.multiple_of(n * blk + pad, blk), blk)
            kn = k_ref[0, rows, :].astype(jnp.float32)
            kaug_ref[0, prows, :] = jnp.where(
                lo, kn, jnp.where(lane == half + n, 1.0, 0.0)).astype(kaug_ref.dtype)
            kaug_ref[1, prows, :] = jnp.where(
                lo, jnp.where(lane == n, 1.0, 0.0), kn).astype(kaug_ref.dtype)
            for t in range(blk // ku):
                vn = v_ref[0, pl.ds(pl.multiple_of(n * blk + t * ku, ku), ku), :]
                vn_t = vn.astype(jnp.float32).T.astype(vt_ref.dtype)
                unit = (n + group) * (blk // ku) + t
                for hh in range(2):
                    vt_ref[unit, hh, 0:half, :] = vn_t[hh * half:(hh + 1) * half]
                    vt_ref[unit, hh, half:, :] = ones
            mean = jnp.sum(kn, axis=0, keepdims=True) * (1.0 / blk)
            kmrows_ref[0, pl.ds(n, 1), :] = jnp.where(lane1 < half, mean, 0.0)
            kmrows_ref[1, pl.ds(n, 1), :] = jnp.where(lane1 < half, 0.0, mean)
            return carry

        lax.fori_loop(0, nblk, build, 0)

    dt = q_ref.dtype
    q_t = q_ref[0].astype(jnp.float32).T
    q_tb = q_t.astype(dt)
    q_aug = []
    for hh in range(2):
        gate_t = jnp.dot(kmrows_ref[hh].astype(dt), q_tb,
                         preferred_element_type=jnp.float32)
        route_t = _moba_route_mask(gate_t, i)
        q_h = q_t[hh * half:(hh + 1) * half]
        q_aug.append(jnp.concatenate([q_h, route_t] if hh == 0 else [route_t, q_h], axis=0).astype(dt))

    upb = blk // ku

    def fold_group(unit0, state, bias_ref=None):
        n_units = group * upb
        rows = pl.ds(pl.multiple_of(unit0 * ku, ku), n_units * ku)
        s = [jnp.dot(kaug_ref[hh, rows, :], q_aug[hh], preferred_element_type=jnp.float32)
             for hh in range(2)]
        state = list(state)
        unbiased = n_units if bias_ref is None else n_units - bias_ref.shape[1] // ku
        for t in range(n_units):
            for hh in range(2):
                m, acc = state[hh]
                s_t = s[hh][t * ku:(t + 1) * ku]
                if t >= unbiased:
                    s_t = s_t + bias_ref[hh, (t - unbiased) * ku:(t - unbiased + 1) * ku, :]
                m_new = jnp.maximum(m, jnp.max(s_t, axis=0, keepdims=True))
                p = jnp.exp2(s_t - m_new).astype(dt)
                acc = jnp.exp2(m - m_new) * acc + jnp.dot(vt_ref[unit0 + t, hh], p,
                                                          preferred_element_type=jnp.float32)
                state[hh] = (m_new, acc)
        return state

    start = (jnp.full((1, blk), NEG, jnp.float32), jnp.zeros((vt_ref.shape[2], blk), jnp.float32))
    state = fold_group((i + 1) * upb, [start, start], tnear_ref)

    def far(g, carry):
        st = fold_group((i + 1 - group * g) * upb, [carry[0:2], carry[2:4]])
        return tuple(st[0]) + tuple(st[1])

    out = lax.fori_loop(1, (i + group) // group, far, tuple(state[0]) + tuple(state[1]))
    o_t = jnp.concatenate([out[1][:half] / out[1][half:half + 1],
                           out[3][:half] / out[3][half:half + 1]], axis=0)
    o_ref[0] = o_t.T.astype(o_ref.dtype)


def _moba_in_weights(w_in, dt):
    nq = N_HEADS * HEAD_DIM
    return jnp.concatenate([w_in[:, :nq] * (LOG2E * HEAD_DIM ** -0.5), w_in[:, nq:]], axis=1).astype(dt)


def _moba_attention(qkv, rel_bias):
    b, s, _ = qkv.shape
    blk = MOBA_BLOCK
    group = MOBA_GROUP
    nblk = s // blk
    hp = N_HEADS // 2
    ku = MOBA_KEY_UNIT
    assert s % blk == 0 and nblk < LANES // 2 and 2 * HEAD_DIM == LANES and blk % ku == 0
    assert group >= 2
    kk = np.arange(blk)[:, None]
    qq = np.arange(blk)[None, :]
    shift = rel_bias[REL_BUCKETS - 1]
    t_own = _bias_table(rel_bias, qq - kk, qq >= kk, shift)
    t_adj = _bias_table(rel_bias, blk + qq - kk, np.ones((blk, blk), bool), shift)
    t_near = jnp.concatenate([t_adj, t_own], axis=1) * LOG2E
    return pl.pallas_call(
        functools.partial(_moba_kernel, blk=blk, nblk=nblk, group=group, ku=ku),
        out_shape=jax.ShapeDtypeStruct((b, s, N_HEADS * HEAD_DIM), qkv.dtype),
        grid=(b, hp, nblk),
        in_specs=[pl.BlockSpec((1, blk, LANES), lambda bb, p, i: (bb, i, p)),
                  pl.BlockSpec((1, s, LANES), lambda bb, p, i: (bb, 0, hp + p)),
                  pl.BlockSpec((1, s, LANES), lambda bb, p, i: (bb, 0, 2 * hp + p)),
                  pl.BlockSpec((2, 2 * blk, blk), lambda bb, p, i: (p, 0, 0))],
        out_specs=pl.BlockSpec((1, blk, LANES), lambda bb, p, i: (bb, i, p)),
        scratch_shapes=[pltpu.VMEM((2, s + group * blk, LANES), qkv.dtype),
                        pltpu.VMEM(((s + group * blk) // ku, 2, HEAD_DIM + MOBA_ONES_ROWS, ku), qkv.dtype),
                        pltpu.VMEM((2, LANES // 2, LANES), jnp.float32)],
        compiler_params=_params(("parallel", "parallel", "arbitrary")),
        name="moba_attention",
    )(qkv, qkv, qkv, t_near)


def _swa_kernel(q_ref, kp_ref, kc_ref, vp_ref, vc_ref, tab_ref, sink_ref, o_ref, *, blk, group):
    n = pl.program_id(1)
    half = LANES // 2
    dt = q_ref.dtype
    q_t = q_ref[0].astype(jnp.float32).T.astype(dt)
    kband = jnp.concatenate([kp_ref[0], kc_ref[0]], axis=0)
    vband = jnp.concatenate([vp_ref[0], vc_ref[0]], axis=0)
    v_t = vband.astype(jnp.float32).T.astype(dt)
    ones = jnp.ones((SWA_ONES_ROWS, 2 * blk), dt)
    zeros = jnp.zeros((half, group * blk), dt)
    tab_n = jnp.minimum(n, 1)

    outs = []
    for kv in range(SWA_KV_HEADS):
        heads = range(kv * group, (kv + 1) * group)
        qg = jnp.concatenate([q_t[h * half:(h + 1) * half] for h in heads], axis=1)
        qg = jnp.concatenate([qg, zeros] if kv == 0 else [zeros, qg], axis=0)
        s = jnp.dot(kband, qg, preferred_element_type=jnp.float32) + tab_ref[tab_n, kv]
        sink = sink_ref[kv]
        m = jnp.maximum(jnp.max(s, axis=0, keepdims=True), sink)
        p = jnp.exp2(s - m).astype(dt)
        v1 = jnp.concatenate([v_t[kv * half:(kv + 1) * half], ones], axis=0)
        acc = jnp.dot(v1, p, preferred_element_type=jnp.float32)
        o = acc[0:half] / (acc[half:half + 1] + jnp.exp2(sink - m))
        outs.extend(o[:, g * blk:(g + 1) * blk] for g in range(group))
    o_ref[0] = jnp.concatenate(outs, axis=0).T.astype(o_ref.dtype)


def _swa_in_weights(w_in, dt):
    nq = N_HEADS * HEAD_DIM
    return jnp.concatenate([w_in[:, :nq] * (LOG2E * HEAD_DIM ** -0.5), w_in[:, nq:]], axis=1).astype(dt)


def _swa_attention(proj, sinks, rel_bias):
    b, s, _ = proj.shape
    blk = SWA_BLOCK
    nb = s // blk
    group = N_HEADS // SWA_KV_HEADS
    nq = N_HEADS * HEAD_DIM
    assert s % blk == 0 and SWA_KV_HEADS * HEAD_DIM == LANES and SWA_WINDOW <= blk
    kcol = nq // LANES
    dist = blk + np.arange(blk)[None, :] - np.arange(2 * blk)[:, None]
    window = (dist >= 0) & (dist < SWA_WINDOW)
    first = window & (np.arange(2 * blk)[:, None] >= blk)
    zero = jnp.zeros_like(sinks)

    def wide(valid):
        tab = _bias_table(rel_bias, dist, valid, zero).reshape(SWA_KV_HEADS, group, 2 * blk, blk)
        return jnp.transpose(tab, (0, 2, 1, 3)).reshape(SWA_KV_HEADS, 2 * blk, group * blk) * LOG2E

    tabs = jnp.stack([wide(first), wide(window)])
    sink_w = jnp.broadcast_to(sinks.astype(jnp.float32).reshape(SWA_KV_HEADS, 1, group, 1) * LOG2E,
                              (SWA_KV_HEADS, 1, group, blk)).reshape(SWA_KV_HEADS, 1, group * blk)
    prev = lambda bb, n: (bb, jnp.maximum(n - 1, 0), kcol)
    cur = lambda bb, n: (bb, n, kcol)
    prev_v = lambda bb, n: (bb, jnp.maximum(n - 1, 0), kcol + 1)
    cur_v = lambda bb, n: (bb, n, kcol + 1)
    return pl.pallas_call(
        functools.partial(_swa_kernel, blk=blk, group=group),
        out_shape=jax.ShapeDtypeStruct((b, s, nq), proj.dtype),
        grid=(b, nb),
        in_specs=[pl.BlockSpec((1, blk, nq), lambda bb, n: (bb, n, 0)),
                  pl.BlockSpec((1, blk, LANES), prev),
                  pl.BlockSpec((1, blk, LANES), cur),
                  pl.BlockSpec((1, blk, LANES), prev_v),
                  pl.BlockSpec((1, blk, LANES), cur_v),
                  pl.BlockSpec(tabs.shape, lambda bb, n: (0, 0, 0, 0)),
                  pl.BlockSpec(sink_w.shape, lambda bb, n: (0, 0, 0))],
        out_specs=pl.BlockSpec((1, blk, nq), lambda bb, n: (bb, n, 0)),
        compiler_params=_params(("parallel", "arbitrary")),
        name="swa_attention",
    )(proj, proj, proj, proj, proj, tabs, sink_w)


def _dsa_proj_kernel(x_ref, g_ref, w_ref, wwt_ref, kvn_ref, wuk_ref,
                     ql_ref, c_ref, ct_ref, qi_ref, ki_ref, wt_ref, *, idx_scale):
    dt = w_ref.dtype
    nq = N_HEADS * HEAD_DIM
    r = DSA_KV_RANK
    ni = DSA_IDX_HEADS * DSA_IDX_DIM
    xn = _rmsnorm(x_ref[0], g_ref[...]).astype(dt)
    y = jnp.dot(xn, w_ref[...], preferred_element_type=jnp.float32)
    scale = LOG2E * HEAD_DIM ** -0.5
    for p in range(N_HEADS // 2):
        qp = y[:, p * LANES:(p + 1) * LANES].astype(dt)
        ql = jnp.dot(qp, wuk_ref[p], preferred_element_type=jnp.float32) * scale
        ql_ref[0, 2 * p] = ql[:, :r].astype(ql_ref.dtype)
        ql_ref[0, 2 * p + 1] = ql[:, r:].astype(ql_ref.dtype)
    c = _rmsnorm(y[:, nq:nq + r], kvn_ref[...])
    c_ref[0] = c.astype(c_ref.dtype)
    kt = ct_ref.shape[3]
    for t in range(ct_ref.shape[1]):
        ct_ref[0, t, 0:r, :] = c[t * kt:(t + 1) * kt].T.astype(ct_ref.dtype)
        ct_ref[0, t, r:, :] = jnp.ones((ct_ref.shape[2] - r, kt), ct_ref.dtype)
    qi_ref[0] = y[:, nq + r:nq + r + ni].astype(qi_ref.dtype)
    ki_ref[0] = y[:, nq + r + ni:nq + r + ni + LANES].astype(ki_ref.dtype)
    wt = lax.dot_general(wwt_ref[...], xn, _NT, preferred_element_type=jnp.float32)
    wt_ref[0] = wt * idx_scale


def _dsa_proj(x, g, w_in, kv_norm, w_uk, dt, *, tm=512):
    b, s, d = x.shape
    nq = N_HEADS * HEAD_DIM
    r = DSA_KV_RANK
    ni = DSA_IDX_HEADS * DSA_IDX_DIM
    di = DSA_IDX_DIM
    assert 2 * di == LANES and 2 * HEAD_DIM == LANES and s % tm == 0
    k_idx = w_in[:, nq + r + ni:nq + r + ni + di]
    w_main = jnp.concatenate([w_in[:, :nq + r + ni], k_idx, k_idx], axis=1).astype(dt)
    wwt = w_in[:, nq + r + ni + di:].T.astype(dt)
    uk = jnp.transpose(w_uk, (1, 2, 0)).reshape(N_HEADS // 2, 2, HEAD_DIM, r)
    z = jnp.zeros_like(uk[:, 0])
    wuk_bd = jnp.concatenate([jnp.concatenate([uk[:, 0], z], axis=2),
                              jnp.concatenate([z, uk[:, 1]], axis=2)], axis=1).astype(dt)
    nw = w_main.shape[1]
    idx_scale = DSA_IDX_HEADS ** -0.5 * DSA_IDX_DIM ** -0.5
    kt = DSA_TILE
    rp = r + DSA_ONES_ROWS
    assert tm % kt == 0
    return pl.pallas_call(
        functools.partial(_dsa_proj_kernel, idx_scale=idx_scale),
        out_shape=(jax.ShapeDtypeStruct((b, N_HEADS, s, r), dt),
                   jax.ShapeDtypeStruct((b, s, r), dt),
                   jax.ShapeDtypeStruct((b, s // kt, rp, kt), dt),
                   jax.ShapeDtypeStruct((b, s, ni), dt),
                   jax.ShapeDtypeStruct((b, s, LANES), dt),
                   jax.ShapeDtypeStruct((b, DSA_IDX_HEADS, s), jnp.float32)),
        grid=(b, s // tm),
        in_specs=[pl.BlockSpec((1, tm, d), lambda bb, i: (bb, i, 0)),
                  pl.BlockSpec((1, d), lambda bb, i: (0, 0)),
                  pl.BlockSpec((d, nw), lambda bb, i: (0, 0)),
                  pl.BlockSpec((DSA_IDX_HEADS, d), lambda bb, i: (0, 0)),
                  pl.BlockSpec((1, r), lambda bb, i: (0, 0)),
                  pl.BlockSpec((N_HEADS // 2, LANES, 2 * r), lambda bb, i: (0, 0, 0))],
        out_specs=(pl.BlockSpec((1, N_HEADS, tm, r), lambda bb, i: (bb, 0, i, 0)),
                   pl.BlockSpec((1, tm, r), lambda bb, i: (bb, i, 0)),
                   pl.BlockSpec((1, tm // kt, rp, kt), lambda bb, i: (bb, i, 0, 0)),
                   pl.BlockSpec((1, tm, ni), lambda bb, i: (bb, i, 0)),
                   pl.BlockSpec((1, tm, LANES), lambda bb, i: (bb, i, 0)),
                   pl.BlockSpec((1, DSA_IDX_HEADS, tm), lambda bb, i: (bb, 0, i))),
        compiler_params=_params(("parallel", "parallel")),
        name="dsa_proj",
    )(x, g.reshape(1, d), w_main, wwt, kv_norm.reshape(1, r), wuk_bd)


def _dsa_kernel(qi_ref, wt_ref, ki_ref, c_ref, ct_ref, ql_ref, tdiag_ref, tadj_ref, wuvt_ref, o_ref,
                hi_ref, lo_ref, acc_ref, m_ref, qlt_ref, *, tile, topk, s_len, hg):
    i = pl.program_id(1)
    half = LANES // 2
    nih = DSA_IDX_HEADS

    def tile_rows(j):
        return pl.ds(pl.multiple_of(j * tile, tile), tile)

    dt = qi_ref.dtype
    zeros = jnp.zeros((half, tile), jnp.float32)
    parts = []
    for pair in range(nih // 2):
        pair_t = qi_ref[0, :, pair * LANES:(pair + 1) * LANES].astype(jnp.float32).T
        parts.append(jnp.concatenate([pair_t[:half], zeros], axis=0))
        parts.append(jnp.concatenate([zeros, pair_t[half:]], axis=0))
    qs_t = jnp.concatenate(parts, axis=1).astype(dt)
    w_t = wt_ref[0]
    krow = lax.broadcasted_iota(jnp.int32, (tile, tile), 0)
    qcol = lax.broadcasted_iota(jnp.int32, (tile, tile), 1)

    def index_tile(j, carry):
        rows = tile_rows(j)
        rel = jnp.dot(ki_ref[0, rows, :], qs_t, preferred_element_type=jnp.float32)
        sc = jnp.zeros((tile, tile), jnp.float32)
        for h in range(nih):
            sc = sc + jnp.maximum(rel[:, h * tile:(h + 1) * tile], 0.0) * w_t[h:h + 1, :]
        sc = jnp.where(sc == 0.0, 0.0, sc)
        bits = lax.bitcast_convert_type(sc, jnp.int32)
        u = jnp.where(bits < 0, bits ^ jnp.int32(0x7FFFFFFF), bits)
        causal = (j * tile + krow) <= (i * tile + qcol)
        u = jnp.where(causal, u, jnp.int32(INT_MIN))
        hi_ref[rows, :] = lax.shift_right_arithmetic(u, I16_BITS).astype(jnp.int16)
        lo_ref[rows, :] = ((u & jnp.int32(I16_SPAN - 1)) - I16_SPAN // 2).astype(jnp.int16)
        return carry

    lax.fori_loop(0, i + 1, index_tile, 0)

    cu = DSA_COUNT_UNROLL
    i16_min = jnp.int16(-I16_SPAN // 2)
    for k in range(1, cu):
        @pl.when(i + k < s_len // tile)
        def _(k=k):
            hi_ref[tile_rows(i + k), :] = jnp.full((tile, tile), i16_min, jnp.int16)
            lo_ref[tile_rows(i + k), :] = jnp.full((tile, tile), i16_min, jnp.int16)

    def count(pred):
        def body(jg, acc):
            for k in range(cu):
                j = jg * cu + k
                ones = jnp.where(pred(j, tile_rows(j)), jnp.int16(1), jnp.int16(0))
                for r in range(0, tile, PACKED_SUBLANES):
                    acc = acc + ones[r:r + PACKED_SUBLANES]
            return acc
        acc = lax.fori_loop(0, (i + cu) // cu, body, jnp.zeros((PACKED_SUBLANES, tile), jnp.int16))
        return jnp.sum(acc.astype(jnp.float32), axis=0, keepdims=True)

    def to_i16(x):
        return x.astype(jnp.int16)

    few = (i * tile + lax.broadcasted_iota(jnp.int32, (1, tile), 1)) < topk

    def search16(ref, target):
        def unsettled(carry):
            bi, _, cnt_ans = carry
            pending = jnp.logical_not(few) & (cnt_ans != target)
            return (bi < I16_BITS) & (jnp.max(jnp.where(pending, 1.0, 0.0)) > 0.0)

        def bit_steps(carry):
            bi, ans, cnt_ans = carry
            for k in range(DSA_BITS_PER_CHECK):
                cand = ans | lax.shift_left(jnp.int32(1), I16_BITS - 1 - (bi + k))
                t16 = to_i16(cand - I16_SPAN // 2)
                cnt = count(lambda j, rows, t16=t16: ref[rows, :] >= t16)
                ok = cnt >= target
                ans, cnt_ans = jnp.where(ok, cand, ans), jnp.where(ok, cnt, cnt_ans)
            return bi + DSA_BITS_PER_CHECK, ans, cnt_ans

        _, ans, cnt = lax.while_loop(unsettled, bit_steps,
                                     (jnp.int32(0), jnp.zeros((1, tile), jnp.int32),
                                      jnp.zeros((1, tile), jnp.float32)))
        return ans - I16_SPAN // 2, cnt

    def count_above(ref, t):
        cnt = count(lambda j, rows: ref[rows, :] >= to_i16(jnp.minimum(t + 1, I16_SPAN // 2 - 1)))
        return jnp.where(t + 1 > I16_SPAN // 2 - 1, 0.0, cnt)

    topk_f = jnp.full((1, tile), float(topk), jnp.float32)
    thr_hi, cnt_hi = search16(hi_ref, topk_f)
    cnt_above = count_above(hi_ref, thr_hi)
    need_lo = topk_f - cnt_above
    h16 = to_i16(thr_hi)

    def keep_low(jg, carry):
        for k in range(cu):
            rows = tile_rows(jg * cu + k)
            lo_ref[rows, :] = jnp.where(hi_ref[rows, :] == h16, lo_ref[rows, :], i16_min)
        return carry

    lax.fori_loop(0, (i + cu) // cu, keep_low, 0)
    thr_lo, cnt_lo = search16(lo_ref, need_lo)
    cnt_lo = jnp.where(cnt_lo == 0.0, cnt_hi - cnt_above, cnt_lo)

    nbits = int(s_len).bit_length()
    cut_all = jnp.full((1, tile), 2 ** nbits - 1, jnp.int32)
    krow16 = lax.broadcasted_iota(jnp.int16, (tile, tile), 0)
    l16 = to_i16(thr_lo)

    def tie_search():
        need = need_lo - count_above(lo_ref, thr_lo)

        def cut_step(bi, cut):
            cand = cut | lax.shift_left(jnp.int32(1), nbits - 1 - bi)
            cnt = count(lambda j, rows: (lo_ref[rows, :] == l16) & (hi_ref[rows, :] == h16)
                        & (krow16 < to_i16(jnp.minimum(cand - j * tile, I16_SPAN // 2 - 1))))
            return jnp.where(cnt <= need, cand, cut)

        return lax.fori_loop(0, nbits, cut_step, jnp.zeros((1, tile), jnp.int32))

    tie = jnp.logical_not(few) & (cnt_lo > need_lo)
    cut = lax.cond(jnp.max(jnp.where(tie, 1.0, 0.0)) > 0.0, tie_search, lambda: cut_all)
    l16_sel = to_i16(jnp.where(thr_hi == -I16_SPAN // 2, I16_SPAN // 2 - 1, thr_lo))
    mask_0 = jnp.zeros((), c_ref.dtype)
    mask_neg = jnp.asarray(NEG, c_ref.dtype)

    m_ref[...] = jnp.full(m_ref.shape, NEG, jnp.float32)
    acc_ref[...] = jnp.zeros(acc_ref.shape, jnp.float32)
    for h in range(N_HEADS):
        qlt_ref[h // hg, :, (h % hg) * tile:(h % hg + 1) * tile] = (
            ql_ref[0, h].astype(jnp.float32).T.astype(qlt_ref.dtype))

    def attend_tile(j, table_ref, pen):
        rows = tile_rows(j)
        hi = hi_ref[rows, :]
        lo = lo_ref[rows, :]
        cut16 = to_i16(jnp.clip(cut - j * tile, -I16_SPAN // 2, I16_SPAN // 2 - 1))
        tie_m = jnp.where(krow16 < cut16, mask_0, mask_neg)
        low_m = jnp.where(lo > l16_sel, mask_0, jnp.where(lo == l16_sel, tie_m, mask_neg))
        mask16 = jnp.where(hi > h16, mask_0, jnp.where(hi == h16, low_m, mask_neg))
        maskadd = mask16.astype(jnp.float32) + pen
        c_t = c_ref[0, rows, :]
        ct_t = ct_ref[0, j]

        for g in range(N_HEADS // hg):
            logits = jnp.dot(c_t, qlt_ref[g], preferred_element_type=jnp.float32)
            for hl in range(hg):
                cols = slice(hl * tile, (hl + 1) * tile)
                s = logits[:, cols] + maskadd
                if table_ref is not None:
                    s = s + table_ref[g, :, cols]
                m_old = m_ref[g, :, cols]
                m_new = jnp.maximum(m_old, jnp.max(s, axis=0, keepdims=True))
                p = jnp.exp2(s - m_new).astype(ct_t.dtype)
                pv = jnp.dot(ct_t, p, preferred_element_type=jnp.float32)
                acc_ref[g, :, cols] = jnp.exp2(m_old - m_new) * acc_ref[g, :, cols] + pv
                m_ref[g, :, cols] = m_new

    def far(j, carry):
        attend_tile(j, None, 0.0)
        return carry

    lax.fori_loop(0, i - 1, far, 0)
    attend_tile(jnp.maximum(i - 1, 0), tadj_ref, jnp.where(i >= 1, 0.0, NEG))
    attend_tile(i, tdiag_ref, 0.0)

    rank = wuvt_ref.shape[2]
    parts = []
    for h in range(N_HEADS):
        g, cols = h // hg, slice((h % hg) * tile, (h % hg + 1) * tile)
        ol = (acc_ref[g, 0:rank, cols] / acc_ref[g, rank:rank + 1, cols]).astype(wuvt_ref.dtype)
        parts.append(jnp.dot(wuvt_ref[h], ol, preferred_element_type=jnp.float32))
    o_ref[0] = jnp.concatenate(parts, axis=0).T.astype(o_ref.dtype)


def _dsa_attention(ql, c, ct, qi, ki2, wt, w_uv, rel_bias):
    b, s, r = c.shape
    tile = DSA_TILE
    nt = s // tile
    topk = min(DSA_TOPK_MAX, s // 4)
    assert s % (tile * DSA_COUNT_UNROLL) == 0 and ct.shape[3] == tile and s < I16_SPAN // 2
    rp = ct.shape[2]
    wuvt = jnp.transpose(w_uv, (1, 2, 0)).astype(c.dtype)
    kk = np.arange(tile)[:, None]
    qq = np.arange(tile)[None, :]
    shift = rel_bias[REL_BUCKETS - 1]
    hg = DSA_HEAD_GROUP
    ng = N_HEADS // hg

    def wide(tab):
        return jnp.transpose(tab.reshape(ng, hg, tile, tile), (0, 2, 1, 3)).reshape(ng, tile, hg * tile)

    t_diag = wide(_bias_table(rel_bias, qq - kk, qq >= kk, shift)) * LOG2E
    t_adj = wide(_bias_table(rel_bias, tile + qq - kk, np.ones((tile, tile), bool), shift)) * LOG2E
    ni = qi.shape[2]
    once = pl.Buffered(1)
    return pl.pallas_call(
        functools.partial(_dsa_kernel, tile=tile, topk=topk, s_len=s, hg=hg),
        out_shape=jax.ShapeDtypeStruct((b, s, N_HEADS * HEAD_DIM), c.dtype),
        grid=(b, nt),
        in_specs=[pl.BlockSpec((1, tile, ni), lambda bb, i: (bb, i, 0)),
                  pl.BlockSpec((1, DSA_IDX_HEADS, tile), lambda bb, i: (bb, 0, i)),
                  pl.BlockSpec((1, s, LANES), lambda bb, i: (bb, 0, 0), pipeline_mode=once),
                  pl.BlockSpec((1, s, r), lambda bb, i: (bb, 0, 0), pipeline_mode=once),
                  pl.BlockSpec((1, nt, rp, tile), lambda bb, i: (bb, 0, 0, 0), pipeline_mode=once),
                  pl.BlockSpec((1, N_HEADS, tile, r), lambda bb, i: (bb, 0, i, 0)),
                  pl.BlockSpec((ng, tile, hg * tile), lambda bb, i: (0, 0, 0), pipeline_mode=once),
                  pl.BlockSpec((ng, tile, hg * tile), lambda bb, i: (0, 0, 0), pipeline_mode=once),
                  pl.BlockSpec((N_HEADS, HEAD_DIM, r), lambda bb, i: (0, 0, 0), pipeline_mode=once)],
        out_specs=pl.BlockSpec((1, tile, N_HEADS * HEAD_DIM), lambda bb, i: (bb, i, 0)),
        scratch_shapes=[pltpu.VMEM((s, tile), jnp.int16),
                        pltpu.VMEM((s, tile), jnp.int16),
                        pltpu.VMEM((ng, rp, hg * tile), jnp.float32),
                        pltpu.VMEM((ng, 1, hg * tile), jnp.float32),
                        pltpu.VMEM((ng, r, hg * tile), c.dtype)],
        compiler_params=_params(("parallel", "arbitrary")),
        name="dsa_attention",
    )(qi, wt, ki2, c, ct, ql, t_diag, t_adj, wuvt)


def _dsa_mixer(h, g, w_in, kv_norm, w_uk, w_uv, rel_bias, mxu_dtype):
    ql, c, ct, qi, ki2, wt = _dsa_proj(h, g, w_in, kv_norm, w_uk, mxu_dtype)
    return _dsa_attention(ql, c, ct, qi, ki2, wt, w_uv, rel_bias)


def kernel(x, rel_bias, norm_mix, norm_mlp, mlp_up, mlp_down, a_w_in, a_w_out, b_w_in, b_sinks, b_w_out,
           c_w_in, c_kv_norm, c_w_uk, c_w_uv, c_w_out, final_norm):
    b, s, d = x.shape
    t = b * s
    dt = MXU_DTYPE
    depth = norm_mix.shape[0]
    h = x.reshape(t, d)
    for i in range(depth):
        kind, j = i % 3, i // 3
        if kind == 0:
            qkv = _norm_proj(h, norm_mix[i], _moba_in_weights(a_w_in[j], dt)).reshape(b, s, -1)
            o, w_out = _moba_attention(qkv, rel_bias), a_w_out[j]
        elif kind == 1:
            proj = _norm_proj(h, norm_mix[i], _swa_in_weights(b_w_in[j], dt)).reshape(b, s, -1)
            o, w_out = _swa_attention(proj, b_sinks[j], rel_bias), b_w_out[j]
        else:
            o = _dsa_mixer(h.reshape(b, s, d), norm_mix[i], c_w_in[j], c_kv_norm[j], c_w_uk[j],
                           c_w_uv[j], rel_bias, dt)
            w_out = c_w_out[j]
        h = _out_proj_mlp(o.reshape(t, -1), w_out.astype(dt), h, norm_mlp[i], mlp_up[i].astype(dt),
                          mlp_down[i].astype(dt), final_norm, final_norm=(i == depth - 1))
    return h.reshape(b, s, d)
```

```python
import functools
import math

import numpy as np
import jax
import jax.numpy as jnp
from jax import lax
from jax.experimental import pallas as pl
from jax.experimental.pallas import tpu as pltpu

LANES = 128
PACKED_SUBLANES = 16
VMEM_LIMIT_BYTES = 56 * 1024 * 1024
MXU_DTYPE = jnp.bfloat16
NEG = -1e30
LOG2E = math.log2(math.e)
INT_MIN = -2 ** 31
I16_BITS = 16
I16_SPAN = 2 ** I16_BITS

N_HEADS = 16
HEAD_DIM = 64
NORM_EPS = 1e-6
REL_BUCKETS = 32
REL_MAX_DIST = 128
MOBA_BLOCK = 256
MOBA_TOPK = 3
MOBA_GROUP = 4
MOBA_KEY_UNIT = 128
SWA_WINDOW = 128
SWA_BLOCK = 128
SWA_KV_HEADS = 2
DSA_KV_RANK = 256
DSA_IDX_HEADS = 8
DSA_IDX_DIM = 64
DSA_TOPK_MAX = 256
DSA_TILE = 256
DSA_HEAD_GROUP = 8
DSA_COUNT_UNROLL = 4
DSA_BITS_PER_CHECK = 4
MOBA_ONES_ROWS = SWA_ONES_ROWS = DSA_ONES_ROWS = PACKED_SUBLANES

_NT = (((1,), (1,)), ((), ()))


def _params(semantics):
    return pltpu.CompilerParams(dimension_semantics=semantics, vmem_limit_bytes=VMEM_LIMIT_BYTES)


def _rmsnorm(x, g):
    var = jnp.mean(x * x, axis=-1, keepdims=True)
    return x * lax.rsqrt(var + NORM_EPS) * g


def _norm_proj_kernel(x_ref, g_ref, w_ref, o_ref):
    xn = _rmsnorm(x_ref[...], g_ref[...]).astype(w_ref.dtype)
    o_ref[...] = jnp.dot(xn, w_ref[...], preferred_element_type=jnp.float32).astype(o_ref.dtype)


def _norm_proj(x, g, w, *, tm=512):
    t, d = x.shape
    n = w.shape[1]
    assert t % tm == 0
    return pl.pallas_call(
        _norm_proj_kernel,
        out_shape=jax.ShapeDtypeStruct((t, n), w.dtype),
        grid=(t // tm,),
        in_specs=[pl.BlockSpec((tm, d), lambda i: (i, 0)),
                  pl.BlockSpec((1, d), lambda i: (0, 0)),
                  pl.BlockSpec((d, n), lambda i: (0, 0))],
        out_specs=pl.BlockSpec((tm, n), lambda i: (i, 0)),
        compiler_params=_params(("parallel",)),
        name="norm_proj",
    )(x, g.reshape(1, d), w)


def _mlp_kernel(a_ref, wo_ref, r_ref, g_ref, wu_ref, wd_ref, gf_ref, o_ref, xn_ref, acc_ref,
                *, final_norm):
    f = pl.program_id(1)

    @pl.when(f == 0)
    def _():
        x = r_ref[...] + jnp.dot(a_ref[...], wo_ref[...], preferred_element_type=jnp.float32)
        xn_ref[...] = _rmsnorm(x, g_ref[...]).astype(xn_ref.dtype)
        acc_ref[...] = x

    u = jnp.dot(xn_ref[...], wu_ref[...], preferred_element_type=jnp.float32)
    a = jnp.square(jnp.maximum(u, 0.0)).astype(wd_ref.dtype)
    acc_ref[...] += jnp.dot(a, wd_ref[...], preferred_element_type=jnp.float32)

    @pl.when(f == pl.num_programs(1) - 1)
    def _():
        y = acc_ref[...]
        if final_norm:
            y = _rmsnorm(y, gf_ref[...])
        o_ref[...] = y


def _out_proj_mlp(a, w_out, res, g, w_up, w_down, g_final, *, final_norm, tm=1024, tf=1024):
    t, d = res.shape
    k = a.shape[1]
    ff = w_up.shape[1]
    assert t % tm == 0 and ff % tf == 0
    return pl.pallas_call(
        functools.partial(_mlp_kernel, final_norm=final_norm),
        out_shape=jax.ShapeDtypeStruct((t, d), jnp.float32),
        grid=(t // tm, ff // tf),
        in_specs=[pl.BlockSpec((tm, k), lambda i, f: (i, 0)),
                  pl.BlockSpec((k, d), lambda i, f: (0, 0)),
                  pl.BlockSpec((tm, d), lambda i, f: (i, 0)),
                  pl.BlockSpec((1, d), lambda i, f: (0, 0)),
                  pl.BlockSpec((d, tf), lambda i, f: (0, f)),
                  pl.BlockSpec((tf, d), lambda i, f: (f, 0)),
                  pl.BlockSpec((1, d), lambda i, f: (0, 0))],
        out_specs=pl.BlockSpec((tm, d), lambda i, f: (i, 0)),
        scratch_shapes=[pltpu.VMEM((tm, d), w_up.dtype), pltpu.VMEM((tm, d), jnp.float32)],
        compiler_params=_params(("parallel", "arbitrary")),
        name="out_proj_mlp",
    )(a, w_out, res, g.reshape(1, d), w_up, w_down, g_final.reshape(1, d))


def _rel_bucket_np(dist):
    n = np.maximum(dist, 0)
    max_exact = REL_BUCKETS // 2
    nf = np.maximum(n, 1).astype(np.float64)
    large = max_exact + (np.log(nf / max_exact) / math.log(REL_MAX_DIST / max_exact)
                         * (REL_BUCKETS - max_exact)).astype(np.int64)
    large = np.minimum(large, REL_BUCKETS - 1)
    return np.where(n < max_exact, n, large).astype(np.int32)


def _bias_table(rel_bias, dist, valid, shift):
    bucket = jnp.asarray(_rel_bucket_np(dist).reshape(1, -1))
    onehot = (bucket == jnp.arange(REL_BUCKETS, dtype=bucket.dtype).reshape(-1, 1)).astype(jnp.float32)
    tab = jnp.einsum("bh,bn->hn", rel_bias, onehot, precision=lax.Precision.HIGHEST)
    tab = tab.reshape((-1,) + dist.shape) - shift.reshape((-1,) + (1,) * dist.ndim)
    return jnp.where(jnp.asarray(valid)[None], tab, NEG).astype(jnp.float32)


def _moba_route_mask(gate_t, i):
    blkid = lax.broadcasted_iota(jnp.int32, gate_t.shape, 0)
    blk_f = blkid.astype(jnp.float32)
    valid = blkid < i
    g = jnp.where(valid, gate_t, -jnp.inf)
    sel = blkid == i
    for _ in range(MOBA_TOPK):
        mx = jnp.max(g, axis=0, keepdims=True)
        first = jnp.min(jnp.where(g == mx, blk_f, float(LANES)), axis=0, keepdims=True)
        pick = (blk_f == first) & valid
        sel = sel | pick
        g = jnp.where(pick, -jnp.inf, g)
    return jnp.where(sel, 0.0, NEG)


def _moba_kernel(q_ref, k_ref, v_ref, tnear_ref, o_ref, kaug_ref, vt_ref, kmrows_ref,
                 *, blk, nblk, group, ku):
    i = pl.program_id(2)
    half = LANES // 2
    pad = group * blk
    lane = lax.broadcasted_iota(jnp.int32, (blk, LANES), 1)
    lo = lane < half

    @pl.when(i == 0)
    def _():
        kmrows_ref[...] = jnp.zeros(kmrows_ref.shape, kmrows_ref.dtype)
        lane1 = lax.broadcasted_iota(jnp.int32, (1, LANES), 1)
        lane_p = lax.broadcasted_iota(jnp.int32, (pad, LANES), 1)
        kaug_ref[0, 0:pad, :] = jnp.where(lane_p == LANES - 1, 1.0, 0.0).astype(kaug_ref.dtype)
        kaug_ref[1, 0:pad, :] = jnp.where(lane_p == half - 1, 1.0, 0.0).astype(kaug_ref.dtype)
        vt_ref[0:pad // ku] = jnp.zeros((pad // ku,) + vt_ref.shape[1:], vt_ref.dtype)
        ones = jnp.ones((vt_ref.shape[2] - half, ku), vt_ref.dtype)

        def build(n, carry):
            rows = pl.ds(pl.multiple_of(n * blk, blk), blk)
            prows = pl.ds(pl.multiple_of(n * blk + pad, blk), blk)
            kn = k_ref[0, rows, :].astype(jnp.float32)
            kaug_ref[0, prows, :] = jnp.where(
                lo, kn, jnp.where(lane == half + n, 1.0, 0.0)).astype(kaug_ref.dtype)
            kaug_ref[1, prows, :] = jnp.where(
                lo, jnp.where(lane == n, 1.0, 0.0), kn).astype(kaug_ref.dtype)
            for t in range(blk // ku):
                vn = v_ref[0, pl.ds(pl.multiple_of(n * blk + t * ku, ku), ku), :]
                vn_t = vn.astype(jnp.float32).T.astype(vt_ref.dtype)
                unit = (n + group) * (blk // ku) + t
                for hh in range(2):
                    vt_ref[unit, hh, 0:half, :] = vn_t[hh * half:(hh + 1) * half]
                    vt_ref[unit, hh, half:, :] = ones
            mean = jnp.sum(kn, axis=0, keepdims=True) * (1.0 / blk)
            kmrows_ref[0, pl.ds(n, 1), :] = jnp.where(lane1 < half, mean, 0.0)
            kmrows_ref[1, pl.ds(n, 1), :] = jnp.where(lane1 < half, 0.0, mean)
            return carry

        lax.fori_loop(0, nblk, build, 0)

    dt = q_ref.dtype
    q_t = q_ref[0].astype(jnp.float32).T
    q_tb = q_t.astype(dt)
    q_aug = []
    for hh in range(2):
        gate_t = jnp.dot(kmrows_ref[hh].astype(dt), q_tb,
                         preferred_element_type=jnp.float32)
        route_t = _moba_route_mask(gate_t, i)
        q_h = q_t[hh * half:(hh + 1) * half]
        q_aug.append(jnp.concatenate([q_h, route_t] if hh == 0 else [route_t, q_h], axis=0).astype(dt))

    upb = blk // ku

    def fold_group(unit0, state, bias_ref=None):
        n_units = group * upb
        rows = pl.ds(pl.multiple_of(unit0 * ku, ku), n_units * ku)
        s = [jnp.dot(kaug_ref[hh, rows, :], q_aug[hh], preferred_element_type=jnp.float32)
             for hh in range(2)]
        state = list(state)
        unbiased = n_units if bias_ref is None else n_units - bias_ref.shape[1] // ku
        for t in range(n_units):
            for hh in range(2):
                m, acc = state[hh]
                s_t = s[hh][t * ku:(t + 1) * ku]
                if t >= unbiased:
                    s_t = s_t + bias_ref[hh, (t - unbiased) * ku:(t - unbiased + 1) * ku, :]
                m_new = jnp.maximum(m, jnp.max(s_t, axis=0, keepdims=True))
                p = jnp.exp2(s_t - m_new).astype(dt)
                acc = jnp.exp2(m - m_new) * acc + jnp.dot(vt_ref[unit0 + t, hh], p,
                                                          preferred_element_type=jnp.float32)
                state[hh] = (m_new, acc)
        return state

    start = (jnp.full((1, blk), NEG, jnp.float32), jnp.zeros((vt_ref.shape[2], blk), jnp.float32))
    state = fold_group((i + 1) * upb, [start, start], tnear_ref)

    def far(g, carry):
        st = fold_group((i + 1 - group * g) * upb, [carry[0:2], carry[2:4]])
        return tuple(st[0]) + tuple(st[1])

    out = lax.fori_loop(1, (i + group) // group, far, tuple(state[0]) + tuple(state[1]))
    o_t = jnp.concatenate([out[1][:half] / out[1][half:half + 1],
                           out[3][:half] / out[3][half:half + 1]], axis=0)
    o_ref[0] = o_t.T.astype(o_ref.dtype)


def _moba_in_weights(w_in, dt):
    nq = N_HEADS * HEAD_DIM
    return jnp.concatenate([w_in[:, :nq] * (LOG2E * HEAD_DIM ** -0.5), w_in[:, nq:]], axis=1).astype(dt)


def _moba_attention(qkv, rel_bias):
    b, s, _ = qkv.shape
    blk = MOBA_BLOCK
    group = MOBA_GROUP
    nblk = s // blk
    hp = N_HEADS // 2
    ku = MOBA_KEY_UNIT
    assert s % blk == 0 and nblk < LANES // 2 and 2 * HEAD_DIM == LANES and blk % ku == 0
    assert group >= 2
    kk = np.arange(blk)[:, None]
    qq = np.arange(blk)[None, :]
    shift = rel_bias[REL_BUCKETS - 1]
    t_own = _bias_table(rel_bias, qq - kk, qq >= kk, shift)
    t_adj = _bias_table(rel_bias, blk + qq - kk, np.ones((blk, blk), bool), shift)
    t_near = jnp.concatenate([t_adj, t_own], axis=1) * LOG2E
    return pl.pallas_call(
        functools.partial(_moba_kernel, blk=blk, nblk=nblk, group=group, ku=ku),
        out_shape=jax.ShapeDtypeStruct((b, s, N_HEADS * HEAD_DIM), qkv.dtype),
        grid=(b, hp, nblk),
        in_specs=[pl.BlockSpec((1, blk, LANES), lambda bb, p, i: (bb, i, p)),
                  pl.BlockSpec((1, s, LANES), lambda bb, p, i: (bb, 0, hp + p)),
                  pl.BlockSpec((1, s, LANES), lambda bb, p, i: (bb, 0, 2 * hp + p)),
                  pl.BlockSpec((2, 2 * blk, blk), lambda bb, p, i: (p, 0, 0))],
        out_specs=pl.BlockSpec((1, blk, LANES), lambda bb, p, i: (bb, i, p)),
        scratch_shapes=[pltpu.VMEM((2, s + group * blk, LANES), qkv.dtype),
                        pltpu.VMEM(((s + group * blk) // ku, 2, HEAD_DIM + MOBA_ONES_ROWS, ku), qkv.dtype),
                        pltpu.VMEM((2, LANES // 2, LANES), jnp.float32)],
        compiler_params=_params(("parallel", "parallel", "arbitrary")),
        name="moba_attention",
    )(qkv, qkv, qkv, t_near)


def _swa_kernel(q_ref, kp_ref, kc_ref, vp_ref, vc_ref, tab_ref, sink_ref, o_ref, *, blk, group):
    n = pl.program_id(1)
    half = LANES // 2
    dt = q_ref.dtype
    q_t = q_ref[0].astype(jnp.float32).T.astype(dt)
    kband = jnp.concatenate([kp_ref[0], kc_ref[0]], axis=0)
    vband = jnp.concatenate([vp_ref[0], vc_ref[0]], axis=0)
    v_t = vband.astype(jnp.float32).T.astype(dt)
    ones = jnp.ones((SWA_ONES_ROWS, 2 * blk), dt)
    zeros = jnp.zeros((half, group * blk), dt)
    tab_n = jnp.minimum(n, 1)

    outs = []
    for kv in range(SWA_KV_HEADS):
        heads = range(kv * group, (kv + 1) * group)
        qg = jnp.concatenate([q_t[h * half:(h + 1) * half] for h in heads], axis=1)
        qg = jnp.concatenate([qg, zeros] if kv == 0 else [zeros, qg], axis=0)
        s = jnp.dot(kband, qg, preferred_element_type=jnp.float32) + tab_ref[tab_n, kv]
        sink = sink_ref[kv]
        m = jnp.maximum(jnp.max(s, axis=0, keepdims=True), sink)
        p = jnp.exp2(s - m).astype(dt)
        v1 = jnp.concatenate([v_t[kv * half:(kv + 1) * half], ones], axis=0)
        acc = jnp.dot(v1, p, preferred_element_type=jnp.float32)
        o = acc[0:half] / (acc[half:half + 1] + jnp.exp2(sink - m))
        outs.extend(o[:, g * blk:(g + 1) * blk] for g in range(group))
    o_ref[0] = jnp.concatenate(outs, axis=0).T.astype(o_ref.dtype)


def _swa_in_weights(w_in, dt):
    nq = N_HEADS * HEAD_DIM
    return jnp.concatenate([w_in[:, :nq] * (LOG2E * HEAD_DIM ** -0.5), w_in[:, nq:]], axis=1).astype(dt)


def _swa_attention(proj, sinks, rel_bias):
    b, s, _ = proj.shape
    blk = SWA_BLOCK
    nb = s // blk
    group = N_HEADS // SWA_KV_HEADS
    nq = N_HEADS * HEAD_DIM
    assert s % blk == 0 and SWA_KV_HEADS * HEAD_DIM == LANES and SWA_WINDOW <= blk
    kcol = nq // LANES
    dist = blk + np.arange(blk)[None, :] - np.arange(2 * blk)[:, None]
    window = (dist >= 0) & (dist < SWA_WINDOW)
    first = window & (np.arange(2 * blk)[:, None] >= blk)
    zero = jnp.zeros_like(sinks)

    def wide(valid):
        tab = _bias_table(rel_bias, dist, valid, zero).reshape(SWA_KV_HEADS, group, 2 * blk, blk)
        return jnp.transpose(tab, (0, 2, 1, 3)).reshape(SWA_KV_HEADS, 2 * blk, group * blk) * LOG2E

    tabs = jnp.stack([wide(first), wide(window)])
    sink_w = jnp.broadcast_to(sinks.astype(jnp.float32).reshape(SWA_KV_HEADS, 1, group, 1) * LOG2E,
                              (SWA_KV_HEADS, 1, group, blk)).reshape(SWA_KV_HEADS, 1, group * blk)
    prev = lambda bb, n: (bb, jnp.maximum(n - 1, 0), kcol)
    cur = lambda bb, n: (bb, n, kcol)
    prev_v = lambda bb, n: (bb, jnp.maximum(n - 1, 0), kcol + 1)
    cur_v = lambda bb, n: (bb, n, kcol + 1)
    return pl.pallas_call(
        functools.partial(_swa_kernel, blk=blk, group=group),
        out_shape=jax.ShapeDtypeStruct((b, s, nq), proj.dtype),
        grid=(b, nb),
        in_specs=[pl.BlockSpec((1, blk, nq), lambda bb, n: (bb, n, 0)),
                  pl.BlockSpec((1, blk, LANES), prev),
                  pl.BlockSpec((1, blk, LANES), cur),
                  pl.BlockSpec((1, blk, LANES), prev_v),
                  pl.BlockSpec((1, blk, LANES), cur_v),
                  pl.BlockSpec(tabs.shape, lambda bb, n: (0, 0, 0, 0)),
                  pl.BlockSpec(sink_w.shape, lambda bb, n: (0, 0, 0))],
        out_specs=pl.BlockSpec((1, blk, nq), lambda bb, n: (bb, n, 0)),
        compiler_params=_params(("parallel", "arbitrary")),
        name="swa_attention",
    )(proj, proj, proj, proj, proj, tabs, sink_w)


def _dsa_proj_kernel(x_ref, g_ref, w_ref, wwt_ref, kvn_ref, wuk_ref,
                     ql_ref, c_ref, ct_ref, qi_ref, ki_ref, wt_ref, *, idx_scale):
    dt = w_ref.dtype
    nq = N_HEADS * HEAD_DIM
    r = DSA_KV_RANK
    ni = DSA_IDX_HEADS * DSA_IDX_DIM
    xn = _rmsnorm(x_ref[0], g_ref[...]).astype(dt)
    y = jnp.dot(xn, w_ref[...], preferred_element_type=jnp.float32)
    scale = LOG2E * HEAD_DIM ** -0.5
    for p in range(N_HEADS // 2):
        qp = y[:, p * LANES:(p + 1) * LANES].astype(dt)
        ql = jnp.dot(qp, wuk_ref[p], preferred_element_type=jnp.float32) * scale
        ql_ref[0, 2 * p] = ql[:, :r].astype(ql_ref.dtype)
        ql_ref[0, 2 * p + 1] = ql[:, r:].astype(ql_ref.dtype)
    c = _rmsnorm(y[:, nq:nq + r], kvn_ref[...])
    c_ref[0] = c.astype(c_ref.dtype)
    kt = ct_ref.shape[3]
    for t in range(ct_ref.shape[1]):
        ct_ref[0, t, 0:r, :] = c[t * kt:(t + 1) * kt].T.astype(ct_ref.dtype)
        ct_ref[0, t, r:, :] = jnp.ones((ct_ref.shape[2] - r, kt), ct_ref.dtype)
    qi_ref[0] = y[:, nq + r:nq + r + ni].astype(qi_ref.dtype)
    ki_ref[0] = y[:, nq + r + ni:nq + r + ni + LANES].astype(ki_ref.dtype)
    wt = lax.dot_general(wwt_ref[...], xn, _NT, preferred_element_type=jnp.float32)
    wt_ref[0] = wt * idx_scale


def _dsa_proj(x, g, w_in, kv_norm, w_uk, dt, *, tm=512):
    b, s, d = x.shape
    nq = N_HEADS * HEAD_DIM
    r = DSA_KV_RANK
    ni = DSA_IDX_HEADS * DSA_IDX_DIM
    di = DSA_IDX_DIM
    assert 2 * di == LANES and 2 * HEAD_DIM == LANES and s % tm == 0
    k_idx = w_in[:, nq + r + ni:nq + r + ni + di]
    w_main = jnp.concatenate([w_in[:, :nq + r + ni], k_idx, k_idx], axis=1).astype(dt)
    wwt = w_in[:, nq + r + ni + di:].T.astype(dt)
    uk = jnp.transpose(w_uk, (1, 2, 0)).reshape(N_HEADS // 2, 2, HEAD_DIM, r)
    z = jnp.zeros_like(uk[:, 0])
    wuk_bd = jnp.concatenate([jnp.concatenate([uk[:, 0], z], axis=2),
                              jnp.concatenate([z, uk[:, 1]], axis=2)], axis=1).astype(dt)
    nw = w_main.shape[1]
    idx_scale = DSA_IDX_HEADS ** -0.5 * DSA_IDX_DIM ** -0.5
    kt = DSA_TILE
    rp = r + DSA_ONES_ROWS
    assert tm % kt == 0
    return pl.pallas_call(
        functools.partial(_dsa_proj_kernel, idx_scale=idx_scale),
        out_shape=(jax.ShapeDtypeStruct((b, N_HEADS, s, r), dt),
                   jax.ShapeDtypeStruct((b, s, r), dt),
                   jax.ShapeDtypeStruct((b, s // kt, rp, kt), dt),
                   jax.ShapeDtypeStruct((b, s, ni), dt),
                   jax.ShapeDtypeStruct((b, s, LANES), dt),
                   jax.ShapeDtypeStruct((b, DSA_IDX_HEADS, s), jnp.float32)),
        grid=(b, s // tm),
        in_specs=[pl.BlockSpec((1, tm, d), lambda bb, i: (bb, i, 0)),
                  pl.BlockSpec((1, d), lambda bb, i: (0, 0)),
                  pl.BlockSpec((d, nw), lambda bb, i: (0, 0)),
                  pl.BlockSpec((DSA_IDX_HEADS, d), lambda bb, i: (0, 0)),
                  pl.BlockSpec((1, r), lambda bb, i: (0, 0)),
                  pl.BlockSpec((N_HEADS // 2, LANES, 2 * r), lambda bb, i: (0, 0, 0))],
        out_specs=(pl.BlockSpec((1, N_HEADS, tm, r), lambda bb, i: (bb, 0, i, 0)),
                   pl.BlockSpec((1, tm, r), lambda bb, i: (bb, i, 0)),
                   pl.BlockSpec((1, tm // kt, rp, kt), lambda bb, i: (bb, i, 0, 0)),
                   pl.BlockSpec((1, tm, ni), lambda bb, i: (bb, i, 0)),
                   pl.BlockSpec((1, tm, LANES), lambda bb, i: (bb, i, 0)),
                   pl.BlockSpec((1, DSA_IDX_HEADS, tm), lambda bb, i: (bb, 0, i))),
        compiler_params=_params(("parallel", "parallel")),
        name="dsa_proj",
    )(x, g.reshape(1, d), w_main, wwt, kv_norm.reshape(1, r), wuk_bd)


def _dsa_kernel(qi_ref, wt_ref, ki_ref, c_ref, ct_ref, ql_ref, tdiag_ref, tadj_ref, wuvt_ref, o_ref,
                hi_ref, lo_ref, acc_ref, m_ref, qlt_ref, *, tile, topk, s_len, hg):
    i = pl.program_id(1)
    half = LANES // 2
    nih = DSA_IDX_HEADS

    def tile_rows(j):
        return pl.ds(pl.multiple_of(j * tile, tile), tile)

    dt = qi_ref.dtype
    zeros = jnp.zeros((half, tile), jnp.float32)
    parts = []
    for pair in range(nih // 2):
        pair_t = qi_ref[0, :, pair * LANES:(pair + 1) * LANES].astype(jnp.float32).T
        parts.append(jnp.concatenate([pair_t[:half], zeros], axis=0))
        parts.append(jnp.concatenate([zeros, pair_t[half:]], axis=0))
    qs_t = jnp.concatenate(parts, axis=1).astype(dt)
    w_t = wt_ref[0]
    krow = lax.broadcasted_iota(jnp.int32, (tile, tile), 0)
    qcol = lax.broadcasted_iota(jnp.int32, (tile, tile), 1)

    def index_tile(j, carry):
        rows = tile_rows(j)
        rel = jnp.dot(ki_ref[0, rows, :], qs_t, preferred_element_type=jnp.float32)
        sc = jnp.zeros((tile, tile), jnp.float32)
        for h in range(nih):
            sc = sc + jnp.maximum(rel[:, h * tile:(h + 1) * tile], 0.0) * w_t[h:h + 1, :]
        sc = jnp.where(sc == 0.0, 0.0, sc)
        bits = lax.bitcast_convert_type(sc, jnp.int32)
        u = jnp.where(bits < 0, bits ^ jnp.int32(0x7FFFFFFF), bits)
        causal = (j * tile + krow) <= (i * tile + qcol)
        u = jnp.where(causal, u, jnp.int32(INT_MIN))
        hi_ref[rows, :] = lax.shift_right_arithmetic(u, I16_BITS).astype(jnp.int16)
        lo_ref[rows, :] = ((u & jnp.int32(I16_SPAN - 1)) - I16_SPAN // 2).astype(jnp.int16)
        return carry

    lax.fori_loop(0, i + 1, index_tile, 0)

    cu = DSA_COUNT_UNROLL
    i16_min = jnp.int16(-I16_SPAN // 2)
    for k in range(1, cu):
        @pl.when(i + k < s_len // tile)
        def _(k=k):
            hi_ref[tile_rows(i + k), :] = jnp.full((tile, tile), i16_min, jnp.int16)
            lo_ref[tile_rows(i + k), :] = jnp.full((tile, tile), i16_min, jnp.int16)

    def count(pred):
        def body(jg, acc):
            for k in range(cu):
                j = jg * cu + k
                ones = jnp.where(pred(j, tile_rows(j)), jnp.int16(1), jnp.int16(0))
                for r in range(0, tile, PACKED_SUBLANES):
                    acc = acc + ones[r:r + PACKED_SUBLANES]
            return acc
        acc = lax.fori_loop(0, (i + cu) // cu, body, jnp.zeros((PACKED_SUBLANES, tile), jnp.int16))
        return jnp.sum(acc.astype(jnp.float32), axis=0, keepdims=True)

    def to_i16(x):
        return x.astype(jnp.int16)

    few = (i * tile + lax.broadcasted_iota(jnp.int32, (1, tile), 1)) < topk

    def search16(ref, target):
        def unsettled(carry):
            bi, _, cnt_ans = carry
            pending = jnp.logical_not(few) & (cnt_ans != target)
            return (bi < I16_BITS) & (jnp.max(jnp.where(pending, 1.0, 0.0)) > 0.0)

        def bit_steps(carry):
            bi, ans, cnt_ans = carry
            for k in range(DSA_BITS_PER_CHECK):
                cand = ans | lax.shift_left(jnp.int32(1), I16_BITS - 1 - (bi + k))
                t16 = to_i16(cand - I16_SPAN // 2)
                cnt = count(lambda j, rows, t16=t16: ref[rows, :] >= t16)
                ok = cnt >= target
                ans, cnt_ans = jnp.where(ok, cand, ans), jnp.where(ok, cnt, cnt_ans)
            return bi + DSA_BITS_PER_CHECK, ans, cnt_ans

        _, ans, cnt = lax.while_loop(unsettled, bit_steps,
                                     (jnp.int32(0), jnp.zeros((1, tile), jnp.int32),
                                      jnp.zeros((1, tile), jnp.float32)))
        return ans - I16_SPAN // 2, cnt

    def count_above(ref, t):
        cnt = count(lambda j, rows: ref[rows, :] >= to_i16(jnp.minimum(t + 1, I16_SPAN // 2 - 1)))
        return jnp.where(t + 1 > I16_SPAN // 2 - 1, 0.0, cnt)

    topk_f = jnp.full((1, tile), float(topk), jnp.float32)
    thr_hi, cnt_hi = search16(hi_ref, topk_f)
    cnt_above = count_above(hi_ref, thr_hi)
    need_lo = topk_f - cnt_above
    h16 = to_i16(thr_hi)

    def keep_low(jg, carry):
        for k in range(cu):
            rows = tile_rows(jg * cu + k)
            lo_ref[rows, :] = jnp.where(hi_ref[rows, :] == h16, lo_ref[rows, :], i16_min)
        return carry

    lax.fori_loop(0, (i + cu) // cu, keep_low, 0)
    thr_lo, cnt_lo = search16(lo_ref, need_lo)
    cnt_lo = jnp.where(cnt_lo == 0.0, cnt_hi - cnt_above, cnt_lo)

    nbits = int(s_len).bit_length()
    cut_all = jnp.full((1, tile), 2 ** nbits - 1, jnp.int32)
    krow16 = lax.broadcasted_iota(jnp.int16, (tile, tile), 0)
    l16 = to_i16(thr_lo)

    def tie_search():
        need = need_lo - count_above(lo_ref, thr_lo)

        def cut_step(bi, cut):
            cand = cut | lax.shift_left(jnp.int32(1), nbits - 1 - bi)
            cnt = count(lambda j, rows: (lo_ref[rows, :] == l16) & (hi_ref[rows, :] == h16)
                        & (krow16 < to_i16(jnp.minimum(cand - j * tile, I16_SPAN // 2 - 1))))
            return jnp.where(cnt <= need, cand, cut)

        return lax.fori_loop(0, nbits, cut_step, jnp.zeros((1, tile), jnp.int32))

    tie = jnp.logical_not(few) & (cnt_lo > need_lo)
    cut = lax.cond(jnp.max(jnp.where(tie, 1.0, 0.0)) > 0.0, tie_search, lambda: cut_all)
    l16_sel = to_i16(jnp.where(thr_hi == -I16_SPAN // 2, I16_SPAN // 2 - 1, thr_lo))
    mask_0 = jnp.zeros((), c_ref.dtype)
    mask_neg = jnp.asarray(NEG, c_ref.dtype)

    m_ref[...] = jnp.full(m_ref.shape, NEG, jnp.float32)
    acc_ref[...] = jnp.zeros(acc_ref.shape, jnp.float32)
    for h in range(N_HEADS):
        qlt_ref[h // hg, :, (h % hg) * tile:(h % hg + 1) * tile] = (
            ql_ref[0, h].astype(jnp.float32).T.astype(qlt_ref.dtype))

    def attend_tile(j, table_ref, pen):
        rows = tile_rows(j)
        hi = hi_ref[rows, :]
        lo = lo_ref[rows, :]
        cut16 = to_i16(jnp.clip(cut - j * tile, -I16_SPAN // 2, I16_SPAN // 2 - 1))
        tie_m = jnp.where(krow16 < cut16, mask_0, mask_neg)
        low_m = jnp.where(lo > l16_sel, mask_0, jnp.where(lo == l16_sel, tie_m, mask_neg))
        mask16 = jnp.where(hi > h16, mask_0, jnp.where(hi == h16, low_m, mask_neg))
        maskadd = mask16.astype(jnp.float32) + pen
        c_t = c_ref[0, rows, :]
        ct_t = ct_ref[0, j]

        for g in range(N_HEADS // hg):
            logits = jnp.dot(c_t, qlt_ref[g], preferred_element_type=jnp.float32)
            for hl in range(hg):
                cols = slice(hl * tile, (hl + 1) * tile)
                s = logits[:, cols] + maskadd
                if table_ref is not None:
                    s = s + table_ref[g, :, cols]
                m_old = m_ref[g, :, cols]
                m_new = jnp.maximum(m_old, jnp.max(s, axis=0, keepdims=True))
                p = jnp.exp2(s - m_new).astype(ct_t.dtype)
                pv = jnp.dot(ct_t, p, preferred_element_type=jnp.float32)
                acc_ref[g, :, cols] = jnp.exp2(m_old - m_new) * acc_ref[g, :, cols] + pv
                m_ref[g, :, cols] = m_new

    n_far = jnp.maximum(i - 1, 0)

    def far_pair(jp, carry):
        attend_tile(2 * jp, None, 0.0)
        attend_tile(2 * jp + 1, None, 0.0)
        return carry

    lax.fori_loop(0, n_far // 2, far_pair, 0)

    @pl.when(n_far % 2 == 1)
    def _():
        attend_tile(n_far - 1, None, 0.0)

    attend_tile(jnp.maximum(i - 1, 0), tadj_ref, jnp.where(i >= 1, 0.0, NEG))
    attend_tile(i, tdiag_ref, 0.0)

    rank = wuvt_ref.shape[2]
    parts = []
    for h in range(N_HEADS):
        g, cols = h // hg, slice((h % hg) * tile, (h % hg + 1) * tile)
        ol = (acc_ref[g, 0:rank, cols] / acc_ref[g, rank:rank + 1, cols]).astype(wuvt_ref.dtype)
        parts.append(jnp.dot(wuvt_ref[h], ol, preferred_element_type=jnp.float32))
    o_ref[0] = jnp.concatenate(parts, axis=0).T.astype(o_ref.dtype)


def _dsa_attention(ql, c, ct, qi, ki2, wt, w_uv, rel_bias):
    b, s, r = c.shape
    tile = DSA_TILE
    nt = s // tile
    topk = min(DSA_TOPK_MAX, s // 4)
    assert s % (tile * DSA_COUNT_UNROLL) == 0 and ct.shape[3] == tile and s < I16_SPAN // 2
    rp = ct.shape[2]
    wuvt = jnp.transpose(w_uv, (1, 2, 0)).astype(c.dtype)
    kk = np.arange(tile)[:, None]
    qq = np.arange(tile)[None, :]
    shift = rel_bias[REL_BUCKETS - 1]
    hg = DSA_HEAD_GROUP
    ng = N_HEADS // hg

    def wide(tab):
        return jnp.transpose(tab.reshape(ng, hg, tile, tile), (0, 2, 1, 3)).reshape(ng, tile, hg * tile)

    t_diag = wide(_bias_table(rel_bias, qq - kk, qq >= kk, shift)) * LOG2E
    t_adj = wide(_bias_table(rel_bias, tile + qq - kk, np.ones((tile, tile), bool), shift)) * LOG2E
    ni = qi.shape[2]
    once = pl.Buffered(1)
    return pl.pallas_call(
        functools.partial(_dsa_kernel, tile=tile, topk=topk, s_len=s, hg=hg),
        out_shape=jax.ShapeDtypeStruct((b, s, N_HEADS * HEAD_DIM), c.dtype),
        grid=(b, nt),
        in_specs=[pl.BlockSpec((1, tile, ni), lambda bb, i: (bb, i, 0)),
                  pl.BlockSpec((1, DSA_IDX_HEADS, tile), lambda bb, i: (bb, 0, i)),
                  pl.BlockSpec((1, s, LANES), lambda bb, i: (bb, 0, 0), pipeline_mode=once),
                  pl.BlockSpec((1, s, r), lambda bb, i: (bb, 0, 0), pipeline_mode=once),
                  pl.BlockSpec((1, nt, rp, tile), lambda bb, i: (bb, 0, 0, 0), pipeline_mode=once),
                  pl.BlockSpec((1, N_HEADS, tile, r), lambda bb, i: (bb, 0, i, 0)),
                  pl.BlockSpec((ng, tile, hg * tile), lambda bb, i: (0, 0, 0), pipeline_mode=once),
                  pl.BlockSpec((ng, tile, hg * tile), lambda bb, i: (0, 0, 0), pipeline_mode=once),
                  pl.BlockSpec((N_HEADS, HEAD_DIM, r), lambda bb, i: (0, 0, 0), pipeline_mode=once)],
        out_specs=pl.BlockSpec((1, tile, N_HEADS * HEAD_DIM), lambda bb, i: (bb, i, 0)),
        scratch_shapes=[pltpu.VMEM((s, tile), jnp.int16),
                        pltpu.VMEM((s, tile), jnp.int16),
                        pltpu.VMEM((ng, rp, hg * tile), jnp.float32),
                        pltpu.VMEM((ng, 1, hg * tile), jnp.float32),
                        pltpu.VMEM((ng, r, hg * tile), c.dtype)],
        compiler_params=_params(("parallel", "arbitrary")),
        name="dsa_attention",
    )(qi, wt, ki2, c, ct, ql, t_diag, t_adj, wuvt)


def _dsa_mixer(h, g, w_in, kv_norm, w_uk, w_uv, rel_bias, mxu_dtype):
    ql, c, ct, qi, ki2, wt = _dsa_proj(h, g, w_in, kv_norm, w_uk, mxu_dtype)
    return _dsa_attention(ql, c, ct, qi, ki2, wt, w_uv, rel_bias)


def kernel(x, rel_bias, norm_mix, norm_mlp, mlp_up, mlp_down, a_w_in, a_w_out, b_w_in, b_sinks, b_w_out,
           c_w_in, c_kv_norm, c_w_uk, c_w_uv, c_w_out, final_norm):
    b, s, d = x.shape
    t = b * s
    dt = MXU_DTYPE
    depth = norm_mix.shape[0]
    h = x.reshape(t, d)
    for i in range(depth):
        kind, j = i % 3, i // 3
        if kind == 0:
            qkv = _norm_proj(h, norm_mix[i], _moba_in_weights(a_w_in[j], dt)).reshape(b, s, -1)
            o, w_out = _moba_attention(qkv, rel_bias), a_w_out[j]
        elif kind == 1:
            proj = _norm_proj(h, norm_mix[i], _swa_in_weights(b_w_in[j], dt)).reshape(b, s, -1)
            o, w_out = _swa_attention(proj, b_sinks[j], rel_bias), b_w_out[j]
        else:
            o = _dsa_mixer(h.reshape(b, s, d), norm_mix[i], c_w_in[j], c_kv_norm[j], c_w_uk[j],
                           c_w_uv[j], rel_bias, dt)
            w_out = c_w_out[j]
        h = _out_proj_mlp(o.reshape(t, -1), w_out.astype(dt), h, norm_mlp[i], mlp_up[i].astype(dt),
                          mlp_down[i].astype(dt), final_norm, final_norm=(i == depth - 1))
    return h.reshape(b, s, d)
```

```python
import functools
import math

import numpy as np
import jax
import jax.numpy as jnp
from jax import lax
from jax.experimental import pallas as pl
from jax.experimental.pallas import tpu as pltpu

LANES = 128
PACKED_SUBLANES = 16
VMEM_LIMIT_BYTES = 56 * 1024 * 1024
MXU_DTYPE = jnp.bfloat16
NEG = -1e30
LOG2E = math.log2(math.e)
INT_MIN = -2 ** 31
I16_BITS = 16
I16_SPAN = 2 ** I16_BITS

N_HEADS = 16
HEAD_DIM = 64
NORM_EPS = 1e-6
REL_BUCKETS = 32
REL_MAX_DIST = 128
MOBA_BLOCK = 256
MOBA_TOPK = 3
MOBA_GROUP = 4
MOBA_KEY_UNIT = 128
SWA_WINDOW = 128
SWA_BLOCK = 128
SWA_KV_HEADS = 2
DSA_KV_RANK = 256
DSA_IDX_HEADS = 8
DSA_IDX_DIM = 64
DSA_TOPK_MAX = 256
DSA_TILE = 256
DSA_HEAD_GROUP = 8
DSA_COUNT_UNROLL = 4
DSA_BITS_PER_CHECK = 4
MOBA_ONES_ROWS = SWA_ONES_ROWS = DSA_ONES_ROWS = PACKED_SUBLANES

_NT = (((1,), (1,)), ((), ()))


def _params(semantics):
    return pltpu.CompilerParams(dimension_semantics=semantics, vmem_limit_bytes=VMEM_LIMIT_BYTES)


def _rmsnorm(x, g):
    var = jnp.mean(x * x, axis=-1, keepdims=True)
    return x * lax.rsqrt(var + NORM_EPS) * g


def _norm_proj_kernel(x_ref, g_ref, w_ref, o_ref):
    xn = _rmsnorm(x_ref[...], g_ref[...]).astype(w_ref.dtype)
    o_ref[...] = jnp.dot(xn, w_ref[...], preferred_element_type=jnp.float32).astype(o_ref.dtype)


def _norm_proj(x, g, w, *, tm=512):
    t, d = x.shape
    n = w.shape[1]
    assert t % tm == 0
    return pl.pallas_call(
        _norm_proj_kernel,
        out_shape=jax.ShapeDtypeStruct((t, n), w.dtype),
        grid=(t // tm,),
        in_specs=[pl.BlockSpec((tm, d), lambda i: (i, 0)),
                  pl.BlockSpec((1, d), lambda i: (0, 0)),
                  pl.BlockSpec((d, n), lambda i: (0, 0))],
        out_specs=pl.BlockSpec((tm, n), lambda i: (i, 0)),
        compiler_params=_params(("parallel",)),
        name="norm_proj",
    )(x, g.reshape(1, d), w)


def _mlp_kernel(a_ref, wo_ref, r_ref, g_ref, wu_ref, wd_ref, gf_ref, o_ref, xn_ref, acc_ref,
                *, final_norm):
    f = pl.program_id(1)

    @pl.when(f == 0)
    def _():
        x = r_ref[...] + jnp.dot(a_ref[...], wo_ref[...], preferred_element_type=jnp.float32)
        xn_ref[...] = _rmsnorm(x, g_ref[...]).astype(xn_ref.dtype)
        acc_ref[...] = x

    u = jnp.dot(xn_ref[...], wu_ref[...], preferred_element_type=jnp.float32)
    a = jnp.square(jnp.maximum(u, 0.0)).astype(wd_ref.dtype)
    acc_ref[...] += jnp.dot(a, wd_ref[...], preferred_element_type=jnp.float32)

    @pl.when(f == pl.num_programs(1) - 1)
    def _():
        y = acc_ref[...]
        if final_norm:
            y = _rmsnorm(y, gf_ref[...])
        o_ref[...] = y


def _out_proj_mlp(a, w_out, res, g, w_up, w_down, g_final, *, final_norm, tm=1024, tf=1024):
    t, d = res.shape
    k = a.shape[1]
    ff = w_up.shape[1]
    assert t % tm == 0 and ff % tf == 0
    return pl.pallas_call(
        functools.partial(_mlp_kernel, final_norm=final_norm),
        out_shape=jax.ShapeDtypeStruct((t, d), jnp.float32),
        grid=(t // tm, ff // tf),
        in_specs=[pl.BlockSpec((tm, k), lambda i, f: (i, 0)),
                  pl.BlockSpec((k, d), lambda i, f: (0, 0)),
                  pl.BlockSpec((tm, d), lambda i, f: (i, 0)),
                  pl.BlockSpec((1, d), lambda i, f: (0, 0)),
                  pl.BlockSpec((d, tf), lambda i, f: (0, f)),
                  pl.BlockSpec((tf, d), lambda i, f: (f, 0)),
                  pl.BlockSpec((1, d), lambda i, f: (0, 0))],
        out_specs=pl.BlockSpec((tm, d), lambda i, f: (i, 0)),
        scratch_shapes=[pltpu.VMEM((tm, d), w_up.dtype), pltpu.VMEM((tm, d), jnp.float32)],
        compiler_params=_params(("parallel", "arbitrary")),
        name="out_proj_mlp",
    )(a, w_out, res, g.reshape(1, d), w_up, w_down, g_final.reshape(1, d))


def _rel_bucket_np(dist):
    n = np.maximum(dist, 0)
    max_exact = REL_BUCKETS // 2
    nf = np.maximum(n, 1).astype(np.float64)
    large = max_exact + (np.log(nf / max_exact) / math.log(REL_MAX_DIST / max_exact)
                         * (REL_BUCKETS - max_exact)).astype(np.int64)
    large = np.minimum(large, REL_BUCKETS - 1)
    return np.where(n < max_exact, n, large).astype(np.int32)


def _bias_table(rel_bias, dist, valid, shift):
    bucket = jnp.asarray(_rel_bucket_np(dist).reshape(1, -1))
    onehot = (bucket == jnp.arange(REL_BUCKETS, dtype=bucket.dtype).reshape(-1, 1)).astype(jnp.float32)
    tab = jnp.einsum("bh,bn->hn", rel_bias, onehot, precision=lax.Precision.HIGHEST)
    tab = tab.reshape((-1,) + dist.shape) - shift.reshape((-1,) + (1,) * dist.ndim)
    return jnp.where(jnp.asarray(valid)[None], tab, NEG).astype(jnp.float32)


def _moba_route_mask(gate_t, i):
    blkid = lax.broadcasted_iota(jnp.int32, gate_t.shape, 0)
    blk_f = blkid.astype(jnp.float32)
    valid = blkid < i
    g = jnp.where(valid, gate_t, -jnp.inf)
    sel = blkid == i
    for _ in range(MOBA_TOPK):
        mx = jnp.max(g, axis=0, keepdims=True)
        first = jnp.min(jnp.where(g == mx, blk_f, float(LANES)), axis=0, keepdims=True)
        pick = (blk_f == first) & valid
        sel = sel | pick
        g = jnp.where(pick, -jnp.inf, g)
    return jnp.where(sel, 0.0, NEG)


def _moba_kernel(q_ref, k_ref, v_ref, tnear_ref, o_ref, kaug_ref, vt_ref, kmrows_ref,
                 *, blk, nblk, group, ku):
    i = pl.program_id(2)
    half = LANES // 2
    pad = group * blk
    lane = lax.broadcasted_iota(jnp.int32, (blk, LANES), 1)
    lo = lane < half

    @pl.when(i == 0)
    def _():
        kmrows_ref[...] = jnp.zeros(kmrows_ref.shape, kmrows_ref.dtype)
        lane1 = lax.broadcasted_iota(jnp.int32, (1, LANES), 1)
        lane_p = lax.broadcasted_iota(jnp.int32, (pad, LANES), 1)
        kaug_ref[0, 0:pad, :] = jnp.where(lane_p == LANES - 1, 1.0, 0.0).astype(kaug_ref.dtype)
        kaug_ref[1, 0:pad, :] = jnp.where(lane_p == half - 1, 1.0, 0.0).astype(kaug_ref.dtype)
        vt_ref[0:pad // ku] = jnp.zeros((pad // ku,) + vt_ref.shape[1:], vt_ref.dtype)
        ones = jnp.ones((vt_ref.shape[2] - half, ku), vt_ref.dtype)

        def build(n, carry):
            rows = pl.ds(pl.multiple_of(n * blk, blk), blk)
            prows = pl.ds(pl.multiple_of(n * blk + pad, blk), blk)
            kn = k_ref[0, rows, :].astype(jnp.float32)
            kaug_ref[0, prows, :] = jnp.where(
                lo, kn, jnp.where(lane == half + n, 1.0, 0.0)).astype(kaug_ref.dtype)
            kaug_ref[1, prows, :] = jnp.where(
                lo, jnp.where(lane == n, 1.0, 0.0), kn).astype(kaug_ref.dtype)
            for t in range(blk // ku):
                vn = v_ref[0, pl.ds(pl.multiple_of(n * blk + t * ku, ku), ku), :]
                vn_t = vn.astype(jnp.float32).T.astype(vt_ref.dtype)
                unit = (n + group) * (blk // ku) + t
                for hh in range(2):
                    vt_ref[unit, hh, 0:half, :] = vn_t[hh * half:(hh + 1) * half]
                    vt_ref[unit, hh, half:, :] = ones
            mean = jnp.sum(kn, axis=0, keepdims=True) * (1.0 / blk)
            kmrows_ref[0, pl.ds(n, 1), :] = jnp.where(lane1 < half, mean, 0.0)
            kmrows_ref[1, pl.ds(n, 1), :] = jnp.where(lane1 < half, 0.0, mean)
            return carry

        lax.fori_loop(0, nblk, build, 0)

    dt = q_ref.dtype
    q_t = q_ref[0].astype(jnp.float32).T
    q_tb = q_t.astype(dt)
    q_aug = []
    for hh in range(2):
        gate_t = jnp.dot(kmrows_ref[hh].astype(dt), q_tb,
                         preferred_element_type=jnp.float32)
        route_t = _moba_route_mask(gate_t, i)
        q_h = q_t[hh * half:(hh + 1) * half]
        q_aug.append(jnp.concatenate([q_h, route_t] if hh == 0 else [route_t, q_h], axis=0).astype(dt))

    upb = blk // ku

    def fold_group(unit0, state, bias_ref=None):
        n_units = group * upb
        rows = pl.ds(pl.multiple_of(unit0 * ku, ku), n_units * ku)
        s = [jnp.dot(kaug_ref[hh, rows, :], q_aug[hh], preferred_element_type=jnp.float32)
             for hh in range(2)]
        state = list(state)
        unbiased = n_units if bias_ref is None else n_units - bias_ref.shape[1] // ku
        for t in range(n_units):
            for hh in range(2):
                m, acc = state[hh]
                s_t = s[hh][t * ku:(t + 1) * ku]
                if t >= unbiased:
                    s_t = s_t + bias_ref[hh, (t - unbiased) * ku:(t - unbiased + 1) * ku, :]
                m_new = jnp.maximum(m, jnp.max(s_t, axis=0, keepdims=True))
                p = jnp.exp2(s_t - m_new).astype(dt)
                acc = jnp.exp2(m - m_new) * acc + jnp.dot(vt_ref[unit0 + t, hh], p,
                                                          preferred_element_type=jnp.float32)
                state[hh] = (m_new, acc)
        return state

    start = (jnp.full((1, blk), NEG, jnp.float32), jnp.zeros((vt_ref.shape[2], blk), jnp.float32))
    state = fold_group((i + 1) * upb, [start, start], tnear_ref)

    def far(g, carry):
        st = fold_group((i + 1 - group * g) * upb, [carry[0:2], carry[2:4]])
        return tuple(st[0]) + tuple(st[1])

    n_far = (i + group) // group - 1
    out = lax.fori_loop(0, n_far // 2, lambda gp, c: far(2 * gp + 2, far(2 * gp + 1, c)),
                        tuple(state[0]) + tuple(state[1]))
    out = lax.cond(n_far % 2 == 1, lambda c: far(n_far, c), lambda c: c, out)
    o_t = jnp.concatenate([out[1][:half] / out[1][half:half + 1],
                           out[3][:half] / out[3][half:half + 1]], axis=0)
    o_ref[0] = o_t.T.astype(o_ref.dtype)


def _moba_in_weights(w_in, dt):
    nq = N_HEADS * HEAD_DIM
    return jnp.concatenate([w_in[:, :nq] * (LOG2E * HEAD_DIM ** -0.5), w_in[:, nq:]], axis=1).astype(dt)


def _moba_attention(qkv, rel_bias):
    b, s, _ = qkv.shape
    blk = MOBA_BLOCK
    group = MOBA_GROUP
    nblk = s // blk
    hp = N_HEADS // 2
    ku = MOBA_KEY_UNIT
    assert s % blk == 0 and nblk < LANES // 2 and 2 * HEAD_DIM == LANES and blk % ku == 0
    assert group >= 2
    kk = np.arange(blk)[:, None]
    qq = np.arange(blk)[None, :]
    shift = rel_bias[REL_BUCKETS - 1]
    t_own = _bias_table(rel_bias, qq - kk, qq >= kk, shift)
    t_adj = _bias_table(rel_bias, blk + qq - kk, np.ones((blk, blk), bool), shift)
    t_near = jnp.concatenate([t_adj, t_own], axis=1) * LOG2E
    return pl.pallas_call(
        functools.partial(_moba_kernel, blk=blk, nblk=nblk, group=group, ku=ku),
        out_shape=jax.ShapeDtypeStruct((b, s, N_HEADS * HEAD_DIM), qkv.dtype),
        grid=(b, hp, nblk),
        in_specs=[pl.BlockSpec((1, blk, LANES), lambda bb, p, i: (bb, i, p)),
                  pl.BlockSpec((1, s, LANES), lambda bb, p, i: (bb, 0, hp + p)),
                  pl.BlockSpec((1, s, LANES), lambda bb, p, i: (bb, 0, 2 * hp + p)),
                  pl.BlockSpec((2, 2 * blk, blk), lambda bb, p, i: (p, 0, 0))],
        out_specs=pl.BlockSpec((1, blk, LANES), lambda bb, p, i: (bb, i, p)),
        scratch_shapes=[pltpu.VMEM((2, s + group * blk, LANES), qkv.dtype),
                        pltpu.VMEM(((s + group * blk) // ku, 2, HEAD_DIM + MOBA_ONES_ROWS, ku), qkv.dtype),
                        pltpu.VMEM((2, LANES // 2, LANES), jnp.float32)],
        compiler_params=_params(("parallel", "parallel", "arbitrary")),
        name="moba_attention",
    )(qkv, qkv, qkv, t_near)


def _swa_kernel(q_ref, kp_ref, kc_ref, vp_ref, vc_ref, tab_ref, sink_ref, o_ref, *, blk, group):
    n = pl.program_id(1)
    half = LANES // 2
    dt = q_ref.dtype
    q_t = q_ref[0].astype(jnp.float32).T.astype(dt)
    kband = jnp.concatenate([kp_ref[0], kc_ref[0]], axis=0)
    vband = jnp.concatenate([vp_ref[0], vc_ref[0]], axis=0)
    v_t = vband.astype(jnp.float32).T.astype(dt)
    ones = jnp.ones((SWA_ONES_ROWS, 2 * blk), dt)
    zeros = jnp.zeros((half, group * blk), dt)
    tab_n = jnp.minimum(n, 1)

    outs = []
    for kv in range(SWA_KV_HEADS):
        heads = range(kv * group, (kv + 1) * group)
        qg = jnp.concatenate([q_t[h * half:(h + 1) * half] for h in heads], axis=1)
        qg = jnp.concatenate([qg, zeros] if kv == 0 else [zeros, qg], axis=0)
        s = jnp.dot(kband, qg, preferred_element_type=jnp.float32) + tab_ref[tab_n, kv]
        sink = sink_ref[kv]
        m = jnp.maximum(jnp.max(s, axis=0, keepdims=True), sink)
        p = jnp.exp2(s - m).astype(dt)
        v1 = jnp.concatenate([v_t[kv * half:(kv + 1) * half], ones], axis=0)
        acc = jnp.dot(v1, p, preferred_element_type=jnp.float32)
        o = acc[0:half] / (acc[half:half + 1] + jnp.exp2(sink - m))
        outs.extend(o[:, g * blk:(g + 1) * blk] for g in range(group))
    o_ref[0] = jnp.concatenate(outs, axis=0).T.astype(o_ref.dtype)


def _swa_in_weights(w_in, dt):
    nq = N_HEADS * HEAD_DIM
    return jnp.concatenate([w_in[:, :nq] * (LOG2E * HEAD_DIM ** -0.5), w_in[:, nq:]], axis=1).astype(dt)


def _swa_attention(proj, sinks, rel_bias):
    b, s, _ = proj.shape
    blk = SWA_BLOCK
    nb = s // blk
    group = N_HEADS // SWA_KV_HEADS
    nq = N_HEADS * HEAD_DIM
    assert s % blk == 0 and SWA_KV_HEADS * HEAD_DIM == LANES and SWA_WINDOW <= blk
    kcol = nq // LANES
    dist = blk + np.arange(blk)[None, :] - np.arange(2 * blk)[:, None]
    window = (dist >= 0) & (dist < SWA_WINDOW)
    first = window & (np.arange(2 * blk)[:, None] >= blk)
    zero = jnp.zeros_like(sinks)

    def wide(valid):
        tab = _bias_table(rel_bias, dist, valid, zero).reshape(SWA_KV_HEADS, group, 2 * blk, blk)
        return jnp.transpose(tab, (0, 2, 1, 3)).reshape(SWA_KV_HEADS, 2 * blk, group * blk) * LOG2E

    tabs = jnp.stack([wide(first), wide(window)])
    sink_w = jnp.broadcast_to(sinks.astype(jnp.float32).reshape(SWA_KV_HEADS, 1, group, 1) * LOG2E,
                              (SWA_KV_HEADS, 1, group, blk)).reshape(SWA_KV_HEADS, 1, group * blk)
    prev = lambda bb, n: (bb, jnp.maximum(n - 1, 0), kcol)
    cur = lambda bb, n: (bb, n, kcol)
    prev_v = lambda bb, n: (bb, jnp.maximum(n - 1, 0), kcol + 1)
    cur_v = lambda bb, n: (bb, n, kcol + 1)
    return pl.pallas_call(
        functools.partial(_swa_kernel, blk=blk, group=group),
        out_shape=jax.ShapeDtypeStruct((b, s, nq), proj.dtype),
        grid=(b, nb),
        in_specs=[pl.BlockSpec((1, blk, nq), lambda bb, n: (bb, n, 0)),
                  pl.BlockSpec((1, blk, LANES), prev),
                  pl.BlockSpec((1, blk, LANES), cur),
                  pl.BlockSpec((1, blk, LANES), prev_v),
                  pl.BlockSpec((1, blk, LANES), cur_v),
                  pl.BlockSpec(tabs.shape, lambda bb, n: (0, 0, 0, 0)),
                  pl.BlockSpec(sink_w.shape, lambda bb, n: (0, 0, 0))],
        out_specs=pl.BlockSpec((1, blk, nq), lambda bb, n: (bb, n, 0)),
        compiler_params=_params(("parallel", "arbitrary")),
        name="swa_attention",
    )(proj, proj, proj, proj, proj, tabs, sink_w)


def _dsa_proj_kernel(x_ref, g_ref, w_ref, wwt_ref, kvn_ref, wuk_ref,
                     ql_ref, c_ref, ct_ref, qi_ref, ki_ref, wt_ref, *, idx_scale):
    dt = w_ref.dtype
    nq = N_HEADS * HEAD_DIM
    r = DSA_KV_RANK
    ni = DSA_IDX_HEADS * DSA_IDX_DIM
    xn = _rmsnorm(x_ref[0], g_ref[...]).astype(dt)
    y = jnp.dot(xn, w_ref[...], preferred_element_type=jnp.float32)
    scale = LOG2E * HEAD_DIM ** -0.5
    for p in range(N_HEADS // 2):
        qp = y[:, p * LANES:(p + 1) * LANES].astype(dt)
        ql = jnp.dot(qp, wuk_ref[p], preferred_element_type=jnp.float32) * scale
        ql_ref[0, 2 * p] = ql[:, :r].astype(ql_ref.dtype)
        ql_ref[0, 2 * p + 1] = ql[:, r:].astype(ql_ref.dtype)
    c = _rmsnorm(y[:, nq:nq + r], kvn_ref[...])
    c_ref[0] = c.astype(c_ref.dtype)
    kt = ct_ref.shape[3]
    for t in range(ct_ref.shape[1]):
        ct_ref[0, t, 0:r, :] = c[t * kt:(t + 1) * kt].T.astype(ct_ref.dtype)
        ct_ref[0, t, r:, :] = jnp.ones((ct_ref.shape[2] - r, kt), ct_ref.dtype)
    qi_ref[0] = y[:, nq + r:nq + r + ni].astype(qi_ref.dtype)
    ki_ref[0] = y[:, nq + r + ni:nq + r + ni + LANES].astype(ki_ref.dtype)
    wt = lax.dot_general(wwt_ref[...], xn, _NT, preferred_element_type=jnp.float32)
    wt_ref[0] = wt * idx_scale


def _dsa_proj(x, g, w_in, kv_norm, w_uk, dt, *, tm=512):
    b, s, d = x.shape
    nq = N_HEADS * HEAD_DIM
    r = DSA_KV_RANK
    ni = DSA_IDX_HEADS * DSA_IDX_DIM
    di = DSA_IDX_DIM
    assert 2 * di == LANES and 2 * HEAD_DIM == LANES and s % tm == 0
    k_idx = w_in[:, nq + r + ni:nq + r + ni + di]
    w_main = jnp.concatenate([w_in[:, :nq + r + ni], k_idx, k_idx], axis=1).astype(dt)
    wwt = w_in[:, nq + r + ni + di:].T.astype(dt)
    uk = jnp.transpose(w_uk, (1, 2, 0)).reshape(N_HEADS // 2, 2, HEAD_DIM, r)
    z = jnp.zeros_like(uk[:, 0])
    wuk_bd = jnp.concatenate([jnp.concatenate([uk[:, 0], z], axis=2),
                              jnp.concatenate([z, uk[:, 1]], axis=2)], axis=1).astype(dt)
    nw = w_main.shape[1]
    idx_scale = DSA_IDX_HEADS ** -0.5 * DSA_IDX_DIM ** -0.5
    kt = DSA_TILE
    rp = r + DSA_ONES_ROWS
    assert tm % kt == 0
    return pl.pallas_call(
        functools.partial(_dsa_proj_kernel, idx_scale=idx_scale),
        out_shape=(jax.ShapeDtypeStruct((b, N_HEADS, s, r), dt),
                   jax.ShapeDtypeStruct((b, s, r), dt),
                   jax.ShapeDtypeStruct((b, s // kt, rp, kt), dt),
                   jax.ShapeDtypeStruct((b, s, ni), dt),
                   jax.ShapeDtypeStruct((b, s, LANES), dt),
                   jax.ShapeDtypeStruct((b, DSA_IDX_HEADS, s), jnp.float32)),
        grid=(b, s // tm),
        in_specs=[pl.BlockSpec((1, tm, d), lambda bb, i: (bb, i, 0)),
                  pl.BlockSpec((1, d), lambda bb, i: (0, 0)),
                  pl.BlockSpec((d, nw), lambda bb, i: (0, 0)),
                  pl.BlockSpec((DSA_IDX_HEADS, d), lambda bb, i: (0, 0)),
                  pl.BlockSpec((1, r), lambda bb, i: (0, 0)),
                  pl.BlockSpec((N_HEADS // 2, LANES, 2 * r), lambda bb, i: (0, 0, 0))],
        out_specs=(pl.BlockSpec((1, N_HEADS, tm, r), lambda bb, i: (bb, 0, i, 0)),
                   pl.BlockSpec((1, tm, r), lambda bb, i: (bb, i, 0)),
                   pl.BlockSpec((1, tm // kt, rp, kt), lambda bb, i: (bb, i, 0, 0)),
                   pl.BlockSpec((1, tm, ni), lambda bb, i: (bb, i, 0)),
                   pl.BlockSpec((1, tm, LANES), lambda bb, i: (bb, i, 0)),
                   pl.BlockSpec((1, DSA_IDX_HEADS, tm), lambda bb, i: (bb, 0, i))),
        compiler_params=_params(("parallel", "parallel")),
        name="dsa_proj",
    )(x, g.reshape(1, d), w_main, wwt, kv_norm.reshape(1, r), wuk_bd)


def _dsa_kernel(qi_ref, wt_ref, ki_ref, c_ref, ct_ref, ql_ref, tdiag_ref, tadj_ref, wuvt_ref, o_ref,
                hi_ref, lo_ref, acc_ref, m_ref, qlt_ref, *, tile, topk, s_len, hg):
    i = pl.program_id(1)
    half = LANES // 2
    nih = DSA_IDX_HEADS

    def tile_rows(j):
        return pl.ds(pl.multiple_of(j * tile, tile), tile)

    dt = qi_ref.dtype
    zeros = jnp.zeros((half, tile), jnp.float32)
    parts = []
    for pair in range(nih // 2):
        pair_t = qi_ref[0, :, pair * LANES:(pair + 1) * LANES].astype(jnp.float32).T
        parts.append(jnp.concatenate([pair_t[:half], zeros], axis=0))
        parts.append(jnp.concatenate([zeros, pair_t[half:]], axis=0))
    qs_t = jnp.concatenate(parts, axis=1).astype(dt)
    w_t = wt_ref[0]
    krow = lax.broadcasted_iota(jnp.int32, (tile, tile), 0)
    qcol = lax.broadcasted_iota(jnp.int32, (tile, tile), 1)

    def index_tile(j, carry):
        rows = tile_rows(j)
        rel = jnp.dot(ki_ref[0, rows, :], qs_t, preferred_element_type=jnp.float32)
        sc = jnp.zeros((tile, tile), jnp.float32)
        for h in range(nih):
            sc = sc + jnp.maximum(rel[:, h * tile:(h + 1) * tile], 0.0) * w_t[h:h + 1, :]
        sc = jnp.where(sc == 0.0, 0.0, sc)
        bits = lax.bitcast_convert_type(sc, jnp.int32)
        u = jnp.where(bits < 0, bits ^ jnp.int32(0x7FFFFFFF), bits)
        causal = (j * tile + krow) <= (i * tile + qcol)
        u = jnp.where(causal, u, jnp.int32(INT_MIN))
        hi_ref[rows, :] = lax.shift_right_arithmetic(u, I16_BITS).astype(jnp.int16)
        lo_ref[rows, :] = ((u & jnp.int32(I16_SPAN - 1)) - I16_SPAN // 2).astype(jnp.int16)
        return carry

    lax.fori_loop(0, i + 1, index_tile, 0)

    cu = DSA_COUNT_UNROLL
    i16_min = jnp.int16(-I16_SPAN // 2)
    for k in range(1, cu):
        @pl.when(i + k < s_len // tile)
        def _(k=k):
            hi_ref[tile_rows(i + k), :] = jnp.full((tile, tile), i16_min, jnp.int16)
            lo_ref[tile_rows(i + k), :] = jnp.full((tile, tile), i16_min, jnp.int16)

    def count(pred):
        def body(jg, acc):
            for k in range(cu):
                j = jg * cu + k
                ones = jnp.where(pred(j, tile_rows(j)), jnp.int16(1), jnp.int16(0))
                for r in range(0, tile, PACKED_SUBLANES):
                    acc = acc + ones[r:r + PACKED_SUBLANES]
            return acc
        acc = lax.fori_loop(0, (i + cu) // cu, body, jnp.zeros((PACKED_SUBLANES, tile), jnp.int16))
        return jnp.sum(acc.astype(jnp.float32), axis=0, keepdims=True)

    def to_i16(x):
        return x.astype(jnp.int16)

    few = (i * tile + lax.broadcasted_iota(jnp.int32, (1, tile), 1)) < topk

    def search16(ref, target):
        def unsettled(carry):
            bi, _, cnt_ans = carry
            pending = jnp.logical_not(few) & (cnt_ans != target)
            return (bi < I16_BITS) & (jnp.max(jnp.where(pending, 1.0, 0.0)) > 0.0)

        def bit_steps(carry):
            bi, ans, cnt_ans = carry
            for k in range(DSA_BITS_PER_CHECK):
                cand = ans | lax.shift_left(jnp.int32(1), I16_BITS - 1 - (bi + k))
                t16 = to_i16(cand - I16_SPAN // 2)
                cnt = count(lambda j, rows, t16=t16: ref[rows, :] >= t16)
                ok = cnt >= target
                ans, cnt_ans = jnp.where(ok, cand, ans), jnp.where(ok, cnt, cnt_ans)
            return bi + DSA_BITS_PER_CHECK, ans, cnt_ans

        _, ans, cnt = lax.while_loop(unsettled, bit_steps,
                                     (jnp.int32(0), jnp.zeros((1, tile), jnp.int32),
                                      jnp.zeros((1, tile), jnp.float32)))
        return ans - I16_SPAN // 2, cnt

    def count_above(ref, t):
        cnt = count(lambda j, rows: ref[rows, :] >= to_i16(jnp.minimum(t + 1, I16_SPAN // 2 - 1)))
        return jnp.where(t + 1 > I16_SPAN // 2 - 1, 0.0, cnt)

    topk_f = jnp.full((1, tile), float(topk), jnp.float32)
    thr_hi, cnt_hi = search16(hi_ref, topk_f)
    cnt_above = count_above(hi_ref, thr_hi)
    need_lo = topk_f - cnt_above
    h16 = to_i16(thr_hi)

    def keep_low(jg, carry):
        for k in range(cu):
            rows = tile_rows(jg * cu + k)
            lo_ref[rows, :] = jnp.where(hi_ref[rows, :] == h16, lo_ref[rows, :], i16_min)
        return carry

    lax.fori_loop(0, (i + cu) // cu, keep_low, 0)
    thr_lo, cnt_lo = search16(lo_ref, need_lo)
    cnt_lo = jnp.where(cnt_lo == 0.0, cnt_hi - cnt_above, cnt_lo)

    nbits = int(s_len).bit_length()
    cut_all = jnp.full((1, tile), 2 ** nbits - 1, jnp.int32)
    krow16 = lax.broadcasted_iota(jnp.int16, (tile, tile), 0)
    l16 = to_i16(thr_lo)

    def tie_search():
        need = need_lo - count_above(lo_ref, thr_lo)

        def cut_step(bi, cut):
            cand = cut | lax.shift_left(jnp.int32(1), nbits - 1 - bi)
            cnt = count(lambda j, rows: (lo_ref[rows, :] == l16) & (hi_ref[rows, :] == h16)
                        & (krow16 < to_i16(jnp.minimum(cand - j * tile, I16_SPAN // 2 - 1))))
            return jnp.where(cnt <= need, cand, cut)

        return lax.fori_loop(0, nbits, cut_step, jnp.zeros((1, tile), jnp.int32))

    tie = jnp.logical_not(few) & (cnt_lo > need_lo)
    cut = lax.cond(jnp.max(jnp.where(tie, 1.0, 0.0)) > 0.0, tie_search, lambda: cut_all)
    l16_sel = to_i16(jnp.where(thr_hi == -I16_SPAN // 2, I16_SPAN // 2 - 1, thr_lo))
    mask_0 = jnp.zeros((), c_ref.dtype)
    mask_neg = jnp.asarray(NEG, c_ref.dtype)

    m_ref[...] = jnp.full(m_ref.shape, NEG, jnp.float32)
    acc_ref[...] = jnp.zeros(acc_ref.shape, jnp.float32)
    for h in range(N_HEADS):
        qlt_ref[h // hg, :, (h % hg) * tile:(h % hg + 1) * tile] = (
            ql_ref[0, h].astype(jnp.float32).T.astype(qlt_ref.dtype))

    def attend_tile(j, table_ref, pen):
        rows = tile_rows(j)
        hi = hi_ref[rows, :]
        lo = lo_ref[rows, :]
        cut16 = to_i16(jnp.clip(cut - j * tile, -I16_SPAN // 2, I16_SPAN // 2 - 1))
        tie_m = jnp.where(krow16 < cut16, mask_0, mask_neg)
        low_m = jnp.where(lo > l16_sel, mask_0, jnp.where(lo == l16_sel, tie_m, mask_neg))
        mask16 = jnp.where(hi > h16, mask_0, jnp.where(hi == h16, low_m, mask_neg))
        maskadd = mask16.astype(jnp.float32) + pen
        c_t = c_ref[0, rows, :]
        ct_t = ct_ref[0, j]

        for g in range(N_HEADS // hg):
            logits = jnp.dot(c_t, qlt_ref[g], preferred_element_type=jnp.float32)
            for hl in range(hg):
                cols = slice(hl * tile, (hl + 1) * tile)
                s = logits[:, cols] + maskadd
                if table_ref is not None:
                    s = s + table_ref[g, :, cols]
                m_old = m_ref[g, :, cols]
                m_new = jnp.maximum(m_old, jnp.max(s, axis=0, keepdims=True))
                p = jnp.exp2(s - m_new).astype(ct_t.dtype)
                pv = jnp.dot(ct_t, p, preferred_element_type=jnp.float32)
                acc_ref[g, :, cols] = jnp.exp2(m_old - m_new) * acc_ref[g, :, cols] + pv
                m_ref[g, :, cols] = m_new

    n_far = jnp.maximum(i - 1, 0)

    def far_pair(jp, carry):
        attend_tile(2 * jp, None, 0.0)
        attend_tile(2 * jp + 1, None, 0.0)
        return carry

    lax.fori_loop(0, n_far // 2, far_pair, 0)

    @pl.when(n_far % 2 == 1)
    def _():
        attend_tile(n_far - 1, None, 0.0)

    attend_tile(jnp.maximum(i - 1, 0), tadj_ref, jnp.where(i >= 1, 0.0, NEG))
    attend_tile(i, tdiag_ref, 0.0)

    rank = wuvt_ref.shape[2]
    parts = []
    for h in range(N_HEADS):
        g, cols = h // hg, slice((h % hg) * tile, (h % hg + 1) * tile)
        ol = (acc_ref[g, 0:rank, cols] / acc_ref[g, rank:rank + 1, cols]).astype(wuvt_ref.dtype)
        parts.append(jnp.dot(wuvt_ref[h], ol, preferred_element_type=jnp.float32))
    o_ref[0] = jnp.concatenate(parts, axis=0).T.astype(o_ref.dtype)


def _dsa_attention(ql, c, ct, qi, ki2, wt, w_uv, rel_bias):
    b, s, r = c.shape
    tile = DSA_TILE
    nt = s // tile
    topk = min(DSA_TOPK_MAX, s // 4)
    assert s % (tile * DSA_COUNT_UNROLL) == 0 and ct.shape[3] == tile and s < I16_SPAN // 2
    rp = ct.shape[2]
    wuvt = jnp.transpose(w_uv, (1, 2, 0)).astype(c.dtype)
    kk = np.arange(tile)[:, None]
    qq = np.arange(tile)[None, :]
    shift = rel_bias[REL_BUCKETS - 1]
    hg = DSA_HEAD_GROUP
    ng = N_HEADS // hg

    def wide(tab):
        return jnp.transpose(tab.reshape(ng, hg, tile, tile), (0, 2, 1, 3)).reshape(ng, tile, hg * tile)

    t_diag = wide(_bias_table(rel_bias, qq - kk, qq >= kk, shift)) * LOG2E
    t_adj = wide(_bias_table(rel_bias, tile + qq - kk, np.ones((tile, tile), bool), shift)) * LOG2E
    ni = qi.shape[2]
    once = pl.Buffered(1)
    return pl.pallas_call(
        functools.partial(_dsa_kernel, tile=tile, topk=topk, s_len=s, hg=hg),
        out_shape=jax.ShapeDtypeStruct((b, s, N_HEADS * HEAD_DIM), c.dtype),
        grid=(b, nt),
        in_specs=[pl.BlockSpec((1, tile, ni), lambda bb, i: (bb, i, 0)),
                  pl.BlockSpec((1, DSA_IDX_HEADS, tile), lambda bb, i: (bb, 0, i)),
                  pl.BlockSpec((1, s, LANES), lambda bb, i: (bb, 0, 0), pipeline_mode=once),
                  pl.BlockSpec((1, s, r), lambda bb, i: (bb, 0, 0), pipeline_mode=once),
                  pl.BlockSpec((1, nt, rp, tile), lambda bb, i: (bb, 0, 0, 0), pipeline_mode=once),
                  pl.BlockSpec((1, N_HEADS, tile, r), lambda bb, i: (bb, 0, i, 0)),
                  pl.BlockSpec((ng, tile, hg * tile), lambda bb, i: (0, 0, 0), pipeline_mode=once),
                  pl.BlockSpec((ng, tile, hg * tile), lambda bb, i: (0, 0, 0), pipeline_mode=once),
                  pl.BlockSpec((N_HEADS, HEAD_DIM, r), lambda bb, i: (0, 0, 0), pipeline_mode=once)],
        out_specs=pl.BlockSpec((1, tile, N_HEADS * HEAD_DIM), lambda bb, i: (bb, i, 0)),
        scratch_shapes=[pltpu.VMEM((s, tile), jnp.int16),
                        pltpu.VMEM((s, tile), jnp.int16),
                        pltpu.VMEM((ng, rp, hg * tile), jnp.float32),
                        pltpu.VMEM((ng, 1, hg * tile), jnp.float32),
                        pltpu.VMEM((ng, r, hg * tile), c.dtype)],
        compiler_params=_params(("parallel", "arbitrary")),
        name="dsa_attention",
    )(qi, wt, ki2, c, ct, ql, t_diag, t_adj, wuvt)


def _dsa_mixer(h, g, w_in, kv_norm, w_uk, w_uv, rel_bias, mxu_dtype):
    ql, c, ct, qi, ki2, wt = _dsa_proj(h, g, w_in, kv_norm, w_uk, mxu_dtype)
    return _dsa_attention(ql, c, ct, qi, ki2, wt, w_uv, rel_bias)


def kernel(x, rel_bias, norm_mix, norm_mlp, mlp_up, mlp_down, a_w_in, a_w_out, b_w_in, b_sinks, b_w_out,
           c_w_in, c_kv_norm, c_w_uk, c_w_uv, c_w_out, final_norm):
    b, s, d = x.shape
    t = b * s
    dt = MXU_DTYPE
    depth = norm_mix.shape[0]
    h = x.reshape(t, d)
    for i in range(depth):
        kind, j = i % 3, i // 3
        if kind == 0:
            qkv = _norm_proj(h, norm_mix[i], _moba_in_weights(a_w_in[j], dt)).reshape(b, s, -1)
            o, w_out = _moba_attention(qkv, rel_bias), a_w_out[j]
        elif kind == 1:
            proj = _norm_proj(h, norm_mix[i], _swa_in_weights(b_w_in[j], dt)).reshape(b, s, -1)
            o, w_out = _swa_attention(proj, b_sinks[j], rel_bias), b_w_out[j]
        else:
            o = _dsa_mixer(h.reshape(b, s, d), norm_mix[i], c_w_in[j], c_kv_norm[j], c_w_uk[j],
                           c_w_uv[j], rel_bias, dt)
            w_out = c_w_out[j]
        h = _out_proj_mlp(o.reshape(t, -1), w_out.astype(dt), h, norm_mlp[i], mlp_up[i].astype(dt),
                          mlp_down[i].astype(dt), final_norm, final_norm=(i == depth - 1))
    return h.reshape(b, s, d)
```

```python
import functools
import math

import numpy as np
import jax
import jax.numpy as jnp
from jax import lax
from jax.experimental import pallas as pl
from jax.experimental.pallas import tpu as pltpu

LANES = 128
PACKED_SUBLANES = 16
VMEM_LIMIT_BYTES = 56 * 1024 * 1024
MXU_DTYPE = jnp.bfloat16
NEG = -1e30
LOG2E = math.log2(math.e)
INT_MIN = -2 ** 31
I16_BITS = 16
I16_SPAN = 2 ** I16_BITS

N_HEADS = 16
HEAD_DIM = 64
NORM_EPS = 1e-6
REL_BUCKETS = 32
REL_MAX_DIST = 128
MOBA_BLOCK = 256
MOBA_TOPK = 3
MOBA_GROUP = 4
MOBA_KEY_UNIT = 128
SWA_WINDOW = 128
SWA_BLOCK = 128
SWA_KV_HEADS = 2
DSA_KV_RANK = 256
DSA_IDX_HEADS = 8
DSA_IDX_DIM = 64
DSA_TOPK_MAX = 256
DSA_TILE = 256
DSA_HEAD_GROUP = 8
DSA_COUNT_UNROLL = 4
DSA_BITS_PER_CHECK = 4
MOBA_ONES_ROWS = SWA_ONES_ROWS = DSA_ONES_ROWS = PACKED_SUBLANES

_NT = (((1,), (1,)), ((), ()))


def _params(semantics):
    return pltpu.CompilerParams(dimension_semantics=semantics, vmem_limit_bytes=VMEM_LIMIT_BYTES)


def _rmsnorm(x, g):
    var = jnp.mean(x * x, axis=-1, keepdims=True)
    return x * lax.rsqrt(var + NORM_EPS) * g


def _norm_proj_kernel(x_ref, g_ref, w_ref, o_ref):
    xn = _rmsnorm(x_ref[...], g_ref[...]).astype(w_ref.dtype)
    o_ref[...] = jnp.dot(xn, w_ref[...], preferred_element_type=jnp.float32).astype(o_ref.dtype)


def _norm_proj(x, g, w, *, tm=512):
    t, d = x.shape
    n = w.shape[1]
    assert t % tm == 0
    return pl.pallas_call(
        _norm_proj_kernel,
        out_shape=jax.ShapeDtypeStruct((t, n), w.dtype),
        grid=(t // tm,),
        in_specs=[pl.BlockSpec((tm, d), lambda i: (i, 0)),
                  pl.BlockSpec((1, d), lambda i: (0, 0)),
                  pl.BlockSpec((d, n), lambda i: (0, 0))],
        out_specs=pl.BlockSpec((tm, n), lambda i: (i, 0)),
        compiler_params=_params(("parallel",)),
        name="norm_proj",
    )(x, g.reshape(1, d), w)


def _mlp_kernel(a_ref, wo_ref, r_ref, g_ref, wu_ref, wd_ref, gf_ref, o_ref, xn_ref, acc_ref,
                *, final_norm):
    f = pl.program_id(1)

    @pl.when(f == 0)
    def _():
        x = r_ref[...] + jnp.dot(a_ref[...], wo_ref[...], preferred_element_type=jnp.float32)
        xn_ref[...] = _rmsnorm(x, g_ref[...]).astype(xn_ref.dtype)
        acc_ref[...] = x

    u = jnp.dot(xn_ref[...], wu_ref[...], preferred_element_type=jnp.float32)
    a = jnp.square(jnp.maximum(u, 0.0)).astype(wd_ref.dtype)
    acc_ref[...] += jnp.dot(a, wd_ref[...], preferred_element_type=jnp.float32)

    @pl.when(f == pl.num_programs(1) - 1)
    def _():
        y = acc_ref[...]
        if final_norm:
            y = _rmsnorm(y, gf_ref[...])
        o_ref[...] = y


def _out_proj_mlp(a, w_out, res, g, w_up, w_down, g_final, *, final_norm, tm=1024, tf=1024):
    t, d = res.shape
    k = a.shape[1]
    ff = w_up.shape[1]
    assert t % tm == 0 and ff % tf == 0
    return pl.pallas_call(
        functools.partial(_mlp_kernel, final_norm=final_norm),
        out_shape=jax.ShapeDtypeStruct((t, d), jnp.float32),
        grid=(t // tm, ff // tf),
        in_specs=[pl.BlockSpec((tm, k), lambda i, f: (i, 0)),
                  pl.BlockSpec((k, d), lambda i, f: (0, 0)),
                  pl.BlockSpec((tm, d), lambda i, f: (i, 0)),
                  pl.BlockSpec((1, d), lambda i, f: (0, 0)),
                  pl.BlockSpec((d, tf), lambda i, f: (0, f)),
                  pl.BlockSpec((tf, d), lambda i, f: (f, 0)),
                  pl.BlockSpec((1, d), lambda i, f: (0, 0))],
        out_specs=pl.BlockSpec((tm, d), lambda i, f: (i, 0)),
        scratch_shapes=[pltpu.VMEM((tm, d), w_up.dtype), pltpu.VMEM((tm, d), jnp.float32)],
        compiler_params=_params(("parallel", "arbitrary")),
        name="out_proj_mlp",
    )(a, w_out, res, g.reshape(1, d), w_up, w_down, g_final.reshape(1, d))


def _rel_bucket_np(dist):
    n = np.maximum(dist, 0)
    max_exact = REL_BUCKETS // 2
    nf = np.maximum(n, 1).astype(np.float64)
    large = max_exact + (np.log(nf / max_exact) / math.log(REL_MAX_DIST / max_exact)
                         * (REL_BUCKETS - max_exact)).astype(np.int64)
    large = np.minimum(large, REL_BUCKETS - 1)
    return np.where(n < max_exact, n, large).astype(np.int32)


def _bias_table(rel_bias, dist, valid, shift):
    bucket = jnp.asarray(_rel_bucket_np(dist).reshape(1, -1))
    onehot = (bucket == jnp.arange(REL_BUCKETS, dtype=bucket.dtype).reshape(-1, 1)).astype(jnp.float32)
    tab = jnp.einsum("bh,bn->hn", rel_bias, onehot, precision=lax.Precision.HIGHEST)
    tab = tab.reshape((-1,) + dist.shape) - shift.reshape((-1,) + (1,) * dist.ndim)
    return jnp.where(jnp.asarray(valid)[None], tab, NEG).astype(jnp.float32)


def _moba_route_mask(gate_t, i):
    blkid = lax.broadcasted_iota(jnp.int32, gate_t.shape, 0)
    blk_f = blkid.astype(jnp.float32)
    valid = blkid < i
    g = jnp.where(valid, gate_t, -jnp.inf)
    sel = blkid == i
    for _ in range(MOBA_TOPK):
        mx = jnp.max(g, axis=0, keepdims=True)
        first = jnp.min(jnp.where(g == mx, blk_f, float(LANES)), axis=0, keepdims=True)
        pick = (blk_f == first) & valid
        sel = sel | pick
        g = jnp.where(pick, -jnp.inf, g)
    return jnp.where(sel, 0.0, NEG)


def _moba_kernel(q_ref, k_ref, v_ref, tnear_ref, o_ref, kaug_ref, vt_ref, kmrows_ref,
                 *, blk, nblk, group, ku):
    i = pl.program_id(2)
    half = LANES // 2
    pad = group * blk
    lane = lax.broadcasted_iota(jnp.int32, (blk, LANES), 1)
    lo = lane < half

    @pl.when(i == 0)
    def _():
        kmrows_ref[...] = jnp.zeros(kmrows_ref.shape, kmrows_ref.dtype)
        lane1 = lax.broadcasted_iota(jnp.int32, (1, LANES), 1)
        lane_p = lax.broadcasted_iota(jnp.int32, (pad, LANES), 1)
        kaug_ref[0, 0:pad, :] = jnp.where(lane_p == LANES - 1, 1.0, 0.0).astype(kaug_ref.dtype)
        kaug_ref[1, 0:pad, :] = jnp.where(lane_p == half - 1, 1.0, 0.0).astype(kaug_ref.dtype)
        vt_ref[0:pad // ku] = jnp.zeros((pad // ku,) + vt_ref.shape[1:], vt_ref.dtype)
        ones = jnp.ones((vt_ref.shape[2] - half, ku), vt_ref.dtype)

        def build(n, carry):
            rows = pl.ds(pl.multiple_of(n * blk, blk), blk)
            prows = pl.ds(pl.multiple_of(n * blk + pad, blk), blk)
            kn = k_ref[0, rows, :].astype(jnp.float32)
            kaug_ref[0, prows, :] = jnp.where(
                lo, kn, jnp.where(lane == half + n, 1.0, 0.0)).astype(kaug_ref.dtype)
            kaug_ref[1, prows, :] = jnp.where(
                lo, jnp.where(lane == n, 1.0, 0.0), kn).astype(kaug_ref.dtype)
            for t in range(blk // ku):
                vn = v_ref[0, pl.ds(pl.multiple_of(n * blk + t * ku, ku), ku), :]
                vn_t = vn.astype(jnp.float32).T.astype(vt_ref.dtype)
                unit = (n + group) * (blk // ku) + t
                for hh in range(2):
                    vt_ref[unit, hh, 0:half, :] = vn_t[hh * half:(hh + 1) * half]
                    vt_ref[unit, hh, half:, :] = ones
            mean = jnp.sum(kn, axis=0, keepdims=True) * (1.0 / blk)
            kmrows_ref[0, pl.ds(n, 1), :] = jnp.where(lane1 < half, mean, 0.0)
            kmrows_ref[1, pl.ds(n, 1), :] = jnp.where(lane1 < half, 0.0, mean)
            return carry

        lax.fori_loop(0, nblk, build, 0)

    dt = q_ref.dtype
    q_t = q_ref[0].astype(jnp.float32).T
    q_tb = q_t.astype(dt)
    q_aug = []
    for hh in range(2):
        gate_t = jnp.dot(kmrows_ref[hh].astype(dt), q_tb,
                         preferred_element_type=jnp.float32)
        route_t = _moba_route_mask(gate_t, i)
        q_h = q_t[hh * half:(hh + 1) * half]
        q_aug.append(jnp.concatenate([q_h, route_t] if hh == 0 else [route_t, q_h], axis=0).astype(dt))

    upb = blk // ku

    def fold_group(unit0, state, bias_ref=None):
        n_units = group * upb
        rows = pl.ds(pl.multiple_of(unit0 * ku, ku), n_units * ku)
        s = [jnp.dot(kaug_ref[hh, rows, :], q_aug[hh], preferred_element_type=jnp.float32)
             for hh in range(2)]
        state = list(state)
        unbiased = n_units if bias_ref is None else n_units - bias_ref.shape[1] // ku
        for t in range(n_units):
            for hh in range(2):
                m, acc = state[hh]
                s_t = s[hh][t * ku:(t + 1) * ku]
                if t >= unbiased:
                    s_t = s_t + bias_ref[hh, (t - unbiased) * ku:(t - unbiased + 1) * ku, :]
                m_new = jnp.maximum(m, jnp.max(s_t, axis=0, keepdims=True))
                p = jnp.exp2(s_t - m_new).astype(dt)
                acc = jnp.exp2(m - m_new) * acc + jnp.dot(vt_ref[unit0 + t, hh], p,
                                                          preferred_element_type=jnp.float32)
                state[hh] = (m_new, acc)
        return state

    start = (jnp.full((1, blk), NEG, jnp.float32), jnp.zeros((vt_ref.shape[2], blk), jnp.float32))
    state = fold_group((i + 1) * upb, [start, start], tnear_ref)

    def far(g, carry):
        st = fold_group((i + 1 - group * g) * upb, [carry[0:2], carry[2:4]])
        return tuple(st[0]) + tuple(st[1])

    n_far = (i + group) // group - 1
    out = lax.fori_loop(0, n_far // 2, lambda gp, c: far(2 * gp + 2, far(2 * gp + 1, c)),
                        tuple(state[0]) + tuple(state[1]))
    out = lax.cond(n_far % 2 == 1, lambda c: far(n_far, c), lambda c: c, out)
    o_t = jnp.concatenate([out[1][:half] / out[1][half:half + 1],
                           out[3][:half] / out[3][half:half + 1]], axis=0)
    o_ref[0] = o_t.T.astype(o_ref.dtype)


def _moba_in_weights(w_in, dt):
    nq = N_HEADS * HEAD_DIM
    return jnp.concatenate([w_in[:, :nq] * (LOG2E * HEAD_DIM ** -0.5), w_in[:, nq:]], axis=1).astype(dt)


def _moba_attention(qkv, rel_bias):
    b, s, _ = qkv.shape
    blk = MOBA_BLOCK
    group = MOBA_GROUP
    nblk = s // blk
    hp = N_HEADS // 2
    ku = MOBA_KEY_UNIT
    assert s % blk == 0 and nblk < LANES // 2 and 2 * HEAD_DIM == LANES and blk % ku == 0
    assert group >= 2
    kk = np.arange(blk)[:, None]
    qq = np.arange(blk)[None, :]
    shift = rel_bias[REL_BUCKETS - 1]
    t_own = _bias_table(rel_bias, qq - kk, qq >= kk, shift)
    t_adj = _bias_table(rel_bias, blk + qq - kk, np.ones((blk, blk), bool), shift)
    t_near = jnp.concatenate([t_adj, t_own], axis=1) * LOG2E
    return pl.pallas_call(
        functools.partial(_moba_kernel, blk=blk, nblk=nblk, group=group, ku=ku),
        out_shape=jax.ShapeDtypeStruct((b, s, N_HEADS * HEAD_DIM), qkv.dtype),
        grid=(b, hp, nblk),
        in_specs=[pl.BlockSpec((1, blk, LANES), lambda bb, p, i: (bb, i, p)),
                  pl.BlockSpec((1, s, LANES), lambda bb, p, i: (bb, 0, hp + p)),
                  pl.BlockSpec((1, s, LANES), lambda bb, p, i: (bb, 0, 2 * hp + p)),
                  pl.BlockSpec((2, 2 * blk, blk), lambda bb, p, i: (p, 0, 0))],
        out_specs=pl.BlockSpec((1, blk, LANES), lambda bb, p, i: (bb, i, p)),
        scratch_shapes=[pltpu.VMEM((2, s + group * blk, LANES), qkv.dtype),
                        pltpu.VMEM(((s + group * blk) // ku, 2, HEAD_DIM + MOBA_ONES_ROWS, ku), qkv.dtype),
                        pltpu.VMEM((2, LANES // 2, LANES), jnp.float32)],
        compiler_params=_params(("parallel", "parallel", "arbitrary")),
        name="moba_attention",
    )(qkv, qkv, qkv, t_near)


def _swa_kernel(q_ref, kp_ref, kc_ref, vp_ref, vc_ref, tab_ref, sink_ref, o_ref, *, blk, group):
    n = pl.program_id(1)
    half = LANES // 2
    dt = q_ref.dtype
    q_t = q_ref[0].astype(jnp.float32).T.astype(dt)
    kband = jnp.concatenate([kp_ref[0], kc_ref[0]], axis=0)
    vband = jnp.concatenate([vp_ref[0], vc_ref[0]], axis=0)
    v_t = vband.astype(jnp.float32).T.astype(dt)
    ones = jnp.ones((SWA_ONES_ROWS, 2 * blk), dt)
    zeros = jnp.zeros((half, group * blk), dt)
    tab_n = jnp.minimum(n, 1)

    outs = []
    for kv in range(SWA_KV_HEADS):
        heads = range(kv * group, (kv + 1) * group)
        qg = jnp.concatenate([q_t[h * half:(h + 1) * half] for h in heads], axis=1)
        qg = jnp.concatenate([qg, zeros] if kv == 0 else [zeros, qg], axis=0)
        s = jnp.dot(kband, qg, preferred_element_type=jnp.float32) + tab_ref[tab_n, kv]
        sink = sink_ref[kv]
        m = jnp.maximum(jnp.max(s, axis=0, keepdims=True), sink)
        p = jnp.exp2(s - m).astype(dt)
        v1 = jnp.concatenate([v_t[kv * half:(kv + 1) * half], ones], axis=0)
        acc = jnp.dot(v1, p, preferred_element_type=jnp.float32)
        o = acc[0:half] / (acc[half:half + 1] + jnp.exp2(sink - m))
        outs.extend(o[:, g * blk:(g + 1) * blk] for g in range(group))
    o_ref[0] = jnp.concatenate(outs, axis=0).T.astype(o_ref.dtype)


def _swa_in_weights(w_in, dt):
    nq = N_HEADS * HEAD_DIM
    return jnp.concatenate([w_in[:, :nq] * (LOG2E * HEAD_DIM ** -0.5), w_in[:, nq:]], axis=1).astype(dt)


def _swa_attention(proj, sinks, rel_bias):
    b, s, _ = proj.shape
    blk = SWA_BLOCK
    nb = s // blk
    group = N_HEADS // SWA_KV_HEADS
    nq = N_HEADS * HEAD_DIM
    assert s % blk == 0 and SWA_KV_HEADS * HEAD_DIM == LANES and SWA_WINDOW <= blk
    kcol = nq // LANES
    dist = blk + np.arange(blk)[None, :] - np.arange(2 * blk)[:, None]
    window = (dist >= 0) & (dist < SWA_WINDOW)
    first = window & (np.arange(2 * blk)[:, None] >= blk)
    zero = jnp.zeros_like(sinks)

    def wide(valid):
        tab = _bias_table(rel_bias, dist, valid, zero).reshape(SWA_KV_HEADS, group, 2 * blk, blk)
        return jnp.transpose(tab, (0, 2, 1, 3)).reshape(SWA_KV_HEADS, 2 * blk, group * blk) * LOG2E

    tabs = jnp.stack([wide(first), wide(window)])
    sink_w = jnp.broadcast_to(sinks.astype(jnp.float32).reshape(SWA_KV_HEADS, 1, group, 1) * LOG2E,
                              (SWA_KV_HEADS, 1, group, blk)).reshape(SWA_KV_HEADS, 1, group * blk)
    prev = lambda bb, n: (bb, jnp.maximum(n - 1, 0), kcol)
    cur = lambda bb, n: (bb, n, kcol)
    prev_v = lambda bb, n: (bb, jnp.maximum(n - 1, 0), kcol + 1)
    cur_v = lambda bb, n: (bb, n, kcol + 1)
    return pl.pallas_call(
        functools.partial(_swa_kernel, blk=blk, group=group),
        out_shape=jax.ShapeDtypeStruct((b, s, nq), proj.dtype),
        grid=(b, nb),
        in_specs=[pl.BlockSpec((1, blk, nq), lambda bb, n: (bb, n, 0)),
                  pl.BlockSpec((1, blk, LANES), prev),
                  pl.BlockSpec((1, blk, LANES), cur),
                  pl.BlockSpec((1, blk, LANES), prev_v),
                  pl.BlockSpec((1, blk, LANES), cur_v),
                  pl.BlockSpec(tabs.shape, lambda bb, n: (0, 0, 0, 0)),
                  pl.BlockSpec(sink_w.shape, lambda bb, n: (0, 0, 0))],
        out_specs=pl.BlockSpec((1, blk, nq), lambda bb, n: (bb, n, 0)),
        compiler_params=_params(("parallel", "arbitrary")),
        name="swa_attention",
    )(proj, proj, proj, proj, proj, tabs, sink_w)


def _dsa_proj_kernel(x_ref, g_ref, w_ref, wwt_ref, kvn_ref, wuk_ref,
                     ql_ref, c_ref, ct_ref, qi_ref, ki_ref, wt_ref, *, idx_scale):
    dt = w_ref.dtype
    nq = N_HEADS * HEAD_DIM
    r = DSA_KV_RANK
    ni = DSA_IDX_HEADS * DSA_IDX_DIM
    xn = _rmsnorm(x_ref[0], g_ref[...]).astype(dt)
    y = jnp.dot(xn, w_ref[...], preferred_element_type=jnp.float32)
    scale = LOG2E * HEAD_DIM ** -0.5
    for p in range(N_HEADS // 2):
        qp = y[:, p * LANES:(p + 1) * LANES].astype(dt)
        ql = jnp.dot(qp, wuk_ref[p], preferred_element_type=jnp.float32) * scale
        ql_ref[0, 2 * p] = ql[:, :r].astype(ql_ref.dtype)
        ql_ref[0, 2 * p + 1] = ql[:, r:].astype(ql_ref.dtype)
    c = _rmsnorm(y[:, nq:nq + r], kvn_ref[...])
    c_ref[0] = c.astype(c_ref.dtype)
    kt = ct_ref.shape[3]
    for t in range(ct_ref.shape[1]):
        ct_ref[0, t, 0:r, :] = c[t * kt:(t + 1) * kt].T.astype(ct_ref.dtype)
        ct_ref[0, t, r:, :] = jnp.ones((ct_ref.shape[2] - r, kt), ct_ref.dtype)
    qi_ref[0] = y[:, nq + r:nq + r + ni].astype(qi_ref.dtype)
    ki_ref[0] = y[:, nq + r + ni:nq + r + ni + LANES].astype(ki_ref.dtype)
    wt = lax.dot_general(wwt_ref[...], xn, _NT, preferred_element_type=jnp.float32)
    wt_ref[0] = wt * idx_scale


def _dsa_proj(x, g, w_in, kv_norm, w_uk, dt, *, tm=512):
    b, s, d = x.shape
    nq = N_HEADS * HEAD_DIM
    r = DSA_KV_RANK
    ni = DSA_IDX_HEADS * DSA_IDX_DIM
    di = DSA_IDX_DIM
    assert 2 * di == LANES and 2 * HEAD_DIM == LANES and s % tm == 0
    k_idx = w_in[:, nq + r + ni:nq + r + ni + di]
    w_main = jnp.concatenate([w_in[:, :nq + r + ni], k_idx, k_idx], axis=1).astype(dt)
    wwt = w_in[:, nq + r + ni + di:].T.astype(dt)
    uk = jnp.transpose(w_uk, (1, 2, 0)).reshape(N_HEADS // 2, 2, HEAD_DIM, r)
    z = jnp.zeros_like(uk[:, 0])
    wuk_bd = jnp.concatenate([jnp.concatenate([uk[:, 0], z], axis=2),
                              jnp.concatenate([z, uk[:, 1]], axis=2)], axis=1).astype(dt)
    nw = w_main.shape[1]
    idx_scale = DSA_IDX_HEADS ** -0.5 * DSA_IDX_DIM ** -0.5
    kt = DSA_TILE
    rp = r + DSA_ONES_ROWS
    assert tm % kt == 0
    return pl.pallas_call(
        functools.partial(_dsa_proj_kernel, idx_scale=idx_scale),
        out_shape=(jax.ShapeDtypeStruct((b, N_HEADS, s, r), dt),
                   jax.ShapeDtypeStruct((b, s, r), dt),
                   jax.ShapeDtypeStruct((b, s // kt, rp, kt), dt),
                   jax.ShapeDtypeStruct((b, s, ni), dt),
                   jax.ShapeDtypeStruct((b, s, LANES), dt),
                   jax.ShapeDtypeStruct((b, DSA_IDX_HEADS, s), jnp.float32)),
        grid=(b, s // tm),
        in_specs=[pl.BlockSpec((1, tm, d), lambda bb, i: (bb, i, 0)),
                  pl.BlockSpec((1, d), lambda bb, i: (0, 0)),
                  pl.BlockSpec((d, nw), lambda bb, i: (0, 0)),
                  pl.BlockSpec((DSA_IDX_HEADS, d), lambda bb, i: (0, 0)),
                  pl.BlockSpec((1, r), lambda bb, i: (0, 0)),
                  pl.BlockSpec((N_HEADS // 2, LANES, 2 * r), lambda bb, i: (0, 0, 0))],
        out_specs=(pl.BlockSpec((1, N_HEADS, tm, r), lambda bb, i: (bb, 0, i, 0)),
                   pl.BlockSpec((1, tm, r), lambda bb, i: (bb, i, 0)),
                   pl.BlockSpec((1, tm // kt, rp, kt), lambda bb, i: (bb, i, 0, 0)),
                   pl.BlockSpec((1, tm, ni), lambda bb, i: (bb, i, 0)),
                   pl.BlockSpec((1, tm, LANES), lambda bb, i: (bb, i, 0)),
                   pl.BlockSpec((1, DSA_IDX_HEADS, tm), lambda bb, i: (bb, 0, i))),
        compiler_params=_params(("parallel", "parallel")),
        name="dsa_proj",
    )(x, g.reshape(1, d), w_main, wwt, kv_norm.reshape(1, r), wuk_bd)


def _dsa_kernel(qi_ref, wt_ref, ki_ref, c_ref, ct_ref, ql_ref, tdiag_ref, tadj_ref, wuvt_ref, o_ref,
                hi_ref, lo_ref, acc_ref, m_ref, qlt_ref, *, tile, topk, s_len, hg):
    i = pl.program_id(1)
    half = LANES // 2
    nih = DSA_IDX_HEADS

    def tile_rows(j):
        return pl.ds(pl.multiple_of(j * tile, tile), tile)

    dt = qi_ref.dtype
    zeros = jnp.zeros((half, tile), jnp.float32)
    parts = []
    for pair in range(nih // 2):
        pair_t = qi_ref[0, :, pair * LANES:(pair + 1) * LANES].astype(jnp.float32).T
        parts.append(jnp.concatenate([pair_t[:half], zeros], axis=0))
        parts.append(jnp.concatenate([zeros, pair_t[half:]], axis=0))
    qs_t = jnp.concatenate(parts, axis=1).astype(dt)
    w_t = wt_ref[0]
    krow = lax.broadcasted_iota(jnp.int32, (tile, tile), 0)
    qcol = lax.broadcasted_iota(jnp.int32, (tile, tile), 1)

    def index_tile(j, carry):
        rows = tile_rows(j)
        rel = jnp.dot(ki_ref[0, rows, :], qs_t, preferred_element_type=jnp.float32)
        sc = jnp.zeros((tile, tile), jnp.float32)
        for h in range(nih):
            sc = sc + jnp.maximum(rel[:, h * tile:(h + 1) * tile], 0.0) * w_t[h:h + 1, :]
        sc = jnp.where(sc == 0.0, 0.0, sc)
        bits = lax.bitcast_convert_type(sc, jnp.int32)
        u = jnp.where(bits < 0, bits ^ jnp.int32(0x7FFFFFFF), bits)
        causal = (j * tile + krow) <= (i * tile + qcol)
        u = jnp.where(causal, u, jnp.int32(INT_MIN))
        hi_ref[rows, :] = lax.shift_right_arithmetic(u, I16_BITS).astype(jnp.int16)
        lo_ref[rows, :] = ((u & jnp.int32(I16_SPAN - 1)) - I16_SPAN // 2).astype(jnp.int16)
        return carry

    lax.fori_loop(0, (i + 1) // 2, lambda jp, c: index_tile(2 * jp + 1, index_tile(2 * jp, c)), 0)

    @pl.when((i + 1) % 2 == 1)
    def _():
        index_tile(i, 0)

    cu = DSA_COUNT_UNROLL
    i16_min = jnp.int16(-I16_SPAN // 2)
    for k in range(1, cu):
        @pl.when(i + k < s_len // tile)
        def _(k=k):
            hi_ref[tile_rows(i + k), :] = jnp.full((tile, tile), i16_min, jnp.int16)
            lo_ref[tile_rows(i + k), :] = jnp.full((tile, tile), i16_min, jnp.int16)

    def count(pred):
        def body(jg, acc):
            for k in range(cu):
                j = jg * cu + k
                ones = jnp.where(pred(j, tile_rows(j)), jnp.int16(1), jnp.int16(0))
                for r in range(0, tile, PACKED_SUBLANES):
                    acc = acc + ones[r:r + PACKED_SUBLANES]
            return acc
        acc = lax.fori_loop(0, (i + cu) // cu, body, jnp.zeros((PACKED_SUBLANES, tile), jnp.int16))
        return jnp.sum(acc.astype(jnp.float32), axis=0, keepdims=True)

    def to_i16(x):
        return x.astype(jnp.int16)

    few = (i * tile + lax.broadcasted_iota(jnp.int32, (1, tile), 1)) < topk

    def search16(ref, target):
        def unsettled(carry):
            bi, _, cnt_ans = carry
            pending = jnp.logical_not(few) & (cnt_ans != target)
            return (bi < I16_BITS) & (jnp.max(jnp.where(pending, 1.0, 0.0)) > 0.0)

        def bit_steps(carry):
            bi, ans, cnt_ans = carry
            for k in range(DSA_BITS_PER_CHECK):
                cand = ans | lax.shift_left(jnp.int32(1), I16_BITS - 1 - (bi + k))
                t16 = to_i16(cand - I16_SPAN // 2)
                cnt = count(lambda j, rows, t16=t16: ref[rows, :] >= t16)
                ok = cnt >= target
                ans, cnt_ans = jnp.where(ok, cand, ans), jnp.where(ok, cnt, cnt_ans)
            return bi + DSA_BITS_PER_CHECK, ans, cnt_ans

        _, ans, cnt = lax.while_loop(unsettled, bit_steps,
                                     (jnp.int32(0), jnp.zeros((1, tile), jnp.int32),
                                      jnp.zeros((1, tile), jnp.float32)))
        return ans - I16_SPAN // 2, cnt

    def count_above(ref, t):
        cnt = count(lambda j, rows: ref[rows, :] >= to_i16(jnp.minimum(t + 1, I16_SPAN // 2 - 1)))
        return jnp.where(t + 1 > I16_SPAN // 2 - 1, 0.0, cnt)

    topk_f = jnp.full((1, tile), float(topk), jnp.float32)
    thr_hi, cnt_hi = search16(hi_ref, topk_f)
    cnt_above = count_above(hi_ref, thr_hi)
    need_lo = topk_f - cnt_above
    h16 = to_i16(thr_hi)

    def keep_low(jg, carry):
        for k in range(cu):
            rows = tile_rows(jg * cu + k)
            lo_ref[rows, :] = jnp.where(hi_ref[rows, :] == h16, lo_ref[rows, :], i16_min)
        return carry

    lax.fori_loop(0, (i + cu) // cu, keep_low, 0)
    thr_lo, cnt_lo = search16(lo_ref, need_lo)
    cnt_lo = jnp.where(cnt_lo == 0.0, cnt_hi - cnt_above, cnt_lo)

    nbits = int(s_len).bit_length()
    cut_all = jnp.full((1, tile), 2 ** nbits - 1, jnp.int32)
    krow16 = lax.broadcasted_iota(jnp.int16, (tile, tile), 0)
    l16 = to_i16(thr_lo)

    def tie_search():
        need = need_lo - count_above(lo_ref, thr_lo)

        def cut_step(bi, cut):
            cand = cut | lax.shift_left(jnp.int32(1), nbits - 1 - bi)
            cnt = count(lambda j, rows: (lo_ref[rows, :] == l16) & (hi_ref[rows, :] == h16)
                        & (krow16 < to_i16(jnp.minimum(cand - j * tile, I16_SPAN // 2 - 1))))
            return jnp.where(cnt <= need, cand, cut)

        return lax.fori_loop(0, nbits, cut_step, jnp.zeros((1, tile), jnp.int32))

    tie = jnp.logical_not(few) & (cnt_lo > need_lo)
    cut = lax.cond(jnp.max(jnp.where(tie, 1.0, 0.0)) > 0.0, tie_search, lambda: cut_all)
    l16_sel = to_i16(jnp.where(thr_hi == -I16_SPAN // 2, I16_SPAN // 2 - 1, thr_lo))
    mask_0 = jnp.zeros((), c_ref.dtype)
    mask_neg = jnp.asarray(NEG, c_ref.dtype)

    m_ref[...] = jnp.full(m_ref.shape, NEG, jnp.float32)
    acc_ref[...] = jnp.zeros(acc_ref.shape, jnp.float32)
    for h in range(N_HEADS):
        qlt_ref[h // hg, :, (h % hg) * tile:(h % hg + 1) * tile] = (
            ql_ref[0, h].astype(jnp.float32).T.astype(qlt_ref.dtype))

    def attend_tile(j, table_ref, pen):
        rows = tile_rows(j)
        hi = hi_ref[rows, :]
        lo = lo_ref[rows, :]
        cut16 = to_i16(jnp.clip(cut - j * tile, -I16_SPAN // 2, I16_SPAN // 2 - 1))
        tie_m = jnp.where(krow16 < cut16, mask_0, mask_neg)
        low_m = jnp.where(lo > l16_sel, mask_0, jnp.where(lo == l16_sel, tie_m, mask_neg))
        mask16 = jnp.where(hi > h16, mask_0, jnp.where(hi == h16, low_m, mask_neg))
        maskadd = mask16.astype(jnp.float32) + pen
        c_t = c_ref[0, rows, :]
        ct_t = ct_ref[0, j]

        for g in range(N_HEADS // hg):
            logits = jnp.dot(c_t, qlt_ref[g], preferred_element_type=jnp.float32)
            for hl in range(hg):
                cols = slice(hl * tile, (hl + 1) * tile)
                s = logits[:, cols] + maskadd
                if table_ref is not None:
                    s = s + table_ref[g, :, cols]
                m_old = m_ref[g, :, cols]
                m_new = jnp.maximum(m_old, jnp.max(s, axis=0, keepdims=True))
                p = jnp.exp2(s - m_new).astype(ct_t.dtype)
                pv = jnp.dot(ct_t, p, preferred_element_type=jnp.float32)
                acc_ref[g, :, cols] = jnp.exp2(m_old - m_new) * acc_ref[g, :, cols] + pv
                m_ref[g, :, cols] = m_new

    n_far = jnp.maximum(i - 1, 0)

    def far_pair(jp, carry):
        attend_tile(2 * jp, None, 0.0)
        attend_tile(2 * jp + 1, None, 0.0)
        return carry

    lax.fori_loop(0, n_far // 2, far_pair, 0)

    @pl.when(n_far % 2 == 1)
    def _():
        attend_tile(n_far - 1, None, 0.0)

    attend_tile(jnp.maximum(i - 1, 0), tadj_ref, jnp.where(i >= 1, 0.0, NEG))
    attend_tile(i, tdiag_ref, 0.0)

    rank = wuvt_ref.shape[2]
    parts = []
    for h in range(N_HEADS):
        g, cols = h // hg, slice((h % hg) * tile, (h % hg + 1) * tile)
        ol = (acc_ref[g, 0:rank, cols] / acc_ref[g, rank:rank + 1, cols]).astype(wuvt_ref.dtype)
        parts.append(jnp.dot(wuvt_ref[h], ol, preferred_element_type=jnp.float32))
    o_ref[0] = jnp.concatenate(parts, axis=0).T.astype(o_ref.dtype)


def _dsa_attention(ql, c, ct, qi, ki2, wt, w_uv, rel_bias):
    b, s, r = c.shape
    tile = DSA_TILE
    nt = s // tile
    topk = min(DSA_TOPK_MAX, s // 4)
    assert s % (tile * DSA_COUNT_UNROLL) == 0 and ct.shape[3] == tile and s < I16_SPAN // 2
    rp = ct.shape[2]
    wuvt = jnp.transpose(w_uv, (1, 2, 0)).astype(c.dtype)
    kk = np.arange(tile)[:, None]
    qq = np.arange(tile)[None, :]
    shift = rel_bias[REL_BUCKETS - 1]
    hg = DSA_HEAD_GROUP
    ng = N_HEADS // hg

    def wide(tab):
        return jnp.transpose(tab.reshape(ng, hg, tile, tile), (0, 2, 1, 3)).reshape(ng, tile, hg * tile)

    t_diag = wide(_bias_table(rel_bias, qq - kk, qq >= kk, shift)) * LOG2E
    t_adj = wide(_bias_table(rel_bias, tile + qq - kk, np.ones((tile, tile), bool), shift)) * LOG2E
    ni = qi.shape[2]
    once = pl.Buffered(1)
    return pl.pallas_call(
        functools.partial(_dsa_kernel, tile=tile, topk=topk, s_len=s, hg=hg),
        out_shape=jax.ShapeDtypeStruct((b, s, N_HEADS * HEAD_DIM), c.dtype),
        grid=(b, nt),
        in_specs=[pl.BlockSpec((1, tile, ni), lambda bb, i: (bb, i, 0)),
                  pl.BlockSpec((1, DSA_IDX_HEADS, tile), lambda bb, i: (bb, 0, i)),
                  pl.BlockSpec((1, s, LANES), lambda bb, i: (bb, 0, 0), pipeline_mode=once),
                  pl.BlockSpec((1, s, r), lambda bb, i: (bb, 0, 0), pipeline_mode=once),
                  pl.BlockSpec((1, nt, rp, tile), lambda bb, i: (bb, 0, 0, 0), pipeline_mode=once),
                  pl.BlockSpec((1, N_HEADS, tile, r), lambda bb, i: (bb, 0, i, 0)),
                  pl.BlockSpec((ng, tile, hg * tile), lambda bb, i: (0, 0, 0), pipeline_mode=once),
                  pl.BlockSpec((ng, tile, hg * tile), lambda bb, i: (0, 0, 0), pipeline_mode=once),
                  pl.BlockSpec((N_HEADS, HEAD_DIM, r), lambda bb, i: (0, 0, 0), pipeline_mode=once)],
        out_specs=pl.BlockSpec((1, tile, N_HEADS * HEAD_DIM), lambda bb, i: (bb, i, 0)),
        scratch_shapes=[pltpu.VMEM((s, tile), jnp.int16),
                        pltpu.VMEM((s, tile), jnp.int16),
                        pltpu.VMEM((ng, rp, hg * tile), jnp.float32),
                        pltpu.VMEM((ng, 1, hg * tile), jnp.float32),
                        pltpu.VMEM((ng, r, hg * tile), c.dtype)],
        compiler_params=_params(("parallel", "arbitrary")),
        name="dsa_attention",
    )(qi, wt, ki2, c, ct, ql, t_diag, t_adj, wuvt)


def _dsa_mixer(h, g, w_in, kv_norm, w_uk, w_uv, rel_bias, mxu_dtype):
    ql, c, ct, qi, ki2, wt = _dsa_proj(h, g, w_in, kv_norm, w_uk, mxu_dtype)
    return _dsa_attention(ql, c, ct, qi, ki2, wt, w_uv, rel_bias)


def kernel(x, rel_bias, norm_mix, norm_mlp, mlp_up, mlp_down, a_w_in, a_w_out, b_w_in, b_sinks, b_w_out,
           c_w_in, c_kv_norm, c_w_uk, c_w_uv, c_w_out, final_norm):
    b, s, d = x.shape
    t = b * s
    dt = MXU_DTYPE
    depth = norm_mix.shape[0]
    h = x.reshape(t, d)
    for i in range(depth):
        kind, j = i % 3, i // 3
        if kind == 0:
            qkv = _norm_proj(h, norm_mix[i], _moba_in_weights(a_w_in[j], dt)).reshape(b, s, -1)
            o, w_out = _moba_attention(qkv, rel_bias), a_w_out[j]
        elif kind == 1:
            proj = _norm_proj(h, norm_mix[i], _swa_in_weights(b_w_in[j], dt)).reshape(b, s, -1)
            o, w_out = _swa_attention(proj, b_sinks[j], rel_bias), b_w_out[j]
        else:
            o = _dsa_mixer(h.reshape(b, s, d), norm_mix[i], c_w_in[j], c_kv_norm[j], c_w_uk[j],
                           c_w_uv[j], rel_bias, dt)
            w_out = c_w_out[j]
        h = _out_proj_mlp(o.reshape(t, -1), w_out.astype(dt), h, norm_mlp[i], mlp_up[i].astype(dt),
                          mlp_down[i].astype(dt), final_norm, final_norm=(i == depth - 1))
    return h.reshape(b, s, d)
```
